```python
import math
import jax, jax.numpy as jnp
from jax import lax
import numpy as np

D_MODEL = 1024
BATCH = 8
SEQ = 2048
DEPTH = 2
DEC_BATCH = 128
DEC_SEQ = 8
PAST_LEN = 16384
PAGE_SIZE = 128

BRANCH_W = D_MODEL // 2
H_A = 4
DH_A = BRANCH_W // H_A
H_B = 4
DH_B = BRANCH_W // H_B
H_C = 4
DH_C = BRANCH_W // H_C
W_D = BRANCH_W
CONV_W = 3
N_BRANCH = 4
D_FF = 4 * D_MODEL
CHUNK = 64
ROPE_BASE = 10000.0
LN_EPS = 1e-5
NEG_BIG = -1e30
GATE_CLAMP = 1.0 - 1e-6
ALPHA = (2 * DEPTH) ** 0.25
BETA = (8 * DEPTH) ** -0.25
SPLIT_SIZES = (BRANCH_W,) * 4 + (H_A, H_A) + (BRANCH_W,) * 4 + (BRANCH_W,) * 4 + (BRANCH_W,) * 3
P_IN = 15 * BRANCH_W + 2 * H_A

kernel_name = 'hybrid_mlstm_ret_hgrn2_conv_step'


def _split_cols(z):
    parts, off = [], 0
    for s in SPLIT_SIZES:
        parts.append(z[..., off:off + s])
        off += s
    return parts


def _layer_norm(x, g, b):
    xf = x.astype(jnp.float32)
    mu = jnp.mean(xf, -1, keepdims=True)
    xc = xf - mu
    var = jnp.mean(jnp.square(xc), -1, keepdims=True)
    return xc * lax.rsqrt(var + LN_EPS) * g + b


def _head_rms(h, w):
    hn = h * lax.rsqrt(jnp.mean(jnp.square(h), -1, keepdims=True) + LN_EPS)
    return hn.reshape(h.shape[:2] + (-1,)) * w


def _head_ln(h, w):
    hc = h - jnp.mean(h, -1, keepdims=True)
    hn = hc * lax.rsqrt(jnp.mean(jnp.square(hc), -1, keepdims=True) + LN_EPS)
    return hn.reshape(h.shape[:2] + (-1,)) * w


def _rope(x, pos):
    half = x.shape[-1] // 2
    inv = ROPE_BASE ** (-jnp.arange(half, dtype=jnp.float32) / half)
    ang = pos[:, None] * inv[None, :]
    cos = jnp.cos(ang)[None, :, None, :]
    sin = jnp.sin(ang)[None, :, None, :]
    x1, x2 = x[..., :half], x[..., half:]
    return jnp.concatenate([x1 * cos - x2 * sin, x1 * sin + x2 * cos], -1)


def _to_chunks(a, c):
    b, t = a.shape[:2]
    return jnp.moveaxis(a.reshape((b, t // c, c) + a.shape[2:]), 1, 0)


def _from_chunks(a):
    nc, b, c = a.shape[:3]
    return jnp.moveaxis(a, 0, 1).reshape((b, nc * c) + a.shape[3:])


def _mlstm_chunked(q, k, v, ig, lf, C0, n0, m0):
    c = math.gcd(q.shape[1], CHUNK)
    xs = tuple(_to_chunks(a, c) for a in (q, k, v, ig, lf))
    causal = jnp.tril(jnp.ones((c, c), bool))

    def step(carry, inp):
        C, n, m = carry
        qc, kc, vc, ic, fc = inp
        b = jnp.swapaxes(jnp.cumsum(fc, axis=1), 1, 2)
        ic = jnp.swapaxes(ic, 1, 2)
        a = b + m[..., None]
        D = b[..., :, None] - b[..., None, :] + ic[..., None, :]
        D = jnp.where(causal, D, NEG_BIG)
        m_t = jnp.maximum(a, jnp.max(D, -1))
        w_inter = jnp.exp(a - m_t)
        s = jnp.einsum('bthd,bshd->bhts', qc, kc) * jnp.exp(D - m_t[..., None])
        num = (jnp.einsum('bhts,bshd->bthd', s, vc)
               + jnp.einsum('bthk,bhkv->bthv', qc, C) * jnp.swapaxes(w_inter, 1, 2)[..., None])
        den = jnp.sum(s, -1) + jnp.einsum('bthk,bhk->bht', qc, n) * w_inter
        den = jnp.swapaxes(jnp.maximum(jnp.abs(den), jnp.exp(-m_t)), 1, 2)
        h = num / den[..., None]
        bL = b[..., -1]
        g_last = bL[..., None] - b + ic
        m_new = jnp.maximum(bL + m, jnp.max(g_last, -1))
        w_s = jnp.exp(g_last - m_new[..., None])
        dec = jnp.exp(bL + m - m_new)
        C_new = dec[..., None, None] * C + jnp.einsum('bhs,bshk,bshv->bhkv', w_s, kc, vc)
        n_new = dec[..., None] * n + jnp.einsum('bhs,bshk->bhk', w_s, kc)
        return (C_new, n_new, m_new), h

    (C, n, m), hs = lax.scan(step, (C0, n0, m0), xs)
    return _from_chunks(hs), C, n, m


def _retention_chunked(q, k, v, log_gamma, S0):
    c = math.gcd(q.shape[1], CHUNK)
    xs = tuple(_to_chunks(a, c) for a in (q, k, v))
    idx = jnp.arange(c, dtype=jnp.float32)
    rel = idx[:, None] - idx[None, :]
    dec = jnp.where(rel >= 0, jnp.exp(log_gamma[:, None, None] * jnp.maximum(rel, 0.0)), 0.0)
    inter = jnp.swapaxes(jnp.exp(log_gamma[:, None] * (idx + 1.0)[None, :]), 0, 1)[None, :, :, None]
    to_end = jnp.exp(log_gamma[:, None] * (c - 1.0 - idx)[None, :])
    chunk_dec = jnp.exp(log_gamma * c)[None, :, None, None]

    def step(S, inp):
        qc, kc, vc = inp
        s = jnp.einsum('bthd,bshd->bhts', qc, kc) * dec
        o = jnp.einsum('bhts,bshv->bthv', s, vc) + jnp.einsum('bthk,bhkv->bthv', qc, S) * inter
        S_new = chunk_dec * S + jnp.einsum('hs,bshk,bshv->bhkv', to_end, kc, vc)
        return S_new, o

    S, os_ = lax.scan(step, S0, xs)
    return _from_chunks(os_), S


def _gla_chunked(q, k, v, g, S0):
    c = math.gcd(q.shape[1], CHUNK)
    xs = tuple(_to_chunks(a, c) for a in (q, k, v, g))
    causal = jnp.tril(jnp.ones((c, c), bool))[None, :, :, None, None]

    def step(S, inp):
        qc, kc, vc, gc = inp
        Bc = jnp.cumsum(gc, axis=1)
        diff = jnp.where(causal, Bc[:, :, None] - Bc[:, None, :], 0.0)
        wdec = jnp.where(causal, jnp.exp(diff), 0.0)
        A = jnp.einsum('bthk,bshk,btshk->bhts', qc, kc, wdec)
        o = jnp.einsum('bhts,bshv->bthv', A, vc) + jnp.einsum('bthk,bhkv->bthv', qc * jnp.exp(Bc), S)
        BL = Bc[:, -1]
        S_new = jnp.exp(BL)[..., None] * S + jnp.einsum('bshk,bshv->bhkv', kc * jnp.exp(BL[:, None] - Bc), vc)
        return S_new, o

    S, os_ = lax.scan(step, S0, xs)
    return _from_chunks(os_), S


def _short_conv(u, buf, w):
    t = u.shape[1]
    full = jnp.concatenate([buf.astype(u.dtype), u], axis=1)
    y = full[:, 0:t] * w[0]
    for j in range(1, CONV_W):
        y = y + full[:, j:j + t] * w[j]
    return y, full[:, -(CONV_W - 1):]


def _layer(x, pos, st, p, lb):
    (w_in, b_gate_a, norm_a, norm_b, norm_c, conv_w, w_branch, w_gate, b_gate,
     w_out, ln1_g, ln1_b, ln2_g, ln2_b, w_up, w_down) = p
    C0, n0, m0, Sr0, Sh0, buf0 = st
    f32 = jnp.float32
    bsz, t, _ = x.shape
    z = jnp.einsum('btd,dp->btp', x, w_in).astype(f32)
    (q_a, k_a, v_a, o_a, i_a, f_a, q_b, k_b, v_b, g_b,
     q_c, f_c, i_c, g_c, x_d, bg_d, cg_d) = _split_cols(z)
    heads = lambda a, h: a.reshape(bsz, t, h, -1)
    ig = i_a + b_gate_a[:H_A]
    lf = jax.nn.log_sigmoid(f_a + b_gate_a[H_A:])
    hA, C1, n1, m1 = _mlstm_chunked(heads(q_a, H_A), heads(k_a, H_A) * DH_A ** -0.5, heads(v_a, H_A),
                                    ig, lf, C0, n0, m0)
    yA = jax.nn.sigmoid(o_a) * _head_rms(hA, norm_a)
    log_gamma = jnp.log(1.0 - 2.0 ** (-5.0 - jnp.arange(H_B, dtype=f32)))
    qb = _rope(heads(q_b, H_B), pos)
    kb = _rope(heads(k_b, H_B), pos) * DH_B ** -0.5
    hB, Sr1 = _retention_chunked(qb, kb, heads(v_b, H_B), log_gamma, Sr0)
    yB = jax.nn.silu(g_b) * _head_ln(hB, norm_b)
    lbh = lb.reshape(H_C, DH_C)
    fc = heads(f_c, H_C)
    kc = (1.0 - lbh) * jax.nn.sigmoid(-fc)
    log_f = jnp.log1p(-jnp.minimum(kc, GATE_CLAMP))
    hC, Sh1 = _gla_chunked(jax.nn.silu(heads(q_c, H_C)), kc, heads(i_c, H_C), log_f, Sh0)
    yC = jax.nn.silu(g_c) * _head_rms(hC, norm_c)
    yconv, buf1 = _short_conv(cg_d * x_d, buf0, conv_w)
    yD = bg_d * yconv
    branches = jnp.stack([yA, yB, yC, yD], axis=2)
    proj = jnp.einsum('btnw,nwd->btnd', branches, w_branch)
    gates = jax.nn.sigmoid(jnp.einsum('btd,de->bte', x, w_gate) + b_gate).reshape(bsz, t, N_BRANCH, D_MODEL)
    mix = jnp.einsum('btd,de->bte', jnp.sum(gates * proj, axis=2), w_out)
    h = _layer_norm(ALPHA * x + mix, ln1_g, ln1_b)
    ff = jnp.einsum('btf,fd->btd', jnp.square(jax.nn.relu(jnp.einsum('btd,df->btf', h, w_up))), w_down)
    out = _layer_norm(ALPHA * h + ff, ln2_g, ln2_b).astype(x.dtype)
    return out, (C1, n1, m1, Sr1, Sh1, buf1)


def _trunk(x, pos, states, params, lb_all):
    outs = ([], [], [], [], [], [])
    for l in range(DEPTH):
        x, new = _layer(x, pos, tuple(s[l] for s in states), tuple(p[l] for p in params), lb_all[l])
        for lst, s in zip(outs, new):
            lst.append(s)
    return x, tuple(jnp.stack(lst) for lst in outs)


def setup_inputs(seed: int = 0) -> dict:
    key = jax.random.key(seed)
    ks = jax.random.split(key, 32)
    f32 = jnp.float32
    nrm = lambda k, s: jax.random.normal(k, s, f32)
    return {
        'x_prompt': nrm(ks[0], (BATCH, SEQ, D_MODEL)),
        'x_sample': nrm(ks[1], (DEC_BATCH, DEC_SEQ, D_MODEL)),
        'state_mlstm_C': 0.5 * nrm(ks[2], (DEPTH, DEC_BATCH, H_A, DH_A, DH_A)),
        'state_mlstm_n': 0.5 * nrm(ks[3], (DEPTH, DEC_BATCH, H_A, DH_A)),
        'state_mlstm_m': 2.0 + nrm(ks[4], (DEPTH, DEC_BATCH, H_A)),
        'state_ret': nrm(ks[5], (DEPTH, DEC_BATCH, H_B, DH_B, DH_B)),
        'state_hgrn': 0.5 * nrm(ks[6], (DEPTH, DEC_BATCH, H_C, DH_C, DH_C)),
        'state_conv': nrm(ks[7], (DEPTH, DEC_BATCH, CONV_W - 1, W_D)),
        'w_in': nrm(ks[8], (DEPTH, D_MODEL, P_IN)) * D_MODEL ** -0.5,
        'b_mlstm_gate': jnp.concatenate([0.1 * nrm(ks[9], (DEPTH, H_A)),
                                         jnp.linspace(3.0, 6.0, H_A, dtype=f32)[None, :] + 0.1 * nrm(ks[10], (DEPTH, H_A))], axis=-1),
        'norm_a': 1.0 + 0.05 * nrm(ks[11], (DEPTH, BRANCH_W)),
        'norm_b': 1.0 + 0.05 * nrm(ks[12], (DEPTH, BRANCH_W)),
        'norm_c': 1.0 + 0.05 * nrm(ks[13], (DEPTH, BRANCH_W)),
        'lb_c': 0.5 * nrm(ks[14], (DEPTH, H_C * DH_C)),
        'conv_w': nrm(ks[15], (DEPTH, CONV_W, W_D)) * CONV_W ** -0.5,
        'w_branch': nrm(ks[16], (DEPTH, N_BRANCH, BRANCH_W, D_MODEL)) * (BRANCH_W ** -0.5 * BETA),
        'w_gate': nrm(ks[17], (DEPTH, D_MODEL, N_BRANCH * D_MODEL)) * D_MODEL ** -0.5,
        'b_gate': 0.1 * nrm(ks[18], (DEPTH, N_BRANCH * D_MODEL)),
        'w_out': nrm(ks[19], (DEPTH, D_MODEL, D_MODEL)) * (D_MODEL ** -0.5 * BETA),
        'ln1_g': 1.0 + 0.05 * nrm(ks[20], (DEPTH, D_MODEL)),
        'ln1_b': 0.02 * nrm(ks[21], (DEPTH, D_MODEL)),
        'ln2_g': 1.0 + 0.05 * nrm(ks[22], (DEPTH, D_MODEL)),
        'ln2_b': 0.02 * nrm(ks[23], (DEPTH, D_MODEL)),
        'w_up': nrm(ks[24], (DEPTH, D_MODEL, D_FF)) * D_MODEL ** -0.5,
        'w_down': nrm(ks[25], (DEPTH, D_FF, D_MODEL)) * (D_FF ** -0.5 * BETA),
    }


def reference(x_prompt, x_sample, state_mlstm_C, state_mlstm_n, state_mlstm_m, state_ret, state_hgrn,
              state_conv, w_in, b_mlstm_gate, norm_a, norm_b, norm_c, lb_c, conv_w, w_branch, w_gate,
              b_gate, w_out, ln1_g, ln1_b, ln2_g, ln2_b, w_up, w_down):
    f32 = jnp.float32
    params = (w_in, b_mlstm_gate, norm_a, norm_b, norm_c, conv_w, w_branch, w_gate, b_gate,
              w_out, ln1_g, ln1_b, ln2_g, ln2_b, w_up, w_down)
    lb_sm = jax.nn.softmax(lb_c.astype(f32), axis=0)
    lb_all = jnp.cumsum(lb_sm, axis=0) - lb_sm[0]
    bp, tp = x_prompt.shape[:2]
    init = (jnp.zeros((DEPTH, bp, H_A, DH_A, DH_A), f32), jnp.zeros((DEPTH, bp, H_A, DH_A), f32),
            jnp.zeros((DEPTH, bp, H_A), f32), jnp.zeros((DEPTH, bp, H_B, DH_B, DH_B), f32),
            jnp.zeros((DEPTH, bp, H_C, DH_C, DH_C), f32), jnp.zeros((DEPTH, bp, CONV_W - 1, W_D), f32))
    pos_p = jnp.arange(tp, dtype=f32)
    y_prompt, p_states = _trunk(x_prompt, pos_p, init, params, lb_all)
    p_mlstm_C, p_mlstm_n, p_mlstm_m, p_ret, p_hgrn, p_conv = p_states
    ts = x_sample.shape[1]
    pos_s = PAST_LEN + jnp.arange(ts, dtype=f32)
    carried = (state_mlstm_C.astype(f32), state_mlstm_n.astype(f32), state_mlstm_m.astype(f32),
               state_ret.astype(f32), state_hgrn.astype(f32), state_conv.astype(f32))
    y_sample, s_states = _trunk(x_sample, pos_s, carried, params, lb_all)
    s_mlstm_C, s_mlstm_n, s_mlstm_m, s_ret, s_hgrn, s_conv = s_states
    return (y_prompt, y_sample, p_mlstm_C, p_mlstm_n, p_mlstm_m, p_ret, p_hgrn, p_conv,
            s_mlstm_C, s_mlstm_n, s_mlstm_m, s_ret, s_hgrn, s_conv)
```

```python
import functools
import math

import jax
import jax.numpy as jnp
from jax import lax
from jax.experimental import pallas as pl
from jax.experimental.pallas import tpu as pltpu

F32 = jnp.float32
BF16 = jnp.bfloat16

D_MODEL = 1024
DEPTH = 2
BRANCH_W = D_MODEL // 2
N_HEADS = 4
HEAD_DIM = BRANCH_W // N_HEADS
CONV_W = 3
N_BRANCH = 4
D_FF = 4 * D_MODEL
CHUNK = 64
ROPE_BASE = 10000.0
LN_EPS = 1e-5
NEG_BIG = -1e30
GATE_CLAMP = 1.0 - 1e-6
ALPHA = (2 * DEPTH) ** 0.25
PAST_LEN = 16384
K_SCALE = HEAD_DIM ** -0.5

LANES = 128
SUBLANES = 8
VMEM_BYTES = 64 * 1024 * 1024

(QA, KA, VA, OA, QB, KB, VB, GB, QC, FC, IC, GC, XD, BGD, CGD) = [i * BRANCH_W for i in range(15)]
GATE_OFF = 15 * BRANCH_W
P_PAD = GATE_OFF + LANES
Y_W = N_BRANCH * BRANCH_W

GLA_SUB = 16


def _vmem_limit(est_bytes):
    return int(min(est_bytes + est_bytes // 4 + (4 << 20), VMEM_BYTES - (6 << 20)))


def _dot(a, b):
    return jnp.dot(a.astype(BF16), b.astype(BF16), preferred_element_type=F32)


def _dot_nt(a, b):
    return lax.dot_general(a.astype(BF16), b.astype(BF16), (((1,), (1,)), ((), ())),
                           preferred_element_type=F32)


def _dot_tn(a, b):
    return lax.dot_general(a.astype(BF16), b.astype(BF16), (((0,), (0,)), ((), ())),
                           preferred_element_type=F32)


def _split3(x):
    hi = x.astype(BF16)
    r = x - hi.astype(F32)
    mid = r.astype(BF16)
    lo = (r - mid.astype(F32)).astype(BF16)
    return hi, mid, lo


def _dot_exact_rhs(a01, x):
    hi, mid, lo = _split3(x)
    d = lambda p: jnp.dot(a01, p, preferred_element_type=F32)
    return (d(hi) + d(mid)) + d(lo)


def _dot_nt_exact_rhs(a01, x):
    hi, mid, lo = _split3(x)
    d = lambda p: lax.dot_general(a01, p, (((1,), (1,)), ((), ())), preferred_element_type=F32)
    return (d(hi) + d(mid)) + d(lo)


def _silu(x):
    return x * jax.nn.sigmoid(x)


def _inproj_kernel(x_ref, w_ref, z_ref):
    xb = x_ref[...].astype(BF16)
    for off in range(0, P_PAD, BRANCH_W):
        wd = min(BRANCH_W, P_PAD - off)
        z_ref[:, off:off + wd] = jnp.dot(xb, w_ref[:, off:off + wd], preferred_element_type=F32)


def _inproj(x, w, tm):
    n = x.shape[0]
    est = D_MODEL * P_PAD * 2 * 2 + 2 * tm * P_PAD * 4 + 2 * tm * D_MODEL * 4
    return pl.pallas_call(
        _inproj_kernel,
        grid=(n // tm,),
        in_specs=[pl.BlockSpec((tm, D_MODEL), lambda i: (i, 0)),
                  pl.BlockSpec((D_MODEL, P_PAD), lambda i: (0, 0))],
        out_specs=pl.BlockSpec((tm, P_PAD), lambda i: (i, 0)),
        out_shape=jax.ShapeDtypeStruct((n, P_PAD), F32),
        compiler_params=pltpu.CompilerParams(dimension_semantics=("arbitrary",),
                                             vmem_limit_bytes=_vmem_limit(est)),
        name="inproj",
    )(x, w)


class _State:
    def __init__(self, get, put):
        self.get = get
        self.put = put


def _mixer_chunk(zc, c, sub, st, cst, prm, put_y):
    tril, causal, sel8 = cst["tril"], cst["causal"], cst["sel8"]

    gp = zc(GATE_OFF, LANES) + prm["bias_if"]
    lf = jnp.minimum(gp, 0.0) - jnp.log1p(jnp.exp(-jnp.abs(gp)))
    b_all = _dot_exact_rhs(tril, lf)
    gp_t = _dot_nt_exact_rhs(sel8, gp)
    b_t = _dot_nt_exact_rhs(sel8, b_all)
    ya = []
    for h in range(N_HEADS):
        hs = h * HEAD_DIM
        q = zc(QA + hs, HEAD_DIM)
        k = zc(KA + hs, HEAD_DIM) * K_SCALE
        v = zc(VA + hs, HEAD_DIM)
        b_col = b_all[:, N_HEADS + h:N_HEADS + h + 1]
        i_col = gp[:, h:h + 1]
        b_row = b_t[N_HEADS + h:N_HEADS + h + 1, :]
        i_row = gp_t[h:h + 1, :]
        m_prev = st.get("m", h)[:, 0:1]
        c_prev = st.get("C", h)
        n_prev = st.get("n", h)
        a_col = b_col + m_prev
        dmat = jnp.where(causal, (b_col - b_row) + i_row, NEG_BIG)
        m_t = jnp.maximum(a_col, jnp.max(dmat, axis=1, keepdims=True))
        w_inter = jnp.exp(a_col - m_t)
        s = _dot_nt(q, k) * jnp.exp(dmat - m_t)
        num = _dot(s, v) + _dot(q, c_prev) * w_inter
        den = jnp.sum(s, axis=1, keepdims=True) + jnp.sum(q * n_prev, axis=1, keepdims=True) * w_inter
        den = jnp.maximum(jnp.abs(den), jnp.exp(-m_t))
        hh = num / den
        hn = hh * lax.rsqrt(jnp.mean(hh * hh, axis=1, keepdims=True) + LN_EPS)
        ya.append(jax.nn.sigmoid(zc(OA + hs, HEAD_DIM)) * (hn * prm["norm_a"][:, hs:hs + HEAD_DIM]))
        b_last = b_col[c - 1:c, :]
        g_col = (b_last - b_col) + i_col
        m_new = jnp.maximum(b_last + m_prev, jnp.max(g_col, axis=0, keepdims=True))
        w_s = jnp.exp(g_col - m_new)
        dec = jnp.exp(b_last + m_prev - m_new)
        kw = k * w_s
        st.put("C", h, dec * c_prev + _dot_tn(kw, v))
        st.put("n", h, dec * n_prev + jnp.sum(kw, axis=0, keepdims=True))
        st.put("m", h, jnp.broadcast_to(m_new, (1, LANES)))
    put_y(0, jnp.concatenate(ya, axis=1))

    cos2, sin2 = cst["cos2"], cst["sin2"]
    rope = lambda x: x * cos2 + pltpu.roll(x, HEAD_DIM // 2, axis=1) * sin2
    yb = []
    for h in range(N_HEADS):
        hs = h * HEAD_DIM
        qb = rope(zc(QB + hs, HEAD_DIM))
        kb = rope(zc(KB + hs, HEAD_DIM)) * K_SCALE
        vb = zc(VB + hs, HEAD_DIM)
        s_prev = st.get("Sr", h)
        s = _dot_nt(qb, kb) * cst["ret_dec"](h)
        o = _dot(s, vb) + _dot(qb, s_prev) * cst["ret_inter"][:, h:h + 1]
        st.put("Sr", h, cst["ret_cdec"][h:h + 1, :] * s_prev + _dot_tn(kb * cst["ret_toend"][:, h:h + 1], vb))
        oc = o - jnp.mean(o, axis=1, keepdims=True)
        on = oc * lax.rsqrt(jnp.mean(oc * oc, axis=1, keepdims=True) + LN_EPS)
        yb.append(_silu(zc(GB + hs, HEAD_DIM)) * (on * prm["norm_b"][:, hs:hs + HEAD_DIM]))
    put_y(BRANCH_W, jnp.concatenate(yb, axis=1))

    kc_all = (1.0 - prm["lb"]) * jax.nn.sigmoid(-zc(FC, BRANCH_W))
    g_all = jnp.log1p(-jnp.minimum(kc_all, GATE_CLAMP))
    bc_all = _dot_exact_rhs(tril, g_all)
    lane_i = lax.broadcasted_iota(jnp.int32, (sub, c), 1)
    row_i = lax.broadcasted_iota(jnp.int32, (sub, c), 0)
    yc = []
    for h in range(N_HEADS):
        hs = h * HEAD_DIM
        qg = _silu(zc(QC + hs, HEAD_DIM))
        kc = kc_all[:, hs:hs + HEAD_DIM]
        vv = zc(IC + hs, HEAD_DIM)
        bc = bc_all[:, hs:hs + HEAD_DIM]
        sh_t = st.get("ShT", h)
        blocks = []
        for r0 in range(0, c, sub):
            bc_i = bc[r0:r0 + sub, :]
            q_i = qg[r0:r0 + sub, :]
            if r0 > 0:
                ref = bc[r0 - 1:r0, :]
                qt = q_i * jnp.exp(bc_i - ref)
                kt = kc * jnp.exp(jnp.minimum(ref - bc, 0.0))
                a_i = jnp.where(lane_i < r0, _dot_nt(qt, kt), 0.0)
            else:
                a_i = jnp.zeros((sub, c), F32)
            for s_ in range(sub):
                w = jnp.exp(jnp.minimum(bc_i - bc[r0 + s_:r0 + s_ + 1, :], 0.0))
                col = jnp.sum((q_i * kc[r0 + s_:r0 + s_ + 1, :]) * w, axis=1, keepdims=True)
                a_i = jnp.where(lane_i == r0 + s_, jnp.where(row_i >= s_, col, 0.0), a_i)
            blocks.append(a_i)
        a = blocks[0] if len(blocks) == 1 else jnp.concatenate(blocks, axis=0)
        o = _dot(a, vv) + _dot_nt(qg * jnp.exp(bc), sh_t)
        b_last = bc[c - 1:c, :]
        st.put("ShT", h, jnp.exp(b_last) * sh_t + _dot_tn(vv, kc * jnp.exp(b_last - bc)))
        on = o * lax.rsqrt(jnp.mean(o * o, axis=1, keepdims=True) + LN_EPS)
        yc.append(_silu(zc(GC + hs, HEAD_DIM)) * (on * prm["norm_c"][:, hs:hs + HEAD_DIM]))
    put_y(2 * BRANCH_W, jnp.concatenate(yc, axis=1))

    u = zc(CGD, BRANCH_W) * zc(XD, BRANCH_W)
    buf = st.get("buf", 0)
    row = lax.broadcasted_iota(jnp.int32, (c, BRANCH_W), 0)
    u1 = jnp.where(row == 0, buf[1:2, :], pltpu.roll(u, 1, axis=0))
    u2 = jnp.where(row == 0, buf[0:1, :], jnp.where(row == 1, buf[1:2, :], pltpu.roll(u, 2, axis=0)))
    cw = prm["conv_w"]
    yconv = (u2 * cw[0:1, :] + u1 * cw[1:2, :]) + u * cw[2:3, :]
    put_y(3 * BRANCH_W, zc(BGD, BRANCH_W) * yconv)
    st.put("buf", 0, u[c - 2:c, :])


def _chunk_consts(c):
    r = lax.broadcasted_iota(jnp.int32, (c, c), 0)
    l = lax.broadcasted_iota(jnp.int32, (c, c), 1)
    causal = r >= l
    r8 = lax.broadcasted_iota(jnp.int32, (SUBLANES, LANES), 0)
    l8 = lax.broadcasted_iota(jnp.int32, (SUBLANES, LANES), 1)
    return {"causal": causal,
            "tril": jnp.where(causal, 1.0, 0.0).astype(BF16),
            "sel8": jnp.where(r8 == l8, 1.0, 0.0).astype(BF16)}


def _layer_lb(lbc, layer):
    e = jnp.exp(lbc - jnp.max(lbc, axis=0, keepdims=True))
    sm = e / jnp.sum(e, axis=0, keepdims=True)
    cum = sm[0:1, :]
    for j in range(1, layer + 1):
        cum = cum + sm[j:j + 1, :]
    return cum - sm[0:1, :]


def _load_params(bias_ref, na_ref, nb_ref, nc_ref, lbc_ref, cw_ref, layer):
    return {"bias_if": bias_ref[...], "norm_a": na_ref[...], "norm_b": nb_ref[...], "norm_c": nc_ref[...],
            "lb": _layer_lb(lbc_ref[...], layer), "conv_w": cw_ref[...]}


def _mixer_prompt_kernel(layer, c, tt,
                         z_ref, cos_ref, sin_ref, rdec_ref, rinter_ref, rtoend_ref, rcdec_ref,
                         bias_ref, na_ref, nb_ref, nc_ref, lbc_ref, cw_ref,
                         y_ref, c_out, n_out, m_out, sr_out, sh_out, buf_out,
                         c_s, n_s, m_s, sr_s, sht_s, buf_s, ybuf):
    j = pl.program_id(1)

    @pl.when(j == 0)
    def _():
        c_s[...] = jnp.zeros_like(c_s)
        n_s[...] = jnp.zeros_like(n_s)
        m_s[...] = jnp.zeros_like(m_s)
        sr_s[...] = jnp.zeros_like(sr_s)
        sht_s[...] = jnp.zeros_like(sht_s)
        buf_s[...] = jnp.zeros_like(buf_s)

    prm = _load_params(bias_ref, na_ref, nb_ref, nc_ref, lbc_ref, cw_ref, layer)
    cst = _chunk_consts(c)
    cst["ret_dec"] = lambda h: rdec_ref[h]
    cst["ret_inter"] = rinter_ref[...]
    cst["ret_toend"] = rtoend_ref[...]
    cst["ret_cdec"] = rcdec_ref[...]

    def get(name, h):
        if name == "C":
            return c_s[h]
        if name == "n":
            return n_s[h:h + 1, :]
        if name == "m":
            return m_s[h:h + 1, :]
        if name == "Sr":
            return sr_s[h]
        if name == "ShT":
            return sht_s[h]
        return buf_s[0:CONV_W - 1, :]

    def put(name, h, val):
        if name == "C":
            c_s[h] = val
        elif name == "n":
            n_s[h:h + 1, :] = val
        elif name == "m":
            m_s[h:h + 1, :] = val
        elif name == "Sr":
            sr_s[h] = val
        elif name == "ShT":
            sht_s[h] = val
        else:
            buf_s[0:CONV_W - 1, :] = val

    st = _State(get, put)

    def body(ci, carry):
        r = pl.multiple_of(ci * c, c)
        zc = lambda off, w: z_ref[pl.ds(r, c), off:off + w]
        cc = dict(cst)
        cc["cos2"] = cos_ref[pl.ds(r, c), :]
        cc["sin2"] = sin_ref[pl.ds(r, c), :]

        def put_y(off, val):
            ybuf[pl.ds(r, c), off:off + BRANCH_W] = val

        _mixer_chunk(zc, c, GLA_SUB, st, cc, prm, put_y)
        return carry

    lax.fori_loop(0, tt // c, body, 0)
    y_ref[...] = ybuf[...].astype(BF16)

    @pl.when(j == pl.num_programs(1) - 1)
    def _():
        c_out[0] = c_s[...]
        n_out[0] = n_s[0:N_HEADS, :]
        m_out[0] = m_s[0:N_HEADS, :]
        sr_out[0] = sr_s[...]
        for h in range(N_HEADS):
            sh_out[0, h] = sht_s[h].T
        buf_out[0] = buf_s[0:CONV_W - 1, :]


def _const_spec(shape):
    nd = len(shape)
    return pl.BlockSpec(shape, lambda *_: (0,) * nd)


def _mixer_prompt(z, row0, bsz, t, tt, layer, tabs, prms):
    c = math.gcd(t, CHUNK)
    nj = t // tt
    blk0 = row0 // tt
    cos2, sin2, rdec, rinter, rtoend, rcdec = tabs
    state_shapes = [
        jax.ShapeDtypeStruct((bsz, N_HEADS, HEAD_DIM, HEAD_DIM), F32),
        jax.ShapeDtypeStruct((bsz, N_HEADS, HEAD_DIM), F32),
        jax.ShapeDtypeStruct((bsz, N_HEADS, LANES), F32),
        jax.ShapeDtypeStruct((bsz, N_HEADS, HEAD_DIM, HEAD_DIM), F32),
        jax.ShapeDtypeStruct((bsz, N_HEADS, HEAD_DIM, HEAD_DIM), F32),
        jax.ShapeDtypeStruct((bsz, CONV_W - 1, BRANCH_W), F32),
    ]
    big = pl.BlockSpec((1, N_HEADS, HEAD_DIM, HEAD_DIM), lambda b, j: (b, 0, 0, 0))
    small = pl.BlockSpec((1, N_HEADS, LANES), lambda b, j: (b, 0, 0))
    est = 2 * tt * P_PAD * 4 + 2 * tt * Y_W * 2 + tt * Y_W * 4 + 8 * N_HEADS * HEAD_DIM * HEAD_DIM * 4
    outs = pl.pallas_call(
        functools.partial(_mixer_prompt_kernel, layer, c, tt),
        grid=(bsz, nj),
        in_specs=[pl.BlockSpec((tt, P_PAD), lambda b, j: (blk0 + b * nj + j, 0)),
                  pl.BlockSpec((tt, LANES), lambda b, j: (j, 0)),
                  pl.BlockSpec((tt, LANES), lambda b, j: (j, 0)),
                  _const_spec(rdec.shape), _const_spec(rinter.shape), _const_spec(rtoend.shape),
                  _const_spec(rcdec.shape)] + [_const_spec(p.shape) for p in prms],
        out_specs=[pl.BlockSpec((tt, Y_W), lambda b, j: (b * nj + j, 0)),
                   big, small, small, big, big,
                   pl.BlockSpec((1, CONV_W - 1, BRANCH_W), lambda b, j: (b, 0, 0))],
        out_shape=[jax.ShapeDtypeStruct((bsz * t, Y_W), BF16)] + state_shapes,
        scratch_shapes=[pltpu.VMEM((N_HEADS, HEAD_DIM, HEAD_DIM), F32),
                        pltpu.VMEM((SUBLANES, LANES), F32),
                        pltpu.VMEM((SUBLANES, LANES), F32),
                        pltpu.VMEM((N_HEADS, HEAD_DIM, HEAD_DIM), F32),
                        pltpu.VMEM((N_HEADS, HEAD_DIM, HEAD_DIM), F32),
                        pltpu.VMEM((SUBLANES, BRANCH_W), F32),
                        pltpu.VMEM((tt, Y_W), F32)],
        compiler_params=pltpu.CompilerParams(dimension_semantics=("arbitrary", "arbitrary"),
                                             vmem_limit_bytes=_vmem_limit(est)),
        name=f"mixer_prompt_l{layer}",
    )(z, cos2, sin2, rdec, rinter, rtoend, rcdec, *prms)
    return outs


def _mixer_sample_kernel(layer, c, bb,
                         z_ref, cos_ref, sin_ref, rdec_ref, rinter_ref, rtoend_ref, rcdec_ref,
                         bias_ref, na_ref, nb_ref, nc_ref, lbc_ref, cw_ref,
                         c_in, n_in, m_in, sr_in, sh_in, buf_in,
                         y_ref, c_out, n_out, m_out, sr_out, sh_out, buf_out,
                         ybuf):
    prm = _load_params(bias_ref, na_ref, nb_ref, nc_ref, lbc_ref, cw_ref, layer)
    cst = _chunk_consts(c)
    cst["ret_dec"] = lambda h: rdec_ref[h]
    cst["ret_inter"] = rinter_ref[...]
    cst["ret_toend"] = rtoend_ref[...]
    cst["ret_cdec"] = rcdec_ref[...]
    cst["cos2"] = cos_ref[...]
    cst["sin2"] = sin_ref[...]

    def body(i, carry):
        r = pl.multiple_of(i * c, c)
        zc = lambda off, w: z_ref[pl.ds(r, c), off:off + w]

        def get(name, h):
            if name == "C":
                return c_in[i, h]
            if name == "n":
                return n_in[i, pl.ds(h, 1), :]
            if name == "m":
                return m_in[i, pl.ds(h, 1), :]
            if name == "Sr":
                return sr_in[i, h]
            if name == "ShT":
                return sh_in[i, h].T
            return buf_in[i]

        def put(name, h, val):
            if name == "C":
                c_out[i, h] = val
            elif name == "n":
                n_out[i, pl.ds(h, 1), :] = val
            elif name == "m":
                m_out[i, pl.ds(h, 1), :] = val
            elif name == "Sr":
                sr_out[i, h] = val
            elif name == "ShT":
                sh_out[i, h] = val.T
            else:
                buf_out[i] = val

        def put_y(off, val):
            ybuf[pl.ds(r, c), off:off + BRANCH_W] = val

        _mixer_chunk(zc, c, c, _State(get, put), cst, prm, put_y)
        return carry

    lax.fori_loop(0, bb, body, 0)
    y_ref[...] = ybuf[...].astype(BF16)


def _mixer_sample(z, row0, bsz, t, bb, layer, tabs, prms, states):
    c = t
    rows = bb * t
    blk0 = row0 // rows
    cos2, sin2, rdec, rinter, rtoend, rcdec = tabs
    big = pl.BlockSpec((bb, N_HEADS, HEAD_DIM, HEAD_DIM), lambda i: (i, 0, 0, 0))
    small = pl.BlockSpec((bb, N_HEADS, LANES), lambda i: (i, 0, 0))
    bufspec = pl.BlockSpec((bb, CONV_W - 1, BRANCH_W), lambda i: (i, 0, 0))
    state_specs = [big, small, small, big, big, bufspec]
    state_shapes = [jax.ShapeDtypeStruct(s.shape, F32) for s in states]
    est = (2 * rows * P_PAD * 4 + 2 * rows * Y_W * 2 + rows * Y_W * 4
           + 2 * 2 * 3 * bb * N_HEADS * HEAD_DIM * HEAD_DIM * 4)
    return pl.pallas_call(
        functools.partial(_mixer_sample_kernel, layer, c, bb),
        grid=(bsz // bb,),
        in_specs=[pl.BlockSpec((rows, P_PAD), lambda i: (blk0 + i, 0)),
                  _const_spec(cos2.shape), _const_spec(sin2.shape),
                  _const_spec(rdec.shape), _const_spec(rinter.shape), _const_spec(rtoend.shape),
                  _const_spec(rcdec.shape)] + [_const_spec(p.shape) for p in prms] + state_specs,
        out_specs=[pl.BlockSpec((rows, Y_W), lambda i: (i, 0))] + state_specs,
        out_shape=[jax.ShapeDtypeStruct((bsz * t, Y_W), BF16)] + state_shapes,
        scratch_shapes=[pltpu.VMEM((rows, Y_W), F32)],
        compiler_params=pltpu.CompilerParams(dimension_semantics=("arbitrary",),
                                             vmem_limit_bytes=_vmem_limit(est)),
        name=f"mixer_sample_l{layer}",
    )(z, cos2, sin2, rdec, rinter, rtoend, rcdec, *prms, *states)


def _layer_norm(r, g, b):
    mu = jnp.mean(r, axis=1, keepdims=True)
    rc = r - mu
    var = jnp.mean(rc * rc, axis=1, keepdims=True)
    return rc * lax.rsqrt(var + LN_EPS) * g + b


def _merge_kernel(n_prompt_blocks, x_ref, yp_ref, ys_ref, wb_ref, wg_ref, bg_ref, wo_ref, g_ref, b_ref, h_ref):
    i = pl.program_id(0)
    x = x_ref[...]
    xb = x.astype(BF16)
    y = jnp.where(i < n_prompt_blocks, yp_ref[...], ys_ref[...])
    acc = None
    for n in range(N_BRANCH):
        proj = jnp.dot(y[:, n * BRANCH_W:(n + 1) * BRANCH_W], wb_ref[n], preferred_element_type=F32)
        gate = jax.nn.sigmoid(jnp.dot(xb, wg_ref[:, n * D_MODEL:(n + 1) * D_MODEL], preferred_element_type=F32)
                              + bg_ref[:, n * D_MODEL:(n + 1) * D_MODEL])
        acc = gate * proj if acc is None else acc + gate * proj
    mix = jnp.dot(acc.astype(BF16), wo_ref[...], preferred_element_type=F32)
    h_ref[...] = _layer_norm(ALPHA * x + mix, g_ref[...], b_ref[...])


def _merge(x, y_p, y_s, wb, wg, bg, wo, g, b, tm):
    n = x.shape[0]
    npb = y_p.shape[0] // tm
    nsb = y_s.shape[0] // tm
    est = (2 * (N_BRANCH * BRANCH_W * D_MODEL + D_MODEL * N_BRANCH * D_MODEL + D_MODEL * D_MODEL) * 2
           + 2 * tm * D_MODEL * 4 * 2 + 2 * 2 * tm * Y_W * 2 + 6 * tm * D_MODEL * 4)
    return pl.pallas_call(
        functools.partial(_merge_kernel, npb),
        grid=(n // tm,),
        in_specs=[pl.BlockSpec((tm, D_MODEL), lambda i: (i, 0)),
                  pl.BlockSpec((tm, Y_W), lambda i: (jnp.minimum(i, npb - 1), 0)),
                  pl.BlockSpec((tm, Y_W), lambda i: (jnp.clip(i - npb, 0, nsb - 1), 0)),
                  _const_spec(wb.shape), _const_spec(wg.shape), _const_spec(bg.shape),
                  _const_spec(wo.shape), _const_spec(g.shape), _const_spec(b.shape)],
        out_specs=pl.BlockSpec((tm, D_MODEL), lambda i: (i, 0)),
        out_shape=jax.ShapeDtypeStruct((n, D_MODEL), F32),
        compiler_params=pltpu.CompilerParams(dimension_semantics=("arbitrary",),
                                             vmem_limit_bytes=_vmem_limit(est)),
        name="merge",
    )(x, y_p, y_s, wb, wg, bg, wo, g, b)


def _ffn_kernel(h_ref, wu_ref, wd_ref, g_ref, b_ref, o_ref):
    h = h_ref[...]
    u = jnp.maximum(jnp.dot(h.astype(BF16), wu_ref[...], preferred_element_type=F32), 0.0)
    ff = jnp.dot((u * u).astype(BF16), wd_ref[...], preferred_element_type=F32)
    o_ref[...] = _layer_norm(ALPHA * h + ff, g_ref[...], b_ref[...])


def _ffn(h, wu, wd, g, b, tm):
    n = h.shape[0]
    est = 2 * 2 * D_MODEL * D_FF * 2 + 2 * 2 * tm * D_MODEL * 4 + tm * D_FF * 6 + 2 * tm * D_MODEL * 4
    return pl.pallas_call(
        _ffn_kernel,
        grid=(n // tm,),
        in_specs=[pl.BlockSpec((tm, D_MODEL), lambda i: (i, 0)),
                  _const_spec(wu.shape), _const_spec(wd.shape), _const_spec(g.shape), _const_spec(b.shape)],
        out_specs=pl.BlockSpec((tm, D_MODEL), lambda i: (i, 0)),
        out_shape=jax.ShapeDtypeStruct((n, D_MODEL), F32),
        compiler_params=pltpu.CompilerParams(dimension_semantics=("arbitrary",),
                                             vmem_limit_bytes=_vmem_limit(est)),
        name="ffn",
    )(h, wu, wd, g, b)


def _pad_lanes(a):
    return jnp.pad(a, ((0, 0), (0, LANES - a.shape[1])))


def _position_tables(pos, c):
    half = HEAD_DIM // 2
    inv = ROPE_BASE ** (-jnp.arange(half, dtype=F32) / half)
    ang = pos[:, None] * inv[None, :]
    cos, sin = jnp.cos(ang), jnp.sin(ang)
    cos2 = jnp.concatenate([cos, cos], axis=1)
    sin2 = jnp.concatenate([-sin, sin], axis=1)
    log_gamma = jnp.log(1.0 - 2.0 ** (-5.0 - jnp.arange(N_HEADS, dtype=F32)))
    idx = jnp.arange(c, dtype=F32)
    rel = idx[:, None] - idx[None, :]
    dec = jnp.where(rel >= 0, jnp.exp(log_gamma[:, None, None] * jnp.maximum(rel, 0.0)), 0.0)
    inter = jnp.swapaxes(jnp.exp(log_gamma[:, None] * (idx + 1.0)[None, :]), 0, 1)
    to_end = jnp.swapaxes(jnp.exp(log_gamma[:, None] * (c - 1.0 - idx)[None, :]), 0, 1)
    cdec = jnp.broadcast_to(jnp.exp(log_gamma * c)[:, None], (N_HEADS, LANES))
    cdec = jnp.pad(cdec, ((0, SUBLANES - N_HEADS), (0, 0)))
    return cos2, sin2, dec, _pad_lanes(inter), _pad_lanes(to_end), cdec


def _pack_w_in(w_in):
    a = 4 * BRANCH_W
    gates = w_in[:, a:a + 2 * N_HEADS]
    wide = jnp.concatenate([w_in[:, :a], w_in[:, a + 2 * N_HEADS:]], axis=1)
    return jnp.concatenate([wide, _pad_lanes(gates)], axis=1).astype(BF16)


def kernel(x_prompt, x_sample, state_mlstm_C, state_mlstm_n, state_mlstm_m, state_ret, state_hgrn, state_conv,
           w_in, b_mlstm_gate, norm_a, norm_b, norm_c, lb_c, conv_w, w_branch, w_gate, b_gate, w_out,
           ln1_g, ln1_b, ln2_g, ln2_b, w_up, w_down):
    bp, tp, _ = x_prompt.shape
    bs, ts, _ = x_sample.shape
    n_p, n_s = bp * tp, bs * ts
    tm = 512
    tt = 256
    bb = 8
    assert n_p % tm == 0 and n_s % tm == 0 and tp % tt == 0 and bs % bb == 0 and n_p % (bb * ts) == 0

    x = jnp.concatenate([x_prompt.reshape(n_p, D_MODEL), x_sample.reshape(n_s, D_MODEL)], axis=0)
    tabs_p = _position_tables(jnp.arange(tp, dtype=F32), math.gcd(tp, CHUNK))
    tabs_s = _position_tables(PAST_LEN + jnp.arange(ts, dtype=F32), math.gcd(ts, CHUNK))
    lbc = lb_c.astype(F32)
    row = lambda a: a.reshape(1, -1).astype(F32)

    p_states, s_states = [], []
    for l in range(DEPTH):
        prms = (_pad_lanes(row(b_mlstm_gate[l])), row(norm_a[l]), row(norm_b[l]), row(norm_c[l]), lbc,
                conv_w[l].astype(F32))
        z = _inproj(x, _pack_w_in(w_in[l]), 256)
        outs_p = _mixer_prompt(z, 0, bp, tp, tt, l, tabs_p, prms)
        st_in = (state_mlstm_C[l].astype(F32), state_mlstm_n[l].astype(F32),
                 jnp.broadcast_to(state_mlstm_m[l].astype(F32)[..., None], (bs, N_HEADS, LANES)),
                 state_ret[l].astype(F32), state_hgrn[l].astype(F32), state_conv[l].astype(F32))
        outs_s = _mixer_sample(z, n_p, bs, ts, bb, l, tabs_s, prms, st_in)
        h = _merge(x, outs_p[0], outs_s[0], w_branch[l].astype(BF16), w_gate[l].astype(BF16), row(b_gate[l]),
                   w_out[l].astype(BF16), row(ln1_g[l]), row(ln1_b[l]), tm)
        x = _ffn(h, w_up[l].astype(BF16), w_down[l].astype(BF16), row(ln2_g[l]), row(ln2_b[l]), tm)
        p_states.append(outs_p[1:])
        s_states.append(outs_s[1:])

    def stack(states):
        cs, ns, ms, srs, shs, bufs = zip(*states)
        return (jnp.stack(cs), jnp.stack(ns), jnp.stack([m[..., 0] for m in ms]),
                jnp.stack(srs), jnp.stack(shs), jnp.stack(bufs))

    y_prompt = x[:n_p].reshape(bp, tp, D_MODEL)
    y_sample = x[n_p:].reshape(bs, ts, D_MODEL)
    return (y_prompt, y_sample) + stack(p_states) + stack(s_states)
```

```python
import functools
import math

import jax
import jax.numpy as jnp
from jax import lax
from jax.experimental import pallas as pl
from jax.experimental.pallas import tpu as pltpu

F32 = jnp.float32
BF16 = jnp.bfloat16

D_MODEL = 1024
DEPTH = 2
BRANCH_W = D_MODEL // 2
N_HEADS = 4
HEAD_DIM = BRANCH_W // N_HEADS
CONV_W = 3
N_BRANCH = 4
D_FF = 4 * D_MODEL
CHUNK = 64
ROPE_BASE = 10000.0
LN_EPS = 1e-5
NEG_BIG = -1e30
GATE_CLAMP = 1.0 - 1e-6
ALPHA = (2 * DEPTH) ** 0.25
PAST_LEN = 16384
K_SCALE = HEAD_DIM ** -0.5

LANES = 128
SUBLANES = 8
VMEM_BYTES = 64 * 1024 * 1024

(QA, KA, VA, OA, QB, KB, VB, GB, QC, FC, IC, GC, XD, BGD, CGD) = [i * BRANCH_W for i in range(15)]
GATE_OFF = 15 * BRANCH_W
P_PAD = GATE_OFF + LANES
Y_W = N_BRANCH * BRANCH_W

GLA_SUB = 16


def _vmem_limit(est_bytes):
    return int(min(est_bytes + est_bytes // 4 + (4 << 20), VMEM_BYTES - (6 << 20)))


def _dot(a, b):
    return jnp.dot(a.astype(BF16), b.astype(BF16), preferred_element_type=F32)


def _dot_nt(a, b):
    return lax.dot_general(a.astype(BF16), b.astype(BF16), (((1,), (1,)), ((), ())),
                           preferred_element_type=F32)


def _dot_tn(a, b):
    return lax.dot_general(a.astype(BF16), b.astype(BF16), (((0,), (0,)), ((), ())),
                           preferred_element_type=F32)


def _split3(x):
    hi = x.astype(BF16)
    r = x - hi.astype(F32)
    mid = r.astype(BF16)
    lo = (r - mid.astype(F32)).astype(BF16)
    return hi, mid, lo


def _dot_exact_rhs(a01, x):
    hi, mid, lo = _split3(x)
    d = lambda p: jnp.dot(a01, p, preferred_element_type=F32)
    return (d(hi) + d(mid)) + d(lo)


def _dot_nt_exact_rhs(a01, x):
    hi, mid, lo = _split3(x)
    d = lambda p: lax.dot_general(a01, p, (((1,), (1,)), ((), ())), preferred_element_type=F32)
    return (d(hi) + d(mid)) + d(lo)


def _silu(x):
    return x * jax.nn.sigmoid(x)


def _group_specs(tm, width, npb, nsb):
    return [pl.BlockSpec((tm, width), lambda i: (jnp.minimum(i, npb - 1), 0)),
            pl.BlockSpec((tm, width), lambda i: (jnp.clip(i - npb, 0, nsb - 1), 0))]


def _group_load(npb, p_ref, s_ref):
    return jnp.where(pl.program_id(0) < npb, p_ref[...], s_ref[...])


def _group_store(npb, p_ref, s_ref, val):
    is_p = pl.program_id(0) < npb

    @pl.when(is_p)
    def _():
        p_ref[...] = val

    @pl.when(jnp.logical_not(is_p))
    def _():
        s_ref[...] = val


def _inproj_kernel(npb, xp_ref, xs_ref, w_ref, z_ref):
    xb = _group_load(npb, xp_ref, xs_ref).astype(BF16)
    for off in range(0, P_PAD, BRANCH_W):
        wd = min(BRANCH_W, P_PAD - off)
        z_ref[:, off:off + wd] = jnp.dot(xb, w_ref[:, off:off + wd], preferred_element_type=F32)


def _inproj(xp, xs, layer, w, tm):
    npb, nsb = xp.shape[0] // tm, xs.shape[0] // tm
    est = D_MODEL * P_PAD * 2 * 2 + 2 * tm * P_PAD * 4 + 2 * 2 * tm * D_MODEL * 4
    return pl.pallas_call(
        functools.partial(_inproj_kernel, npb),
        grid=(npb + nsb,),
        in_specs=_group_specs(tm, D_MODEL, npb, nsb) + [_layer_spec(layer, D_MODEL, P_PAD)],
        out_specs=pl.BlockSpec((tm, P_PAD), lambda i: (i, 0)),
        out_shape=jax.ShapeDtypeStruct(((npb + nsb) * tm, P_PAD), F32),
        compiler_params=pltpu.CompilerParams(dimension_semantics=("arbitrary",),
                                             vmem_limit_bytes=_vmem_limit(est)),
        name="inproj",
    )(xp, xs, w)


class _State:
    def __init__(self, get, put):
        self.get = get
        self.put = put


def _mixer_chunk(zc, c, sub, st, cst, prm, put_y):
    tril, causal, sel8 = cst["tril"], cst["causal"], cst["sel8"]
    old = {(name, h): st.get(name, h) for name in ("C", "n", "m", "Sr", "ShT") for h in range(N_HEADS)}
    old[("buf", 0)] = st.get("buf", 0)
    new = {}

    gp = zc(GATE_OFF, LANES) + prm["bias_if"]
    lf = jnp.minimum(gp, 0.0) - jnp.log1p(jnp.exp(-jnp.abs(gp)))
    b_all = _dot_exact_rhs(tril, lf)
    gp_t = _dot_nt_exact_rhs(sel8, gp)
    b_t = _dot_nt_exact_rhs(sel8, b_all)
    ya = []
    for h in range(N_HEADS):
        hs = h * HEAD_DIM
        q = zc(QA + hs, HEAD_DIM)
        k = zc(KA + hs, HEAD_DIM) * K_SCALE
        v = zc(VA + hs, HEAD_DIM)
        b_col = b_all[:, N_HEADS + h:N_HEADS + h + 1]
        i_col = gp[:, h:h + 1]
        b_row = b_t[N_HEADS + h:N_HEADS + h + 1, :]
        i_row = gp_t[h:h + 1, :]
        m_prev = old[("m", h)][:, 0:1]
        c_prev = old[("C", h)]
        n_prev = old[("n", h)]
        a_col = b_col + m_prev
        dmat = jnp.where(causal, (b_col - b_row) + i_row, NEG_BIG)
        m_t = jnp.maximum(a_col, jnp.max(dmat, axis=1, keepdims=True))
        w_inter = jnp.exp(a_col - m_t)
        s = _dot_nt(q, k) * jnp.exp(dmat - m_t)
        num = _dot(s, v) + _dot(q, c_prev) * w_inter
        den = jnp.sum(s, axis=1, keepdims=True) + jnp.sum(q * n_prev, axis=1, keepdims=True) * w_inter
        den = jnp.maximum(jnp.abs(den), jnp.exp(-m_t))
        hh = num / den
        hn = hh * lax.rsqrt(jnp.mean(hh * hh, axis=1, keepdims=True) + LN_EPS)
        ya.append(jax.nn.sigmoid(zc(OA + hs, HEAD_DIM)) * (hn * prm["norm_a"][:, hs:hs + HEAD_DIM]))
        b_last = b_col[c - 1:c, :]
        g_col = (b_last - b_col) + i_col
        m_new = jnp.maximum(b_last + m_prev, jnp.max(g_col, axis=0, keepdims=True))
        w_s = jnp.exp(g_col - m_new)
        dec = jnp.exp(b_last + m_prev - m_new)
        kw = k * w_s
        new[("C", h)] = dec * c_prev + _dot_tn(kw, v)
        new[("n", h)] = dec * n_prev + jnp.sum(kw, axis=0, keepdims=True)
        new[("m", h)] = jnp.broadcast_to(m_new, (1, LANES))
    put_y(0, jnp.concatenate(ya, axis=1))

    cos2, sin2 = cst["cos2"], cst["sin2"]
    rope = lambda x: x * cos2 + pltpu.roll(x, HEAD_DIM // 2, axis=1) * sin2
    yb = []
    for h in range(N_HEADS):
        hs = h * HEAD_DIM
        qb = rope(zc(QB + hs, HEAD_DIM))
        kb = rope(zc(KB + hs, HEAD_DIM)) * K_SCALE
        vb = zc(VB + hs, HEAD_DIM)
        s_prev = old[("Sr", h)]
        s = _dot_nt(qb, kb) * cst["ret_dec"](h)
        o = _dot(s, vb) + _dot(qb, s_prev) * cst["ret_inter"][:, h:h + 1]
        new[("Sr", h)] = cst["ret_cdec"][h:h + 1, :] * s_prev + _dot_tn(kb * cst["ret_toend"][:, h:h + 1], vb)
        oc = o - jnp.mean(o, axis=1, keepdims=True)
        on = oc * lax.rsqrt(jnp.mean(oc * oc, axis=1, keepdims=True) + LN_EPS)
        yb.append(_silu(zc(GB + hs, HEAD_DIM)) * (on * prm["norm_b"][:, hs:hs + HEAD_DIM]))
    put_y(BRANCH_W, jnp.concatenate(yb, axis=1))

    kc_all = (1.0 - prm["lb"]) * jax.nn.sigmoid(-zc(FC, BRANCH_W))
    g_all = jnp.log1p(-jnp.minimum(kc_all, GATE_CLAMP))
    bc_all = _dot_exact_rhs(tril, g_all)
    lane_i = lax.broadcasted_iota(jnp.int32, (sub, c), 1)
    row_i = lax.broadcasted_iota(jnp.int32, (sub, c), 0)
    yc = []
    for h in range(N_HEADS):
        hs = h * HEAD_DIM
        qg = _silu(zc(QC + hs, HEAD_DIM))
        kc = kc_all[:, hs:hs + HEAD_DIM]
        vv = zc(IC + hs, HEAD_DIM)
        bc = bc_all[:, hs:hs + HEAD_DIM]
        sh_t = old[("ShT", h)]
        blocks = []
        for r0 in range(0, c, sub):
            bc_i = bc[r0:r0 + sub, :]
            q_i = qg[r0:r0 + sub, :]
            if r0 > 0:
                ref = bc[r0 - 1:r0, :]
                qt = q_i * jnp.exp(bc_i - ref)
                kt = kc * jnp.exp(jnp.minimum(ref - bc, 0.0))
                a_i = jnp.where(lane_i < r0, _dot_nt(qt, kt), 0.0)
            else:
                a_i = jnp.zeros((sub, c), F32)
            for s_ in range(sub):
                w = jnp.exp(jnp.minimum(bc_i - bc[r0 + s_:r0 + s_ + 1, :], 0.0))
                col = jnp.sum((q_i * kc[r0 + s_:r0 + s_ + 1, :]) * w, axis=1, keepdims=True)
                a_i = jnp.where(lane_i == r0 + s_, jnp.where(row_i >= s_, col, 0.0), a_i)
            blocks.append(a_i)
        a = blocks[0] if len(blocks) == 1 else jnp.concatenate(blocks, axis=0)
        o = _dot(a, vv) + _dot_nt(qg * jnp.exp(bc), sh_t)
        b_last = bc[c - 1:c, :]
        new[("ShT", h)] = jnp.exp(b_last) * sh_t + _dot_tn(vv, kc * jnp.exp(b_last - bc))
        on = o * lax.rsqrt(jnp.mean(o * o, axis=1, keepdims=True) + LN_EPS)
        yc.append(_silu(zc(GC + hs, HEAD_DIM)) * (on * prm["norm_c"][:, hs:hs + HEAD_DIM]))
    put_y(2 * BRANCH_W, jnp.concatenate(yc, axis=1))

    u = zc(CGD, BRANCH_W) * zc(XD, BRANCH_W)
    buf = old[("buf", 0)]
    row = lax.broadcasted_iota(jnp.int32, (c, BRANCH_W), 0)
    u1 = jnp.where(row == 0, buf[1:2, :], pltpu.roll(u, 1, axis=0))
    u2 = jnp.where(row == 0, buf[0:1, :], jnp.where(row == 1, buf[1:2, :], pltpu.roll(u, 2, axis=0)))
    cw = prm["conv_w"]
    yconv = (u2 * cw[0:1, :] + u1 * cw[1:2, :]) + u * cw[2:3, :]
    put_y(3 * BRANCH_W, zc(BGD, BRANCH_W) * yconv)
    new[("buf", 0)] = u[c - 2:c, :]
    for (name, h), val in new.items():
        st.put(name, h, val)


def _chunk_consts(c):
    r = lax.broadcasted_iota(jnp.int32, (c, c), 0)
    l = lax.broadcasted_iota(jnp.int32, (c, c), 1)
    causal = r >= l
    r8 = lax.broadcasted_iota(jnp.int32, (SUBLANES, LANES), 0)
    l8 = lax.broadcasted_iota(jnp.int32, (SUBLANES, LANES), 1)
    return {"causal": causal,
            "tril": jnp.where(causal, 1.0, 0.0).astype(BF16),
            "sel8": jnp.where(r8 == l8, 1.0, 0.0).astype(BF16)}


def _layer_lb(lbc, layer):
    e = jnp.exp(lbc - jnp.max(lbc, axis=0, keepdims=True))
    sm = e / jnp.sum(e, axis=0, keepdims=True)
    cum = sm[0:1, :]
    for j in range(1, layer + 1):
        cum = cum + sm[j:j + 1, :]
    return cum - sm[0:1, :]


def _load_params(bias_ref, na_ref, nb_ref, nc_ref, lbc_ref, cw_ref, layer):
    return {"bias_if": bias_ref[...], "norm_a": na_ref[...], "norm_b": nb_ref[...], "norm_c": nc_ref[...],
            "lb": _layer_lb(lbc_ref[...], layer), "conv_w": cw_ref[...]}


def _mixer_prompt_kernel(layer, c, tt,
                         z_ref, cos_ref, sin_ref, rdec_ref, rinter_ref, rtoend_ref, rcdec_ref,
                         bias_ref, na_ref, nb_ref, nc_ref, lbc_ref, cw_ref,
                         y_ref, c_out, n_out, m_out, sr_out, sh_out, buf_out,
                         c_s, n_s, m_s, sr_s, sht_s, buf_s, ybuf):
    j = pl.program_id(1)

    @pl.when(j == 0)
    def _():
        c_s[...] = jnp.zeros_like(c_s)
        n_s[...] = jnp.zeros_like(n_s)
        m_s[...] = jnp.zeros_like(m_s)
        sr_s[...] = jnp.zeros_like(sr_s)
        sht_s[...] = jnp.zeros_like(sht_s)
        buf_s[...] = jnp.zeros_like(buf_s)

    prm = _load_params(bias_ref, na_ref, nb_ref, nc_ref, lbc_ref, cw_ref, layer)
    cst = _chunk_consts(c)
    cst["ret_dec"] = lambda h: rdec_ref[h]
    cst["ret_inter"] = rinter_ref[...]
    cst["ret_toend"] = rtoend_ref[...]
    cst["ret_cdec"] = rcdec_ref[...]

    def get(name, h):
        if name == "C":
            return c_s[h]
        if name == "n":
            return n_s[h:h + 1, :]
        if name == "m":
            return m_s[h:h + 1, :]
        if name == "Sr":
            return sr_s[h]
        if name == "ShT":
            return sht_s[h]
        return buf_s[0:CONV_W - 1, :]

    def put(name, h, val):
        if name == "C":
            c_s[h] = val
        elif name == "n":
            n_s[h:h + 1, :] = val
        elif name == "m":
            m_s[h:h + 1, :] = val
        elif name == "Sr":
            sr_s[h] = val
        elif name == "ShT":
            sht_s[h] = val
        else:
            buf_s[0:CONV_W - 1, :] = val

    st = _State(get, put)

    def body(ci, carry):
        r = pl.multiple_of(ci * c, c)
        zc = lambda off, w: z_ref[pl.ds(r, c), off:off + w]
        cc = dict(cst)
        cc["cos2"] = cos_ref[pl.ds(r, c), :]
        cc["sin2"] = sin_ref[pl.ds(r, c), :]

        def put_y(off, val):
            ybuf[pl.ds(r, c), off:off + BRANCH_W] = val

        _mixer_chunk(zc, c, GLA_SUB, st, cc, prm, put_y)
        return carry

    lax.fori_loop(0, tt // c, body, 0)
    y_ref[...] = ybuf[...].astype(BF16)

    @pl.when(j == pl.num_programs(1) - 1)
    def _():
        c_out[0] = c_s[...]
        n_out[0] = n_s[0:N_HEADS, :]
        m_out[0] = m_s[0:N_HEADS, :]
        sr_out[0] = sr_s[...]
        for h in range(N_HEADS):
            sh_out[0, h] = sht_s[h].T
        buf_out[0] = buf_s[0:CONV_W - 1, :]


def _const_spec(shape):
    nd = len(shape)
    return pl.BlockSpec(shape, lambda *_: (0,) * nd)


def _mixer_prompt(z, row0, bsz, t, tt, layer, tabs, prms):
    c = math.gcd(t, CHUNK)
    nj = t // tt
    blk0 = row0 // tt
    cos2, sin2, rdec, rinter, rtoend, rcdec = tabs
    state_shapes = [
        jax.ShapeDtypeStruct((bsz, N_HEADS, HEAD_DIM, HEAD_DIM), F32),
        jax.ShapeDtypeStruct((bsz, N_HEADS, HEAD_DIM), F32),
        jax.ShapeDtypeStruct((bsz, N_HEADS, LANES), F32),
        jax.ShapeDtypeStruct((bsz, N_HEADS, HEAD_DIM, HEAD_DIM), F32),
        jax.ShapeDtypeStruct((bsz, N_HEADS, HEAD_DIM, HEAD_DIM), F32),
        jax.ShapeDtypeStruct((bsz, CONV_W - 1, BRANCH_W), F32),
    ]
    big = pl.BlockSpec((1, N_HEADS, HEAD_DIM, HEAD_DIM), lambda b, j: (b, 0, 0, 0))
    small = pl.BlockSpec((1, N_HEADS, LANES), lambda b, j: (b, 0, 0))
    est = 2 * tt * P_PAD * 4 + 2 * tt * Y_W * 2 + tt * Y_W * 4 + 8 * N_HEADS * HEAD_DIM * HEAD_DIM * 4
    outs = pl.pallas_call(
        functools.partial(_mixer_prompt_kernel, layer, c, tt),
        grid=(bsz, nj),
        in_specs=[pl.BlockSpec((tt, P_PAD), lambda b, j: (blk0 + b * nj + j, 0)),
                  pl.BlockSpec((tt, LANES), lambda b, j: (j, 0)),
                  pl.BlockSpec((tt, LANES), lambda b, j: (j, 0)),
                  _const_spec(rdec.shape), _const_spec(rinter.shape), _const_spec(rtoend.shape),
                  _const_spec(rcdec.shape)] + [_const_spec(p.shape) for p in prms],
        out_specs=[pl.BlockSpec((tt, Y_W), lambda b, j: (b * nj + j, 0)),
                   big, small, small, big, big,
                   pl.BlockSpec((1, CONV_W - 1, BRANCH_W), lambda b, j: (b, 0, 0))],
        out_shape=[jax.ShapeDtypeStruct((bsz * t, Y_W), BF16)] + state_shapes,
        scratch_shapes=[pltpu.VMEM((N_HEADS, HEAD_DIM, HEAD_DIM), F32),
                        pltpu.VMEM((SUBLANES, LANES), F32),
                        pltpu.VMEM((SUBLANES, LANES), F32),
                        pltpu.VMEM((N_HEADS, HEAD_DIM, HEAD_DIM), F32),
                        pltpu.VMEM((N_HEADS, HEAD_DIM, HEAD_DIM), F32),
                        pltpu.VMEM((SUBLANES, BRANCH_W), F32),
                        pltpu.VMEM((tt, Y_W), F32)],
        compiler_params=pltpu.CompilerParams(dimension_semantics=("arbitrary", "arbitrary"),
                                             vmem_limit_bytes=_vmem_limit(est)),
        name=f"mixer_prompt_l{layer}",
    )(z, cos2, sin2, rdec, rinter, rtoend, rcdec, *prms)
    return outs


N_STATES = 6
N_MIXER_CONST_INPUTS = 13


def _mixer_sample_kernel(layer, c, bb, n_carried, *refs):
    (z_ref, cos_ref, sin_ref, rdec_ref, rinter_ref, rtoend_ref, rcdec_ref,
     bias_ref, na_ref, nb_ref, nc_ref, lbc_ref, cw_ref) = refs[:N_MIXER_CONST_INPUTS]
    c_in, n_in, m_in, sr_in, sh_in, buf_in = refs[N_MIXER_CONST_INPUTS:N_MIXER_CONST_INPUTS + N_STATES]
    y_ref, c_out, n_out, m_out, sr_out, sh_out, buf_out, ybuf = refs[N_MIXER_CONST_INPUTS + N_STATES + n_carried:]
    prm = _load_params(bias_ref, na_ref, nb_ref, nc_ref, lbc_ref, cw_ref, layer)
    cst = _chunk_consts(c)
    cst["ret_dec"] = lambda h: rdec_ref[h]
    cst["ret_inter"] = rinter_ref[...]
    cst["ret_toend"] = rtoend_ref[...]
    cst["ret_cdec"] = rcdec_ref[...]
    cst["cos2"] = cos_ref[...]
    cst["sin2"] = sin_ref[...]

    def body(i, carry):
        r = pl.multiple_of(i * c, c)
        zc = lambda off, w: z_ref[pl.ds(r, c), off:off + w]

        def get(name, h):
            if name == "C":
                return c_in[i, h]
            if name == "n":
                return n_in[i, pl.ds(h, 1), :]
            if name == "m":
                return m_in[i, pl.ds(h, 1), :]
            if name == "Sr":
                return sr_in[i, h]
            if name == "ShT":
                return sh_in[i, h].T
            return buf_in[i]

        def put(name, h, val):
            if name == "C":
                c_out[i, h] = val
            elif name == "n":
                n_out[i, pl.ds(h, 1), :] = val
            elif name == "m":
                m_out[i, pl.ds(h, 1), :] = val
            elif name == "Sr":
                sr_out[i, h] = val
            elif name == "ShT":
                sh_out[i, h] = val.T
            else:
                buf_out[i] = val

        def put_y(off, val):
            ybuf[pl.ds(r, c), off:off + BRANCH_W] = val

        _mixer_chunk(zc, c, c, _State(get, put), cst, prm, put_y)
        return carry

    lax.fori_loop(0, bb, body, 0)
    y_ref[...] = ybuf[...].astype(BF16)


def _mixer_sample(z, row0, bsz, t, bb, layer, tabs, prms, states, carried):
    c = t
    rows = bb * t
    blk0 = row0 // rows
    cos2, sin2, rdec, rinter, rtoend, rcdec = tabs

    def lspec(*tail):
        zeros = (0,) * (len(tail) - 1)
        return pl.BlockSpec((None,) + tail, lambda i: (layer, i) + zeros)

    big = lspec(bb, N_HEADS, HEAD_DIM, HEAD_DIM)
    small = lspec(bb, N_HEADS, LANES)
    state_specs = [big, small, small, big, big, lspec(bb, CONV_W - 1, BRANCH_W)]
    state_shapes = [jax.ShapeDtypeStruct(s.shape, F32) for s in states]
    n_in = N_MIXER_CONST_INPUTS + N_STATES
    est = (2 * rows * P_PAD * 4 + 2 * rows * Y_W * 2 + rows * Y_W * 4
           + 2 * 2 * 3 * bb * N_HEADS * HEAD_DIM * HEAD_DIM * 4)
    return pl.pallas_call(
        functools.partial(_mixer_sample_kernel, layer, c, bb, len(carried)),
        grid=(bsz // bb,),
        in_specs=[pl.BlockSpec((rows, P_PAD), lambda i: (blk0 + i, 0)),
                  _const_spec(cos2.shape), _const_spec(sin2.shape),
                  _const_spec(rdec.shape), _const_spec(rinter.shape), _const_spec(rtoend.shape),
                  _const_spec(rcdec.shape)] + [_const_spec(p.shape) for p in prms] + state_specs
                 + [pl.BlockSpec(memory_space=pl.ANY)] * len(carried),
        out_specs=[pl.BlockSpec((rows, Y_W), lambda i: (i, 0))] + state_specs,
        out_shape=[jax.ShapeDtypeStruct((bsz * t, Y_W), BF16)] + state_shapes,
        input_output_aliases={n_in + k: 1 + k for k in range(len(carried))},
        scratch_shapes=[pltpu.VMEM((rows, Y_W), F32)],
        compiler_params=pltpu.CompilerParams(dimension_semantics=("arbitrary",),
                                             vmem_limit_bytes=_vmem_limit(est)),
        name=f"mixer_sample_l{layer}",
    )(z, cos2, sin2, rdec, rinter, rtoend, rcdec, *prms, *states, *carried)


def _layer_norm(r, g, b):
    mu = jnp.mean(r, axis=1, keepdims=True)
    rc = r - mu
    var = jnp.mean(rc * rc, axis=1, keepdims=True)
    return rc * lax.rsqrt(var + LN_EPS) * g + b


def _merge_kernel(npb, xp_ref, xs_ref, yp_ref, ys_ref, wb_ref, wg_ref, bg_ref, wo_ref, g_ref, b_ref,
                  hp_ref, hs_ref):
    x = _group_load(npb, xp_ref, xs_ref)
    xb = x.astype(BF16)
    y = _group_load(npb, yp_ref, ys_ref)
    acc = None
    for n in range(N_BRANCH):
        proj = jnp.dot(y[:, n * BRANCH_W:(n + 1) * BRANCH_W], wb_ref[n], preferred_element_type=F32)
        gate = jax.nn.sigmoid(jnp.dot(xb, wg_ref[:, n * D_MODEL:(n + 1) * D_MODEL], preferred_element_type=F32)
                              + bg_ref[:, n * D_MODEL:(n + 1) * D_MODEL])
        acc = gate * proj if acc is None else acc + gate * proj
    mix = jnp.dot(acc.astype(BF16), wo_ref[...], preferred_element_type=F32)
    _group_store(npb, hp_ref, hs_ref, _layer_norm(ALPHA * x + mix, g_ref[...], b_ref[...]))


def _layer_spec(layer, *tail):
    return pl.BlockSpec((None,) + tail, lambda *_: (layer,) + (0,) * len(tail))


def _group_out(xp, xs, tm, npb, nsb):
    return dict(out_specs=_group_specs(tm, D_MODEL, npb, nsb),
                out_shape=[jax.ShapeDtypeStruct(xp.shape, F32), jax.ShapeDtypeStruct(xs.shape, F32)])


def _merge(xp, xs, y_p, y_s, layer, wb, wg, bg, wo, g, b, tm):
    npb, nsb = xp.shape[0] // tm, xs.shape[0] // tm
    est = (2 * (N_BRANCH * BRANCH_W * D_MODEL + D_MODEL * N_BRANCH * D_MODEL + D_MODEL * D_MODEL) * 2
           + 2 * 2 * tm * D_MODEL * 4 * 2 + 2 * 2 * tm * Y_W * 2 + 6 * tm * D_MODEL * 4)
    return pl.pallas_call(
        functools.partial(_merge_kernel, npb),
        grid=(npb + nsb,),
        in_specs=_group_specs(tm, D_MODEL, npb, nsb) + _group_specs(tm, Y_W, npb, nsb)
                 + [_layer_spec(layer, N_BRANCH, BRANCH_W, D_MODEL), _layer_spec(layer, D_MODEL, N_BRANCH * D_MODEL),
                    _const_spec(bg.shape), _layer_spec(layer, D_MODEL, D_MODEL),
                    _const_spec(g.shape), _const_spec(b.shape)],
        compiler_params=pltpu.CompilerParams(dimension_semantics=("arbitrary",),
                                             vmem_limit_bytes=_vmem_limit(est)),
        name="merge",
        **_group_out(xp, xs, tm, npb, nsb),
    )(xp, xs, y_p, y_s, wb, wg, bg, wo, g, b)


def _ffn_kernel(npb, hp_ref, hs_ref, wu_ref, wd_ref, g_ref, b_ref, op_ref, os_ref):
    h = _group_load(npb, hp_ref, hs_ref)
    u = jnp.maximum(jnp.dot(h.astype(BF16), wu_ref[...], preferred_element_type=F32), 0.0)
    ff = jnp.dot((u * u).astype(BF16), wd_ref[...], preferred_element_type=F32)
    _group_store(npb, op_ref, os_ref, _layer_norm(ALPHA * h + ff, g_ref[...], b_ref[...]))


def _ffn(hp, hs, layer, wu, wd, g, b, tm):
    npb, nsb = hp.shape[0] // tm, hs.shape[0] // tm
    est = 2 * 2 * D_MODEL * D_FF * 2 + 2 * 2 * 2 * tm * D_MODEL * 4 + tm * D_FF * 6 + 2 * tm * D_MODEL * 4
    return pl.pallas_call(
        functools.partial(_ffn_kernel, npb),
        grid=(npb + nsb,),
        in_specs=_group_specs(tm, D_MODEL, npb, nsb)
                 + [_layer_spec(layer, D_MODEL, D_FF), _layer_spec(layer, D_FF, D_MODEL),
                    _const_spec(g.shape), _const_spec(b.shape)],
        compiler_params=pltpu.CompilerParams(dimension_semantics=("arbitrary",),
                                             vmem_limit_bytes=_vmem_limit(est)),
        name="ffn",
        **_group_out(hp, hs, tm, npb, nsb),
    )(hp, hs, wu, wd, g, b)


def _pad_lanes(a):
    return jnp.pad(a, ((0, 0), (0, LANES - a.shape[1])))


def _position_tables(pos, c):
    half = HEAD_DIM // 2
    inv = ROPE_BASE ** (-jnp.arange(half, dtype=F32) / half)
    ang = pos[:, None] * inv[None, :]
    cos, sin = jnp.cos(ang), jnp.sin(ang)
    cos2 = jnp.concatenate([cos, cos], axis=1)
    sin2 = jnp.concatenate([-sin, sin], axis=1)
    log_gamma = jnp.log(1.0 - 2.0 ** (-5.0 - jnp.arange(N_HEADS, dtype=F32)))
    idx = jnp.arange(c, dtype=F32)
    rel = idx[:, None] - idx[None, :]
    dec = jnp.where(rel >= 0, jnp.exp(log_gamma[:, None, None] * jnp.maximum(rel, 0.0)), 0.0)
    inter = jnp.swapaxes(jnp.exp(log_gamma[:, None] * (idx + 1.0)[None, :]), 0, 1)
    to_end = jnp.swapaxes(jnp.exp(log_gamma[:, None] * (c - 1.0 - idx)[None, :]), 0, 1)
    cdec = jnp.broadcast_to(jnp.exp(log_gamma * c)[:, None], (N_HEADS, LANES))
    cdec = jnp.pad(cdec, ((0, SUBLANES - N_HEADS), (0, 0)))
    return cos2, sin2, dec, _pad_lanes(inter), _pad_lanes(to_end), cdec


def _pack_w_in(w_in):
    a = 4 * BRANCH_W
    g = 2 * N_HEADS
    wb = w_in.astype(BF16)
    pad = jnp.zeros(wb.shape[:2] + (LANES - g,), BF16)
    return jnp.concatenate([wb[..., :a], wb[..., a + g:], wb[..., a:a + g], pad], axis=-1)


def kernel(x_prompt, x_sample, state_mlstm_C, state_mlstm_n, state_mlstm_m, state_ret, state_hgrn, state_conv,
           w_in, b_mlstm_gate, norm_a, norm_b, norm_c, lb_c, conv_w, w_branch, w_gate, b_gate, w_out,
           ln1_g, ln1_b, ln2_g, ln2_b, w_up, w_down):
    bp, tp, _ = x_prompt.shape
    bs, ts, _ = x_sample.shape
    n_p, n_s = bp * tp, bs * ts
    tm = 512
    tt = 256
    bb = 8
    assert n_p % tm == 0 and n_s % tm == 0 and tp % tt == 0 and bs % bb == 0 and n_p % (bb * ts) == 0

    xp = x_prompt.reshape(n_p, D_MODEL)
    xs = x_sample.reshape(n_s, D_MODEL)
    tabs_p = _position_tables(jnp.arange(tp, dtype=F32), math.gcd(tp, CHUNK))
    tabs_s = _position_tables(PAST_LEN + jnp.arange(ts, dtype=F32), math.gcd(ts, CHUNK))
    lbc = lb_c.astype(F32)
    row = lambda a: a.reshape(1, -1).astype(F32)
    w_in_p = _pack_w_in(w_in)
    wb, wg, wo, wu, wd = (w.astype(BF16) for w in (w_branch, w_gate, w_out, w_up, w_down))
    st_in = (state_mlstm_C.astype(F32), state_mlstm_n.astype(F32),
             jnp.broadcast_to(state_mlstm_m.astype(F32)[..., None], (DEPTH, bs, N_HEADS, LANES)),
             state_ret.astype(F32), state_hgrn.astype(F32), state_conv.astype(F32))

    p_states, s_states = [], ()
    for l in range(DEPTH):
        prms = (_pad_lanes(row(b_mlstm_gate[l])), row(norm_a[l]), row(norm_b[l]), row(norm_c[l]), lbc,
                conv_w[l].astype(F32))
        z = _inproj(xp, xs, l, w_in_p, 256)
        outs_p = _mixer_prompt(z, 0, bp, tp, tt, l, tabs_p, prms)
        outs_s = _mixer_sample(z, n_p, bs, ts, bb, l, tabs_s, prms, st_in, s_states)
        hp, hs = _merge(xp, xs, outs_p[0], outs_s[0], l, wb, wg, row(b_gate[l]), wo, row(ln1_g[l]), row(ln1_b[l]), tm)
        xp, xs = _ffn(hp, hs, l, wu, wd, row(ln2_g[l]), row(ln2_b[l]), tm)
        p_states.append(outs_p[1:])
        s_states = tuple(outs_s[1:])

    cs, ns, ms, srs, shs, bufs = zip(*p_states)
    p_out = (jnp.stack(cs), jnp.stack(ns), jnp.stack([m[..., 0] for m in ms]),
             jnp.stack(srs), jnp.stack(shs), jnp.stack(bufs))
    s_out = s_states[:2] + (s_states[2][..., 0],) + s_states[3:]
    return (xp.reshape(bp, tp, D_MODEL), xs.reshape(bs, ts, D_MODEL)) + p_out + s_out
```

```python
import functools
import math

import jax
import jax.numpy as jnp
import numpy as np
from jax import lax
from jax.experimental import pallas as pl
from jax.experimental.pallas import tpu as pltpu

F32 = jnp.float32
BF16 = jnp.bfloat16

D_MODEL = 1024
DEPTH = 2
BRANCH_W = D_MODEL // 2
N_HEADS = 4
HEAD_DIM = BRANCH_W // N_HEADS
CONV_W = 3
N_BRANCH = 4
D_FF = 4 * D_MODEL
CHUNK = 64
ROPE_BASE = 10000.0
LN_EPS = 1e-5
NEG_BIG = -1e30
GATE_CLAMP = 1.0 - 1e-6
ALPHA = (2 * DEPTH) ** 0.25
PAST_LEN = 16384
K_SCALE = HEAD_DIM ** -0.5

LANES = 128
SUBLANES = 8
VMEM_BYTES = 64 * 1024 * 1024

(QA, KA, VA, OA, QB, KB, VB, GB, QC, FC, IC, GC, XD, BGD, CGD) = [i * BRANCH_W for i in range(15)]
GATE_OFF = 15 * BRANCH_W
P_PAD = GATE_OFF + LANES
Y_W = N_BRANCH * BRANCH_W


def _vmem_limit(est_bytes):
    return int(min(est_bytes + est_bytes // 4 + (4 << 20), VMEM_BYTES - (6 << 20)))


def _dot(a, b):
    return jnp.dot(a.astype(BF16), b.astype(BF16), preferred_element_type=F32)


def _dot_nt(a, b):
    return lax.dot_general(a.astype(BF16), b.astype(BF16), (((1,), (1,)), ((), ())),
                           preferred_element_type=F32)


def _dot_tn(a, b):
    return lax.dot_general(a.astype(BF16), b.astype(BF16), (((0,), (0,)), ((), ())),
                           preferred_element_type=F32)


def _split3(x):
    hi = x.astype(BF16)
    r = x - hi.astype(F32)
    mid = r.astype(BF16)
    lo = (r - mid.astype(F32)).astype(BF16)
    return hi, mid, lo


def _dot_exact_rhs(a01, x):
    hi, mid, lo = _split3(x)
    d = lambda p: jnp.dot(a01, p, preferred_element_type=F32)
    return (d(hi) + d(mid)) + d(lo)


def _dot_nt_exact_rhs(a01, x):
    hi, mid, lo = _split3(x)
    d = lambda p: lax.dot_general(a01, p, (((1,), (1,)), ((), ())), preferred_element_type=F32)
    return (d(hi) + d(mid)) + d(lo)


def _silu(x):
    return x * jax.nn.sigmoid(x)


def _group_specs(tm, width, npb, nsb):
    return [pl.BlockSpec((tm, width), lambda i: (jnp.minimum(i, npb - 1), 0)),
            pl.BlockSpec((tm, width), lambda i: (jnp.clip(i - npb, 0, nsb - 1), 0))]


def _group_load(npb, p_ref, s_ref):
    return jnp.where(pl.program_id(0) < npb, p_ref[...], s_ref[...])


def _group_store(npb, p_ref, s_ref, val):
    is_p = pl.program_id(0) < npb

    @pl.when(is_p)
    def _():
        p_ref[...] = val

    @pl.when(jnp.logical_not(is_p))
    def _():
        s_ref[...] = val


def _inproj_kernel(npb, xp_ref, xs_ref, w_ref, z_ref):
    xb = _group_load(npb, xp_ref, xs_ref).astype(BF16)
    for off in range(0, P_PAD, BRANCH_W):
        wd = min(BRANCH_W, P_PAD - off)
        z_ref[:, off:off + wd] = jnp.dot(xb, w_ref[:, off:off + wd], preferred_element_type=F32)


def _inproj(xp, xs, layer, w, tm):
    npb, nsb = xp.shape[0] // tm, xs.shape[0] // tm
    est = D_MODEL * P_PAD * 2 * 2 + 2 * tm * P_PAD * 4 + 2 * 2 * tm * D_MODEL * 4
    return pl.pallas_call(
        functools.partial(_inproj_kernel, npb),
        grid=(npb + nsb,),
        in_specs=_group_specs(tm, D_MODEL, npb, nsb) + [_layer_spec(layer, D_MODEL, P_PAD)],
        out_specs=pl.BlockSpec((tm, P_PAD), lambda i: (i, 0)),
        out_shape=jax.ShapeDtypeStruct(((npb + nsb) * tm, P_PAD), F32),
        compiler_params=pltpu.CompilerParams(dimension_semantics=("arbitrary",),
                                             vmem_limit_bytes=_vmem_limit(est)),
        name="inproj",
    )(xp, xs, w)


class _State:
    def __init__(self, get, put):
        self.get = get
        self.put = put


def _mixer_chunk(zc, c, st, cst, prm, put_y):
    tril, causal, sel8 = cst["tril"], cst["causal"], cst["sel8"]
    old = {(name, h): st.get(name, h) for name in ("C", "n", "m", "Sr", "ShT") for h in range(N_HEADS)}
    old[("buf", 0)] = st.get("buf", 0)
    new = {}

    heads = range(N_HEADS)
    hd = lambda off, h: zc(off + h * HEAD_DIM, HEAD_DIM)
    hsl = lambda a, h: a[:, h * HEAD_DIM:(h + 1) * HEAD_DIM]
    cos2, sin2 = cst["cos2"], cst["sin2"]
    rope = lambda x: x * cos2 + pltpu.roll(x, HEAD_DIM // 2, axis=1) * sin2
    lv = cst["gla_lv"][...]

    gp = zc(GATE_OFF, LANES) + prm["bias_if"]
    lf = jnp.minimum(gp, 0.0) - jnp.log1p(jnp.exp(-jnp.abs(gp)))
    b_all = _dot_exact_rhs(tril, lf)
    gp_t = _dot_nt_exact_rhs(sel8, gp)
    kc_all = (1.0 - prm["lb"]) * jax.nn.sigmoid(-zc(FC, BRANCH_W))
    g_all = jnp.log1p(-jnp.minimum(kc_all, GATE_CLAMP))
    xs = _dot_exact_rhs(cst["gla_m"][...], g_all)
    qg_all = _silu(zc(QC, BRANCH_W))
    s_a, qc_a, s_b, qs_b, a_c = [], [], [], [], []
    for h in heads:
        qa = hd(QA, h)
        s_a.append(_dot_nt(qa, hd(KA, h) * K_SCALE))
        qc_a.append(_dot(qa, old[("C", h)]))
    for h in heads:
        qb = rope(hd(QB, h))
        kb = rope(hd(KB, h)) * K_SCALE
        s_b.append(_dot_nt(qb, kb))
        qs_b.append(_dot(qb, old[("Sr", h)]))
        new[("Sr", h)] = (cst["ret_cdec"][h:h + 1, :] * old[("Sr", h)]
                          + _dot_tn(kb * cst["ret_toend"][:, h:h + 1], hd(VB, h)))
    for h in heads:
        qg, kc = hsl(qg_all, h), hsl(kc_all, h)
        a = jnp.where(lv == 0, _dot_nt(qg, kc), 0.0)
        a_c.append(jnp.where(lv == 1, _dot_nt(qg * jnp.exp(hsl(g_all, h)), kc), a))

    u = zc(CGD, BRANCH_W) * zc(XD, BRANCH_W)
    buf = old[("buf", 0)]
    row = lax.broadcasted_iota(jnp.int32, (c, BRANCH_W), 0)
    u1 = jnp.where(row == 0, buf[1:2, :], pltpu.roll(u, 1, axis=0))
    u2 = jnp.where(row == 0, buf[0:1, :], jnp.where(row == 1, buf[1:2, :], pltpu.roll(u, 2, axis=0)))
    cw = prm["conv_w"]
    yconv = (u2 * cw[0:1, :] + u1 * cw[1:2, :]) + u * cw[2:3, :]
    put_y(3 * BRANCH_W, zc(BGD, BRANCH_W) * yconv)
    new[("buf", 0)] = u[c - 2:c, :]
    qn_a = [jnp.sum(hd(QA, h) * old[("n", h)], axis=1, keepdims=True) for h in heads]

    b_t = _dot_nt_exact_rhs(sel8, b_all)
    for h in heads:
        b_col = b_all[:, N_HEADS + h:N_HEADS + h + 1]
        m_prev = old[("m", h)][:, 0:1]
        b_last = b_col[c - 1:c, :]
        g_col = (b_last - b_col) + gp[:, h:h + 1]
        m_new = jnp.maximum(b_last + m_prev, jnp.max(g_col, axis=0, keepdims=True))
        dec = jnp.exp(b_last + m_prev - m_new)
        kw = (hd(KA, h) * K_SCALE) * jnp.exp(g_col - m_new)
        new[("C", h)] = dec * old[("C", h)] + _dot_tn(kw, hd(VA, h))
        new[("n", h)] = dec * old[("n", h)] + jnp.sum(kw, axis=0, keepdims=True)
        new[("m", h)] = jnp.broadcast_to(m_new, (1, LANES))
    o_b = []
    for h in heads:
        o_b.append(_dot(s_b[h] * cst["ret_dec"](h), hd(VB, h)) + qs_b[h] * cst["ret_inter"][:, h:h + 1])
    oi_c = []
    for h in heads:
        qg, kc, bc = hsl(qg_all, h), hsl(kc_all, h), hsl(xs[0:c], h)
        a, sec, n = a_c[h], 1, c // 2
        while n > 1:
            kt = kc * jnp.exp(hsl(xs[sec * c:(sec + 1) * c], h))
            qt = qg * jnp.exp(hsl(xs[(sec + 1) * c:(sec + 2) * c], h))
            a = jnp.where(lv == n, _dot_nt(qt, kt), a)
            sec, n = sec + 2, n // 2
        a_c[h] = a
        sh_t = old[("ShT", h)]
        oi_c.append(_dot_nt(qg * jnp.exp(bc), sh_t))
        b_last = bc[c - 1:c, :]
        new[("ShT", h)] = jnp.exp(b_last) * sh_t + _dot_tn(hd(IC, h), kc * jnp.exp(b_last - bc))

    yb = []
    for h in heads:
        oc = o_b[h] - jnp.mean(o_b[h], axis=1, keepdims=True)
        on = oc * lax.rsqrt(jnp.mean(oc * oc, axis=1, keepdims=True) + LN_EPS)
        yb.append(_silu(hd(GB, h)) * (on * hsl(prm["norm_b"], h)))
    put_y(BRANCH_W, jnp.concatenate(yb, axis=1))
    num_a, den_a = [], []
    for h in heads:
        b_col = b_all[:, N_HEADS + h:N_HEADS + h + 1]
        b_row = b_t[N_HEADS + h:N_HEADS + h + 1, :]
        a_col = b_col + old[("m", h)][:, 0:1]
        dmat = jnp.where(causal, (b_col - b_row) + gp_t[h:h + 1, :], NEG_BIG)
        m_t = jnp.maximum(a_col, jnp.max(dmat, axis=1, keepdims=True))
        w_inter = jnp.exp(a_col - m_t)
        s = s_a[h] * jnp.exp(dmat - m_t)
        num_a.append(_dot(s, hd(VA, h)) + qc_a[h] * w_inter)
        den = jnp.sum(s, axis=1, keepdims=True) + qn_a[h] * w_inter
        den_a.append(jnp.maximum(jnp.abs(den), jnp.exp(-m_t)))
    o_c = [_dot(a_c[h], hd(IC, h)) + oi_c[h] for h in heads]

    ya, yc = [], []
    for h in heads:
        hh = num_a[h] / den_a[h]
        hn = hh * lax.rsqrt(jnp.mean(hh * hh, axis=1, keepdims=True) + LN_EPS)
        ya.append(jax.nn.sigmoid(hd(OA, h)) * (hn * hsl(prm["norm_a"], h)))
    put_y(0, jnp.concatenate(ya, axis=1))
    for h in heads:
        on = o_c[h] * lax.rsqrt(jnp.mean(o_c[h] * o_c[h], axis=1, keepdims=True) + LN_EPS)
        yc.append(_silu(hd(GC, h)) * (on * hsl(prm["norm_c"], h)))
    put_y(2 * BRANCH_W, jnp.concatenate(yc, axis=1))
    for (name, h), val in new.items():
        st.put(name, h, val)


def _chunk_consts(c):
    r = lax.broadcasted_iota(jnp.int32, (c, c), 0)
    l = lax.broadcasted_iota(jnp.int32, (c, c), 1)
    causal = r >= l
    r8 = lax.broadcasted_iota(jnp.int32, (SUBLANES, LANES), 0)
    l8 = lax.broadcasted_iota(jnp.int32, (SUBLANES, LANES), 1)
    return {"causal": causal,
            "tril": jnp.where(causal, 1.0, 0.0).astype(BF16),
            "sel8": jnp.where(r8 == l8, 1.0, 0.0).astype(BF16)}


def _layer_lb(lbc, layer):
    e = jnp.exp(lbc - jnp.max(lbc, axis=0, keepdims=True))
    sm = e / jnp.sum(e, axis=0, keepdims=True)
    cum = sm[0:1, :]
    for j in range(1, layer + 1):
        cum = cum + sm[j:j + 1, :]
    return cum - sm[0:1, :]


def _load_params(bias_ref, na_ref, nb_ref, nc_ref, lbc_ref, cw_ref, layer):
    return {"bias_if": bias_ref[...], "norm_a": na_ref[...], "norm_b": nb_ref[...], "norm_c": nc_ref[...],
            "lb": _layer_lb(lbc_ref[...], layer), "conv_w": cw_ref[...]}


def _mixer_prompt_kernel(layer, c, tt,
                         z_ref, cos_ref, sin_ref, rdec_ref, rinter_ref, rtoend_ref, rcdec_ref,
                         bias_ref, na_ref, nb_ref, nc_ref, lbc_ref, cw_ref, gm_ref, lv_ref,
                         y_ref, c_out, n_out, m_out, sr_out, sh_out, buf_out,
                         c_s, n_s, m_s, sr_s, sht_s, buf_s, ybuf):
    j = pl.program_id(1)

    @pl.when(j == 0)
    def _():
        c_s[...] = jnp.zeros_like(c_s)
        n_s[...] = jnp.zeros_like(n_s)
        m_s[...] = jnp.zeros_like(m_s)
        sr_s[...] = jnp.zeros_like(sr_s)
        sht_s[...] = jnp.zeros_like(sht_s)
        buf_s[...] = jnp.zeros_like(buf_s)

    prm = _load_params(bias_ref, na_ref, nb_ref, nc_ref, lbc_ref, cw_ref, layer)
    cst = _chunk_consts(c)
    cst["ret_dec"] = lambda h: rdec_ref[h]
    cst["ret_inter"] = rinter_ref[...]
    cst["ret_toend"] = rtoend_ref[...]
    cst["ret_cdec"] = rcdec_ref[...]
    cst["gla_m"] = gm_ref
    cst["gla_lv"] = lv_ref

    def get(name, h):
        if name == "C":
            return c_s[h]
        if name == "n":
            return n_s[h:h + 1, :]
        if name == "m":
            return m_s[h:h + 1, :]
        if name == "Sr":
            return sr_s[h]
        if name == "ShT":
            return sht_s[h]
        return buf_s[0:CONV_W - 1, :]

    def put(name, h, val):
        if name == "C":
            c_s[h] = val
        elif name == "n":
            n_s[h:h + 1, :] = val
        elif name == "m":
            m_s[h:h + 1, :] = val
        elif name == "Sr":
            sr_s[h] = val
        elif name == "ShT":
            sht_s[h] = val
        else:
            buf_s[0:CONV_W - 1, :] = val

    st = _State(get, put)

    def body(ci, carry):
        r = pl.multiple_of(ci * c, c)
        zc = lambda off, w: z_ref[pl.ds(r, c), off:off + w]
        cc = dict(cst)
        cc["cos2"] = cos_ref[pl.ds(r, c), :]
        cc["sin2"] = sin_ref[pl.ds(r, c), :]

        def put_y(off, val):
            ybuf[pl.ds(r, c), off:off + BRANCH_W] = val

        _mixer_chunk(zc, c, st, cc, prm, put_y)
        return carry

    lax.fori_loop(0, tt // c, body, 0)
    y_ref[...] = ybuf[...].astype(BF16)

    @pl.when(j == pl.num_programs(1) - 1)
    def _():
        c_out[0] = c_s[...]
        n_out[0] = n_s[0:N_HEADS, :]
        m_out[0] = m_s[0:N_HEADS, :]
        sr_out[0] = sr_s[...]
        for h in range(N_HEADS):
            sh_out[0, h] = sht_s[h].T
        buf_out[0] = buf_s[0:CONV_W - 1, :]


def _const_spec(shape):
    nd = len(shape)
    return pl.BlockSpec(shape, lambda *_: (0,) * nd)


def _mixer_prompt(z, row0, bsz, t, tt, layer, tabs, prms):
    c = math.gcd(t, CHUNK)
    nj = t // tt
    blk0 = row0 // tt
    cos2, sin2, rdec, rinter, rtoend, rcdec = tabs
    state_shapes = [
        jax.ShapeDtypeStruct((bsz, N_HEADS, HEAD_DIM, HEAD_DIM), F32),
        jax.ShapeDtypeStruct((bsz, N_HEADS, HEAD_DIM), F32),
        jax.ShapeDtypeStruct((bsz, N_HEADS, LANES), F32),
        jax.ShapeDtypeStruct((bsz, N_HEADS, HEAD_DIM, HEAD_DIM), F32),
        jax.ShapeDtypeStruct((bsz, N_HEADS, HEAD_DIM, HEAD_DIM), F32),
        jax.ShapeDtypeStruct((bsz, CONV_W - 1, BRANCH_W), F32),
    ]
    big = pl.BlockSpec((1, N_HEADS, HEAD_DIM, HEAD_DIM), lambda b, j: (b, 0, 0, 0))
    small = pl.BlockSpec((1, N_HEADS, LANES), lambda b, j: (b, 0, 0))
    est = 2 * tt * P_PAD * 4 + 2 * tt * Y_W * 2 + tt * Y_W * 4 + 8 * N_HEADS * HEAD_DIM * HEAD_DIM * 4
    outs = pl.pallas_call(
        functools.partial(_mixer_prompt_kernel, layer, c, tt),
        grid=(bsz, nj),
        in_specs=[pl.BlockSpec((tt, P_PAD), lambda b, j: (blk0 + b * nj + j, 0)),
                  pl.BlockSpec((tt, LANES), lambda b, j: (j, 0)),
                  pl.BlockSpec((tt, LANES), lambda b, j: (j, 0)),
                  _const_spec(rdec.shape), _const_spec(rinter.shape), _const_spec(rtoend.shape),
                  _const_spec(rcdec.shape)] + [_const_spec(p.shape) for p in prms],
        out_specs=[pl.BlockSpec((tt, Y_W), lambda b, j: (b * nj + j, 0)),
                   big, small, small, big, big,
                   pl.BlockSpec((1, CONV_W - 1, BRANCH_W), lambda b, j: (b, 0, 0))],
        out_shape=[jax.ShapeDtypeStruct((bsz * t, Y_W), BF16)] + state_shapes,
        scratch_shapes=[pltpu.VMEM((N_HEADS, HEAD_DIM, HEAD_DIM), F32),
                        pltpu.VMEM((SUBLANES, LANES), F32),
                        pltpu.VMEM((SUBLANES, LANES), F32),
                        pltpu.VMEM((N_HEADS, HEAD_DIM, HEAD_DIM), F32),
                        pltpu.VMEM((N_HEADS, HEAD_DIM, HEAD_DIM), F32),
                        pltpu.VMEM((SUBLANES, BRANCH_W), F32),
                        pltpu.VMEM((tt, Y_W), F32)],
        compiler_params=pltpu.CompilerParams(dimension_semantics=("arbitrary", "arbitrary"),
                                             vmem_limit_bytes=_vmem_limit(est)),
        name=f"mixer_prompt_l{layer}",
    )(z, cos2, sin2, rdec, rinter, rtoend, rcdec, *prms)
    return outs


N_STATES = 6
N_MIXER_CONST_INPUTS = 15


def _mixer_sample_kernel(layer, c, bb, n_carried, *refs):
    (z_ref, cos_ref, sin_ref, rdec_ref, rinter_ref, rtoend_ref, rcdec_ref,
     bias_ref, na_ref, nb_ref, nc_ref, lbc_ref, cw_ref, gm_ref, lv_ref) = refs[:N_MIXER_CONST_INPUTS]
    c_in, n_in, m_in, sr_in, sh_in, buf_in = refs[N_MIXER_CONST_INPUTS:N_MIXER_CONST_INPUTS + N_STATES]
    y_ref, c_out, n_out, m_out, sr_out, sh_out, buf_out, ybuf = refs[N_MIXER_CONST_INPUTS + N_STATES + n_carried:]
    prm = _load_params(bias_ref, na_ref, nb_ref, nc_ref, lbc_ref, cw_ref, layer)
    cst = _chunk_consts(c)
    cst["ret_dec"] = lambda h: rdec_ref[h]
    cst["ret_inter"] = rinter_ref[...]
    cst["ret_toend"] = rtoend_ref[...]
    cst["ret_cdec"] = rcdec_ref[...]
    cst["gla_m"] = gm_ref
    cst["gla_lv"] = lv_ref
    cst["cos2"] = cos_ref[...]
    cst["sin2"] = sin_ref[...]

    def body(i, carry):
        r = pl.multiple_of(i * c, c)
        zc = lambda off, w: z_ref[pl.ds(r, c), off:off + w]

        def get(name, h):
            if name == "C":
                return c_in[i, h]
            if name == "n":
                return n_in[i, pl.ds(h, 1), :]
            if name == "m":
                return m_in[i, pl.ds(h, 1), :]
            if name == "Sr":
                return sr_in[i, h]
            if name == "ShT":
                return sh_in[i, h].T
            return buf_in[i]

        def put(name, h, val):
            if name == "C":
                c_out[i, h] = val
            elif name == "n":
                n_out[i, pl.ds(h, 1), :] = val
            elif name == "m":
                m_out[i, pl.ds(h, 1), :] = val
            elif name == "Sr":
                sr_out[i, h] = val
            elif name == "ShT":
                sh_out[i, h] = val.T
            else:
                buf_out[i] = val

        def put_y(off, val):
            ybuf[pl.ds(r, c), off:off + BRANCH_W] = val

        _mixer_chunk(zc, c, _State(get, put), cst, prm, put_y)
        return carry

    lax.fori_loop(0, bb, body, 0)
    y_ref[...] = ybuf[...].astype(BF16)


def _mixer_sample(z, row0, bsz, t, bb, layer, tabs, prms, states, carried):
    c = t
    rows = bb * t
    blk0 = row0 // rows
    cos2, sin2, rdec, rinter, rtoend, rcdec = tabs

    def lspec(*tail):
        zeros = (0,) * (len(tail) - 1)
        return pl.BlockSpec((None,) + tail, lambda i: (layer, i) + zeros)

    big = lspec(bb, N_HEADS, HEAD_DIM, HEAD_DIM)
    small = lspec(bb, N_HEADS, LANES)
    state_specs = [big, small, small, big, big, lspec(bb, CONV_W - 1, BRANCH_W)]
    state_shapes = [jax.ShapeDtypeStruct(s.shape, F32) for s in states]
    n_in = N_MIXER_CONST_INPUTS + N_STATES
    est = (2 * rows * P_PAD * 4 + 2 * rows * Y_W * 2 + rows * Y_W * 4
           + 2 * 2 * 3 * bb * N_HEADS * HEAD_DIM * HEAD_DIM * 4)
    return pl.pallas_call(
        functools.partial(_mixer_sample_kernel, layer, c, bb, len(carried)),
        grid=(bsz // bb,),
        in_specs=[pl.BlockSpec((rows, P_PAD), lambda i: (blk0 + i, 0)),
                  _const_spec(cos2.shape), _const_spec(sin2.shape),
                  _const_spec(rdec.shape), _const_spec(rinter.shape), _const_spec(rtoend.shape),
                  _const_spec(rcdec.shape)] + [_const_spec(p.shape) for p in prms] + state_specs
                 + [pl.BlockSpec(memory_space=pl.ANY)] * len(carried),
        out_specs=[pl.BlockSpec((rows, Y_W), lambda i: (i, 0))] + state_specs,
        out_shape=[jax.ShapeDtypeStruct((bsz * t, Y_W), BF16)] + state_shapes,
        input_output_aliases={n_in + k: 1 + k for k in range(len(carried))},
        scratch_shapes=[pltpu.VMEM((rows, Y_W), F32)],
        compiler_params=pltpu.CompilerParams(dimension_semantics=("arbitrary",),
                                             vmem_limit_bytes=_vmem_limit(est)),
        name=f"mixer_sample_l{layer}",
    )(z, cos2, sin2, rdec, rinter, rtoend, rcdec, *prms, *states, *carried)


def _layer_norm(r, g, b):
    mu = jnp.mean(r, axis=1, keepdims=True)
    rc = r - mu
    var = jnp.mean(rc * rc, axis=1, keepdims=True)
    return rc * lax.rsqrt(var + LN_EPS) * g + b


def _merge_kernel(npb, xp_ref, xs_ref, yp_ref, ys_ref, wb_ref, wg_ref, bg_ref, wo_ref, g_ref, b_ref,
                  hp_ref, hs_ref):
    x = _group_load(npb, xp_ref, xs_ref)
    xb = x.astype(BF16)
    y = _group_load(npb, yp_ref, ys_ref)
    acc = None
    for n in range(N_BRANCH):
        proj = jnp.dot(y[:, n * BRANCH_W:(n + 1) * BRANCH_W], wb_ref[n], preferred_element_type=F32)
        gate = jax.nn.sigmoid(jnp.dot(xb, wg_ref[:, n * D_MODEL:(n + 1) * D_MODEL], preferred_element_type=F32)
                              + bg_ref[:, n * D_MODEL:(n + 1) * D_MODEL])
        acc = gate * proj if acc is None else acc + gate * proj
    mix = jnp.dot(acc.astype(BF16), wo_ref[...], preferred_element_type=F32)
    _group_store(npb, hp_ref, hs_ref, _layer_norm(ALPHA * x + mix, g_ref[...], b_ref[...]))


def _layer_spec(layer, *tail):
    return pl.BlockSpec((None,) + tail, lambda *_: (layer,) + (0,) * len(tail))


def _group_out(xp, xs, tm, npb, nsb):
    return dict(out_specs=_group_specs(tm, D_MODEL, npb, nsb),
                out_shape=[jax.ShapeDtypeStruct(xp.shape, F32), jax.ShapeDtypeStruct(xs.shape, F32)])


def _merge(xp, xs, y_p, y_s, layer, wb, wg, bg, wo, g, b, tm):
    npb, nsb = xp.shape[0] // tm, xs.shape[0] // tm
    est = (2 * (N_BRANCH * BRANCH_W * D_MODEL + D_MODEL * N_BRANCH * D_MODEL + D_MODEL * D_MODEL) * 2
           + 2 * 2 * tm * D_MODEL * 4 * 2 + 2 * 2 * tm * Y_W * 2 + 6 * tm * D_MODEL * 4)
    return pl.pallas_call(
        functools.partial(_merge_kernel, npb),
        grid=(npb + nsb,),
        in_specs=_group_specs(tm, D_MODEL, npb, nsb) + _group_specs(tm, Y_W, npb, nsb)
                 + [_layer_spec(layer, N_BRANCH, BRANCH_W, D_MODEL), _layer_spec(layer, D_MODEL, N_BRANCH * D_MODEL),
                    _const_spec(bg.shape), _layer_spec(layer, D_MODEL, D_MODEL),
                    _const_spec(g.shape), _const_spec(b.shape)],
        compiler_params=pltpu.CompilerParams(dimension_semantics=("arbitrary",),
                                             vmem_limit_bytes=_vmem_limit(est)),
        name="merge",
        **_group_out(xp, xs, tm, npb, nsb),
    )(xp, xs, y_p, y_s, wb, wg, bg, wo, g, b)


def _ffn_kernel(npb, hp_ref, hs_ref, wu_ref, wd_ref, g_ref, b_ref, op_ref, os_ref):
    h = _group_load(npb, hp_ref, hs_ref)
    u = jnp.maximum(jnp.dot(h.astype(BF16), wu_ref[...], preferred_element_type=F32), 0.0)
    ff = jnp.dot((u * u).astype(BF16), wd_ref[...], preferred_element_type=F32)
    _group_store(npb, op_ref, os_ref, _layer_norm(ALPHA * h + ff, g_ref[...], b_ref[...]))


def _ffn(hp, hs, layer, wu, wd, g, b, tm):
    npb, nsb = hp.shape[0] // tm, hs.shape[0] // tm
    est = 2 * 2 * D_MODEL * D_FF * 2 + 2 * 2 * 2 * tm * D_MODEL * 4 + tm * D_FF * 6 + 2 * tm * D_MODEL * 4
    return pl.pallas_call(
        functools.partial(_ffn_kernel, npb),
        grid=(npb + nsb,),
        in_specs=_group_specs(tm, D_MODEL, npb, nsb)
                 + [_layer_spec(layer, D_MODEL, D_FF), _layer_spec(layer, D_FF, D_MODEL),
                    _const_spec(g.shape), _const_spec(b.shape)],
        compiler_params=pltpu.CompilerParams(dimension_semantics=("arbitrary",),
                                             vmem_limit_bytes=_vmem_limit(est)),
        name="ffn",
        **_group_out(hp, hs, tm, npb, nsb),
    )(hp, hs, wu, wd, g, b)


def _pad_lanes(a):
    return jnp.pad(a, ((0, 0), (0, LANES - a.shape[1])))


def _position_tables(pos, c):
    half = HEAD_DIM // 2
    inv = ROPE_BASE ** (-jnp.arange(half, dtype=F32) / half)
    ang = pos[:, None] * inv[None, :]
    cos, sin = jnp.cos(ang), jnp.sin(ang)
    cos2 = jnp.concatenate([cos, cos], axis=1)
    sin2 = jnp.concatenate([-sin, sin], axis=1)
    log_gamma = jnp.log(1.0 - 2.0 ** (-5.0 - jnp.arange(N_HEADS, dtype=F32)))
    idx = jnp.arange(c, dtype=F32)
    rel = idx[:, None] - idx[None, :]
    dec = jnp.where(rel >= 0, jnp.exp(log_gamma[:, None, None] * jnp.maximum(rel, 0.0)), 0.0)
    inter = jnp.swapaxes(jnp.exp(log_gamma[:, None] * (idx + 1.0)[None, :]), 0, 1)
    to_end = jnp.swapaxes(jnp.exp(log_gamma[:, None] * (c - 1.0 - idx)[None, :]), 0, 1)
    cdec = jnp.broadcast_to(jnp.exp(log_gamma * c)[:, None], (N_HEADS, LANES))
    cdec = jnp.pad(cdec, ((0, SUBLANES - N_HEADS), (0, 0)))
    return cos2, sin2, dec, _pad_lanes(inter), _pad_lanes(to_end), cdec


def _gla_tables(c):
    r = np.arange(c)[:, None]
    j = np.arange(c)[None, :]
    mats = [j <= r]
    n = c // 2
    while n > 1:
        start = (r // n) * n
        mats.append((j > r) & (j <= start + n - 1))
        mats.append((j >= start) & (j <= r))
        n //= 2
    x = np.maximum(r ^ j, 1)
    level = np.where(j > r, -1, np.where(j == r, 0, 1 << np.floor(np.log2(x)).astype(np.int64)))
    return jnp.asarray(np.concatenate(mats, 0), BF16), jnp.asarray(level, jnp.int32)


def _pack_w_in(w_in):
    a = 4 * BRANCH_W
    g = 2 * N_HEADS
    wb = w_in.astype(BF16)
    pad = jnp.zeros(wb.shape[:2] + (LANES - g,), BF16)
    return jnp.concatenate([wb[..., :a], wb[..., a + g:], wb[..., a:a + g], pad], axis=-1)


def kernel(x_prompt, x_sample, state_mlstm_C, state_mlstm_n, state_mlstm_m, state_ret, state_hgrn, state_conv,
           w_in, b_mlstm_gate, norm_a, norm_b, norm_c, lb_c, conv_w, w_branch, w_gate, b_gate, w_out,
           ln1_g, ln1_b, ln2_g, ln2_b, w_up, w_down):
    bp, tp, _ = x_prompt.shape
    bs, ts, _ = x_sample.shape
    n_p, n_s = bp * tp, bs * ts
    tm = 512
    tt = 256
    bb = 8
    assert n_p % tm == 0 and n_s % tm == 0 and tp % tt == 0 and bs % bb == 0 and n_p % (bb * ts) == 0

    xp = x_prompt.reshape(n_p, D_MODEL)
    xs = x_sample.reshape(n_s, D_MODEL)
    tabs_p = _position_tables(jnp.arange(tp, dtype=F32), math.gcd(tp, CHUNK))
    tabs_s = _position_tables(PAST_LEN + jnp.arange(ts, dtype=F32), math.gcd(ts, CHUNK))
    gla_p = _gla_tables(math.gcd(tp, CHUNK))
    gla_s = _gla_tables(math.gcd(ts, CHUNK))
    lbc = lb_c.astype(F32)
    row = lambda a: a.reshape(1, -1).astype(F32)
    w_in_p = _pack_w_in(w_in)
    wb, wg, wo, wu, wd = (w.astype(BF16) for w in (w_branch, w_gate, w_out, w_up, w_down))
    st_in = (state_mlstm_C.astype(F32), state_mlstm_n.astype(F32),
             jnp.broadcast_to(state_mlstm_m.astype(F32)[..., None], (DEPTH, bs, N_HEADS, LANES)),
             state_ret.astype(F32), state_hgrn.astype(F32), state_conv.astype(F32))

    p_states, s_states = [], ()
    for l in range(DEPTH):
        prms = (_pad_lanes(row(b_mlstm_gate[l])), row(norm_a[l]), row(norm_b[l]), row(norm_c[l]), lbc,
                conv_w[l].astype(F32))
        z = _inproj(xp, xs, l, w_in_p, 256)
        outs_p = _mixer_prompt(z, 0, bp, tp, tt, l, tabs_p, prms + gla_p)
        outs_s = _mixer_sample(z, n_p, bs, ts, bb, l, tabs_s, prms + gla_s, st_in, s_states)
        hp, hs = _merge(xp, xs, outs_p[0], outs_s[0], l, wb, wg, row(b_gate[l]), wo, row(ln1_g[l]), row(ln1_b[l]), tm)
        xp, xs = _ffn(hp, hs, l, wu, wd, row(ln2_g[l]), row(ln2_b[l]), tm)
        p_states.append(outs_p[1:])
        s_states = tuple(outs_s[1:])

    cs, ns, ms, srs, shs, bufs = zip(*p_states)
    p_out = (jnp.stack(cs), jnp.stack(ns), jnp.stack([m[..., 0] for m in ms]),
             jnp.stack(srs), jnp.stack(shs), jnp.stack(bufs))
    s_out = s_states[:2] + (s_states[2][..., 0],) + s_states[3:]
    return (xp.reshape(bp, tp, D_MODEL), xs.reshape(bs, ts, D_MODEL)) + p_out + s_out
```

```python
import functools
import math

import jax
import jax.numpy as jnp
import numpy as np
from jax import lax
from jax.experimental import pallas as pl
from jax.experimental.pallas import tpu as pltpu

F32 = jnp.float32
BF16 = jnp.bfloat16

D_MODEL = 1024
DEPTH = 2
BRANCH_W = D_MODEL // 2
N_HEADS = 4
HEAD_DIM = BRANCH_W // N_HEADS
CONV_W = 3
N_BRANCH = 4
D_FF = 4 * D_MODEL
CHUNK = 64
ROPE_BASE = 10000.0
LN_EPS = 1e-5
NEG_BIG = -1e30
GATE_CLAMP = 1.0 - 1e-6
ALPHA = (2 * DEPTH) ** 0.25
PAST_LEN = 16384
K_SCALE = HEAD_DIM ** -0.5

LANES = 128
SUBLANES = 8
VMEM_BYTES = 64 * 1024 * 1024

(QA, KA, VA, OA, QB, KB, VB, GB, QC, FC, IC, GC, XD, BGD, CGD) = [i * BRANCH_W for i in range(15)]
GATE_OFF = 15 * BRANCH_W
P_PAD = GATE_OFF + LANES
Y_W = N_BRANCH * BRANCH_W


def _vmem_limit(est_bytes):
    return int(min(est_bytes + est_bytes // 4 + (4 << 20), VMEM_BYTES - (6 << 20)))


def _dot(a, b):
    return jnp.dot(a.astype(BF16), b.astype(BF16), preferred_element_type=F32)


def _dot_nt(a, b):
    return lax.dot_general(a.astype(BF16), b.astype(BF16), (((1,), (1,)), ((), ())),
                           preferred_element_type=F32)


def _dot_tn(a, b):
    return lax.dot_general(a.astype(BF16), b.astype(BF16), (((0,), (0,)), ((), ())),
                           preferred_element_type=F32)


def _split3(x):
    hi = x.astype(BF16)
    r = x - hi.astype(F32)
    mid = r.astype(BF16)
    lo = (r - mid.astype(F32)).astype(BF16)
    return hi, mid, lo


def _dot_exact_rhs(a01x3, x):
    parts = _split3(x)
    if x.shape[0] % (2 * SUBLANES) == 0:
        stacked = jnp.concatenate(parts, axis=0)
    else:
        stacked = jnp.concatenate([p.astype(F32) for p in parts], axis=0).astype(BF16)
    return jnp.dot(a01x3, stacked, preferred_element_type=F32)


def _dot_nt_exact_rhs(a01, x):
    hi, mid, lo = _split3(x)
    d = lambda p: lax.dot_general(a01, p, (((1,), (1,)), ((), ())), preferred_element_type=F32)
    return (d(hi) + d(mid)) + d(lo)


def _silu(x):
    return x * jax.nn.sigmoid(x)


def _group_specs(tm, width, npb, nsb):
    return [pl.BlockSpec((tm, width), lambda i: (jnp.minimum(i, npb - 1), 0)),
            pl.BlockSpec((tm, width), lambda i: (jnp.clip(i - npb, 0, nsb - 1), 0))]


def _group_load(npb, p_ref, s_ref):
    return jnp.where(pl.program_id(0) < npb, p_ref[...], s_ref[...])


def _group_store(npb, p_ref, s_ref, val):
    is_p = pl.program_id(0) < npb

    @pl.when(is_p)
    def _():
        p_ref[...] = val

    @pl.when(jnp.logical_not(is_p))
    def _():
        s_ref[...] = val


def _inproj_kernel(npb, xp_ref, xs_ref, w_ref, z_ref):
    xb = _group_load(npb, xp_ref, xs_ref).astype(BF16)
    for off in range(0, P_PAD, BRANCH_W):
        wd = min(BRANCH_W, P_PAD - off)
        z_ref[:, off:off + wd] = jnp.dot(xb, w_ref[:, off:off + wd], preferred_element_type=F32)


def _inproj(xp, xs, layer, w, tm):
    npb, nsb = xp.shape[0] // tm, xs.shape[0] // tm
    est = D_MODEL * P_PAD * 2 * 2 + 2 * tm * P_PAD * 4 + 2 * 2 * tm * D_MODEL * 4
    return pl.pallas_call(
        functools.partial(_inproj_kernel, npb),
        grid=(npb + nsb,),
        in_specs=_group_specs(tm, D_MODEL, npb, nsb) + [_layer_spec(layer, D_MODEL, P_PAD)],
        out_specs=pl.BlockSpec((tm, P_PAD), lambda i: (i, 0)),
        out_shape=jax.ShapeDtypeStruct(((npb + nsb) * tm, P_PAD), F32),
        compiler_params=pltpu.CompilerParams(dimension_semantics=("arbitrary",),
                                             vmem_limit_bytes=_vmem_limit(est)),
        name="inproj",
    )(xp, xs, w)


class _State:
    def __init__(self, get, put):
        self.get = get
        self.put = put


def _pipelined(chunk_generators):
    pending_tail = None
    for g in chunk_generators:
        next(g)
        if pending_tail is not None:
            next(pending_tail, None)
        next(g)
        pending_tail = g
    next(pending_tail, None)


def _mixer_chunk(zc, c, st, cst, prm, put_y):
    tril, causal, sel8 = cst["tril"], cst["causal"], cst["sel8"]
    old = {(name, h): st.get(name, h) for name in ("C", "n", "m", "Sr", "ShT") for h in range(N_HEADS)}
    old[("buf", 0)] = st.get("buf", 0)
    new = {}

    heads = range(N_HEADS)
    hd = lambda off, h: zc(off + h * HEAD_DIM, HEAD_DIM)
    hsl = lambda a, h: a[:, h * HEAD_DIM:(h + 1) * HEAD_DIM]
    cos2, sin2 = cst["cos2"], cst["sin2"]
    rope = lambda x: x * cos2 + pltpu.roll(x, HEAD_DIM // 2, axis=1) * sin2
    lv = cst["gla_lv"][...]

    gp = zc(GATE_OFF, LANES) + prm["bias_if"]
    lf = jnp.minimum(gp, 0.0) - jnp.log1p(jnp.exp(-jnp.abs(gp)))
    b_all = _dot_exact_rhs(tril, lf)
    gp_t = _dot_nt_exact_rhs(sel8, gp)
    s_a, qc_a, s_b, qs_b, a_c = [], [], [], [], []
    for h in heads:
        qa = hd(QA, h)
        s_a.append(_dot_nt(qa, hd(KA, h) * K_SCALE))
        qc_a.append(_dot(qa, old[("C", h)]))
    for h in heads:
        qb = rope(hd(QB, h))
        kb = rope(hd(KB, h)) * K_SCALE
        s_b.append(_dot_nt(qb, kb))
        qs_b.append(_dot(qb, old[("Sr", h)]))
        new[("Sr", h)] = (cst["ret_cdec"][h:h + 1, :] * old[("Sr", h)]
                          + _dot_tn(kb * cst["ret_toend"][:, h:h + 1], hd(VB, h)))
    kc_all = (1.0 - prm["lb"]) * jax.nn.sigmoid(-zc(FC, BRANCH_W))
    g_all = jnp.log1p(-jnp.minimum(kc_all, GATE_CLAMP))
    xs = _dot_exact_rhs(cst["gla_m"][...], g_all)
    qg_all = _silu(zc(QC, BRANCH_W))
    for h in heads:
        qg, kc = hsl(qg_all, h), hsl(kc_all, h)
        a = jnp.where(lv == 0, _dot_nt(qg, kc), 0.0)
        a_c.append(jnp.where(lv == 1, _dot_nt(qg * jnp.exp(hsl(g_all, h)), kc), a))

    yield
    b_t = _dot_nt_exact_rhs(sel8, b_all)
    qn_a = [jnp.sum(hd(QA, h) * old[("n", h)], axis=1, keepdims=True) for h in heads]
    o_b = []
    for h in heads:
        o_b.append(_dot(s_b[h] * cst["ret_dec"](h), hd(VB, h)) + qs_b[h] * cst["ret_inter"][:, h:h + 1])
    num_a, den_a, oi_c = [], [], []
    for h in heads:
        b_col = b_all[:, N_HEADS + h:N_HEADS + h + 1]
        b_row = b_t[N_HEADS + h:N_HEADS + h + 1, :]
        a_col = b_col + old[("m", h)][:, 0:1]
        dmat = jnp.where(causal, (b_col - b_row) + gp_t[h:h + 1, :], NEG_BIG)
        m_t = jnp.maximum(a_col, jnp.max(dmat, axis=1, keepdims=True))
        w_inter = jnp.exp(a_col - m_t)
        s = s_a[h] * jnp.exp(dmat - m_t)
        num_a.append(_dot(s, hd(VA, h)) + qc_a[h] * w_inter)
        den = jnp.sum(s, axis=1, keepdims=True) + qn_a[h] * w_inter
        den_a.append(jnp.maximum(jnp.abs(den), jnp.exp(-m_t)))
    for h in heads:
        b_col = b_all[:, N_HEADS + h:N_HEADS + h + 1]
        m_prev = old[("m", h)][:, 0:1]
        b_last = b_col[c - 1:c, :]
        g_col = (b_last - b_col) + gp[:, h:h + 1]
        m_new = jnp.maximum(b_last + m_prev, jnp.max(g_col, axis=0, keepdims=True))
        dec = jnp.exp(b_last + m_prev - m_new)
        kw = (hd(KA, h) * K_SCALE) * jnp.exp(g_col - m_new)
        new[("C", h)] = dec * old[("C", h)] + _dot_tn(kw, hd(VA, h))
        new[("n", h)] = dec * old[("n", h)] + jnp.sum(kw, axis=0, keepdims=True)
        new[("m", h)] = jnp.broadcast_to(m_new, (1, LANES))
    for h in heads:
        qg, kc, bc = hsl(qg_all, h), hsl(kc_all, h), hsl(xs[0:c], h)
        a, sec, n = a_c[h], 1, c // 2
        while n > 1:
            e = jnp.exp(hsl(xs[sec * c:(sec + 1) * c], h))
            a = jnp.where(lv == n, _dot_nt(qg * e, kc * e), a)
            sec, n = sec + 1, n // 2
        a_c[h] = a
        sh_t = old[("ShT", h)]
        oi_c.append(_dot_nt(qg * jnp.exp(bc), sh_t))
        bc_last = bc[c - 1:c, :]
        new[("ShT", h)] = jnp.exp(bc_last) * sh_t + _dot_tn(hd(IC, h), kc * jnp.exp(bc_last - bc))
    o_c = [_dot(a_c[h], hd(IC, h)) + oi_c[h] for h in heads]
    u = zc(CGD, BRANCH_W) * zc(XD, BRANCH_W)
    new[("buf", 0)] = u[c - 2:c, :]
    for (name, h), val in new.items():
        st.put(name, h, val)

    yield
    ya, yb, yc = [], [], []
    for h in heads:
        hh = num_a[h] / den_a[h]
        hn = hh * lax.rsqrt(jnp.mean(hh * hh, axis=1, keepdims=True) + LN_EPS)
        ya.append(jax.nn.sigmoid(hd(OA, h)) * (hn * hsl(prm["norm_a"], h)))
    put_y(0, jnp.concatenate(ya, axis=1))
    buf = old[("buf", 0)]
    row = lax.broadcasted_iota(jnp.int32, (c, BRANCH_W), 0)
    u1 = jnp.where(row == 0, buf[1:2, :], pltpu.roll(u, 1, axis=0))
    u2 = jnp.where(row == 0, buf[0:1, :], jnp.where(row == 1, buf[1:2, :], pltpu.roll(u, 2, axis=0)))
    cw = prm["conv_w"]
    yconv = (u2 * cw[0:1, :] + u1 * cw[1:2, :]) + u * cw[2:3, :]
    put_y(3 * BRANCH_W, zc(BGD, BRANCH_W) * yconv)
    for h in heads:
        oc = o_b[h] - jnp.mean(o_b[h], axis=1, keepdims=True)
        on = oc * lax.rsqrt(jnp.mean(oc * oc, axis=1, keepdims=True) + LN_EPS)
        yb.append(_silu(hd(GB, h)) * (on * hsl(prm["norm_b"], h)))
    put_y(BRANCH_W, jnp.concatenate(yb, axis=1))
    for h in heads:
        on = o_c[h] * lax.rsqrt(jnp.mean(o_c[h] * o_c[h], axis=1, keepdims=True) + LN_EPS)
        yc.append(_silu(hd(GC, h)) * (on * hsl(prm["norm_c"], h)))
    put_y(2 * BRANCH_W, jnp.concatenate(yc, axis=1))


def _chunk_consts(c):
    r = lax.broadcasted_iota(jnp.int32, (c, c), 0)
    l = lax.broadcasted_iota(jnp.int32, (c, c), 1)
    causal = r >= l
    r3 = lax.broadcasted_iota(jnp.int32, (c, 3 * c), 0)
    l3 = lax.broadcasted_iota(jnp.int32, (c, 3 * c), 1)
    l3 = jnp.where(l3 >= 2 * c, l3 - 2 * c, jnp.where(l3 >= c, l3 - c, l3))
    r8 = lax.broadcasted_iota(jnp.int32, (SUBLANES, LANES), 0)
    l8 = lax.broadcasted_iota(jnp.int32, (SUBLANES, LANES), 1)
    return {"causal": causal,
            "tril": jnp.where(r3 >= l3, 1.0, 0.0).astype(BF16),
            "sel8": jnp.where(r8 == l8, 1.0, 0.0).astype(BF16)}


def _layer_lb(lbc, layer):
    e = jnp.exp(lbc - jnp.max(lbc, axis=0, keepdims=True))
    sm = e / jnp.sum(e, axis=0, keepdims=True)
    cum = sm[0:1, :]
    for j in range(1, layer + 1):
        cum = cum + sm[j:j + 1, :]
    return cum - sm[0:1, :]


def _load_params(bias_ref, na_ref, nb_ref, nc_ref, lbc_ref, cw_ref, layer):
    return {"bias_if": bias_ref[...], "norm_a": na_ref[...], "norm_b": nb_ref[...], "norm_c": nc_ref[...],
            "lb": _layer_lb(lbc_ref[...], layer), "conv_w": cw_ref[...]}


PROMPT_LOCKSTEP = 1


def _mixer_prompt_kernel(layer, c, tt, *refs):
    ns = PROMPT_LOCKSTEP
    z_refs = refs[:ns]
    (cos_ref, sin_ref, rdec_ref, rinter_ref, rtoend_ref, rcdec_ref,
     bias_ref, na_ref, nb_ref, nc_ref, lbc_ref, cw_ref, gm_ref, lv_ref,
     y_ref, c_out, n_out, m_out, sr_out, sh_out, buf_out,
     c_s, n_s, m_s, sr_s, sht_s, buf_s) = refs[ns:]
    j = pl.program_id(1)

    @pl.when(j == 0)
    def _():
        c_s[...] = jnp.zeros_like(c_s)
        n_s[...] = jnp.zeros_like(n_s)
        m_s[...] = jnp.zeros_like(m_s)
        sr_s[...] = jnp.zeros_like(sr_s)
        sht_s[...] = jnp.zeros_like(sht_s)
        buf_s[...] = jnp.zeros_like(buf_s)

    prm = _load_params(bias_ref, na_ref, nb_ref, nc_ref, lbc_ref, cw_ref, layer)
    cst = _chunk_consts(c)
    cst["ret_dec"] = lambda h: rdec_ref[h]
    cst["ret_inter"] = rinter_ref[...]
    cst["ret_toend"] = rtoend_ref[...]
    cst["ret_cdec"] = rcdec_ref[...]
    cst["gla_m"] = gm_ref
    cst["gla_lv"] = lv_ref

    def state(s):
        def get(name, h):
            if name == "C":
                return c_s[s, h]
            if name == "n":
                return n_s[s, h:h + 1, :]
            if name == "m":
                return m_s[s, h:h + 1, :]
            if name == "Sr":
                return sr_s[s, h]
            if name == "ShT":
                return sht_s[s, h]
            return buf_s[s, 0:CONV_W - 1, :]

        def put(name, h, val):
            if name == "C":
                c_s[s, h] = val
            elif name == "n":
                n_s[s, h:h + 1, :] = val
            elif name == "m":
                m_s[s, h:h + 1, :] = val
            elif name == "Sr":
                sr_s[s, h] = val
            elif name == "ShT":
                sht_s[s, h] = val
            else:
                buf_s[s, 0:CONV_W - 1, :] = val

        return _State(get, put)

    def chunk(s, k):
        r = k * c
        cc = dict(cst)
        cc["cos2"] = cos_ref[r:r + c, :]
        cc["sin2"] = sin_ref[r:r + c, :]
        zc = lambda off, w: z_refs[s][r:r + c, off:off + w]

        def put_y(off, val):
            y_ref[s, r:r + c, off:off + BRANCH_W] = val.astype(BF16)

        return _mixer_chunk(zc, c, state(s), cc, prm, put_y)

    for s in range(ns):
        _pipelined(chunk(s, k) for k in range(tt // c))

    @pl.when(j == pl.num_programs(1) - 1)
    def _():
        c_out[...] = c_s[...]
        n_out[...] = n_s[:, 0:N_HEADS, :]
        m_out[...] = m_s[:, 0:N_HEADS, :]
        sr_out[...] = sr_s[...]
        for s in range(ns):
            for h in range(N_HEADS):
                sh_out[s, h] = sht_s[s, h].T
        buf_out[...] = buf_s[:, 0:CONV_W - 1, :]


def _const_spec(shape):
    nd = len(shape)
    return pl.BlockSpec(shape, lambda *_: (0,) * nd)


def _mixer_prompt(z, bsz, t, tt, layer, tabs, prms):
    c = math.gcd(t, CHUNK)
    nj = t // tt
    ns = PROMPT_LOCKSTEP
    assert bsz % ns == 0
    cos2, sin2, rdec, rinter, rtoend, rcdec = tabs
    state_shapes = [
        jax.ShapeDtypeStruct((bsz, N_HEADS, HEAD_DIM, HEAD_DIM), F32),
        jax.ShapeDtypeStruct((bsz, N_HEADS, HEAD_DIM), F32),
        jax.ShapeDtypeStruct((bsz, N_HEADS, LANES), F32),
        jax.ShapeDtypeStruct((bsz, N_HEADS, HEAD_DIM, HEAD_DIM), F32),
        jax.ShapeDtypeStruct((bsz, N_HEADS, HEAD_DIM, HEAD_DIM), F32),
        jax.ShapeDtypeStruct((bsz, CONV_W - 1, BRANCH_W), F32),
    ]
    big = pl.BlockSpec((ns, N_HEADS, HEAD_DIM, HEAD_DIM), lambda g, j: (g, 0, 0, 0))
    small = pl.BlockSpec((ns, N_HEADS, LANES), lambda g, j: (g, 0, 0))
    est = (2 * ns * tt * P_PAD * 4 + 2 * ns * tt * Y_W * 2 + 10 * ns * N_HEADS * HEAD_DIM * HEAD_DIM * 4 + (6 << 20))

    def z_spec(s):
        return pl.BlockSpec((tt, P_PAD), lambda g, j: ((g * ns + s) * nj + j, 0))

    outs = pl.pallas_call(
        functools.partial(_mixer_prompt_kernel, layer, c, tt),
        grid=(bsz // ns, nj),
        in_specs=[z_spec(s) for s in range(ns)]
                 + [pl.BlockSpec((tt, LANES), lambda g, j: (j, 0)),
                    pl.BlockSpec((tt, LANES), lambda g, j: (j, 0)),
                    _const_spec(rdec.shape), _const_spec(rinter.shape), _const_spec(rtoend.shape),
                    _const_spec(rcdec.shape)] + [_const_spec(p.shape) for p in prms],
        out_specs=[pl.BlockSpec((None, ns, tt, Y_W), lambda g, j: (g, 0, j, 0)),
                   big, small, small, big, big,
                   pl.BlockSpec((ns, CONV_W - 1, BRANCH_W), lambda g, j: (g, 0, 0))],
        out_shape=[jax.ShapeDtypeStruct((bsz // ns, ns, t, Y_W), BF16)] + state_shapes,
        scratch_shapes=[pltpu.VMEM((ns, N_HEADS, HEAD_DIM, HEAD_DIM), F32),
                        pltpu.VMEM((ns, SUBLANES, LANES), F32),
                        pltpu.VMEM((ns, SUBLANES, LANES), F32),
                        pltpu.VMEM((ns, N_HEADS, HEAD_DIM, HEAD_DIM), F32),
                        pltpu.VMEM((ns, N_HEADS, HEAD_DIM, HEAD_DIM), F32),
                        pltpu.VMEM((ns, SUBLANES, BRANCH_W), F32)],
        compiler_params=pltpu.CompilerParams(dimension_semantics=("arbitrary", "arbitrary"),
                                             vmem_limit_bytes=_vmem_limit(est)),
        name=f"mixer_prompt_l{layer}",
    )(*([z] * ns), cos2, sin2, rdec, rinter, rtoend, rcdec, *prms)
    return [outs[0].reshape(bsz * t, Y_W)] + list(outs[1:])


N_STATES = 6
N_MIXER_CONST_INPUTS = 15


def _mixer_sample_kernel(layer, c, bb, n_carried, *refs):
    (z_ref, cos_ref, sin_ref, rdec_ref, rinter_ref, rtoend_ref, rcdec_ref,
     bias_ref, na_ref, nb_ref, nc_ref, lbc_ref, cw_ref, gm_ref, lv_ref) = refs[:N_MIXER_CONST_INPUTS]
    c_in, n_in, m_in, sr_in, sh_in, buf_in = refs[N_MIXER_CONST_INPUTS:N_MIXER_CONST_INPUTS + N_STATES]
    y_ref, c_out, n_out, m_out, sr_out, sh_out, buf_out, ybuf = refs[N_MIXER_CONST_INPUTS + N_STATES + n_carried:]
    prm = _load_params(bias_ref, na_ref, nb_ref, nc_ref, lbc_ref, cw_ref, layer)
    cst = _chunk_consts(c)
    cst["ret_dec"] = lambda h: rdec_ref[h]
    cst["ret_inter"] = rinter_ref[...]
    cst["ret_toend"] = rtoend_ref[...]
    cst["ret_cdec"] = rcdec_ref[...]
    cst["gla_m"] = gm_ref
    cst["gla_lv"] = lv_ref
    cst["cos2"] = cos_ref[...]
    cst["sin2"] = sin_ref[...]

    def sequence(i):
        r = i * c
        zc = lambda off, w: z_ref[r:r + c, off:off + w]

        def get(name, h):
            if name == "C":
                return c_in[i, h]
            if name == "n":
                return n_in[i, pl.ds(h, 1), :]
            if name == "m":
                return m_in[i, pl.ds(h, 1), :]
            if name == "Sr":
                return sr_in[i, h]
            if name == "ShT":
                return sh_in[i, h].T
            return buf_in[i]

        def put(name, h, val):
            if name == "C":
                c_out[i, h] = val
            elif name == "n":
                n_out[i, pl.ds(h, 1), :] = val
            elif name == "m":
                m_out[i, pl.ds(h, 1), :] = val
            elif name == "Sr":
                sr_out[i, h] = val
            elif name == "ShT":
                sh_out[i, h] = val.T
            else:
                buf_out[i] = val

        def put_y(off, val):
            ybuf[r:r + c, off:off + BRANCH_W] = val

        return _mixer_chunk(zc, c, _State(get, put), cst, prm, put_y)

    _pipelined(sequence(i) for i in range(bb))
    y_ref[...] = ybuf[...].astype(BF16)


def _mixer_sample(z, row0, bsz, t, bb, layer, tabs, prms, states, carried):
    c = t
    rows = bb * t
    blk0 = row0 // rows
    cos2, sin2, rdec, rinter, rtoend, rcdec = tabs

    def lspec(*tail):
        zeros = (0,) * (len(tail) - 1)
        return pl.BlockSpec((None,) + tail, lambda i: (layer, i) + zeros)

    big = lspec(bb, N_HEADS, HEAD_DIM, HEAD_DIM)
    small = lspec(bb, N_HEADS, LANES)
    state_specs = [big, small, small, big, big, lspec(bb, CONV_W - 1, BRANCH_W)]
    state_shapes = [jax.ShapeDtypeStruct(s.shape, F32) for s in states]
    n_in = N_MIXER_CONST_INPUTS + N_STATES
    est = (2 * rows * P_PAD * 4 + 2 * rows * Y_W * 2 + rows * Y_W * 4
           + 2 * 2 * 3 * bb * N_HEADS * HEAD_DIM * HEAD_DIM * 4)
    return pl.pallas_call(
        functools.partial(_mixer_sample_kernel, layer, c, bb, len(carried)),
        grid=(bsz // bb,),
        in_specs=[pl.BlockSpec((rows, P_PAD), lambda i: (blk0 + i, 0)),
                  _const_spec(cos2.shape), _const_spec(sin2.shape),
                  _const_spec(rdec.shape), _const_spec(rinter.shape), _const_spec(rtoend.shape),
                  _const_spec(rcdec.shape)] + [_const_spec(p.shape) for p in prms] + state_specs
                 + [pl.BlockSpec(memory_space=pl.ANY)] * len(carried),
        out_specs=[pl.BlockSpec((rows, Y_W), lambda i: (i, 0))] + state_specs,
        out_shape=[jax.ShapeDtypeStruct((bsz * t, Y_W), BF16)] + state_shapes,
        input_output_aliases={n_in + k: 1 + k for k in range(len(carried))},
        scratch_shapes=[pltpu.VMEM((rows, Y_W), F32)],
        compiler_params=pltpu.CompilerParams(dimension_semantics=("arbitrary",),
                                             vmem_limit_bytes=_vmem_limit(est)),
        name=f"mixer_sample_l{layer}",
    )(z, cos2, sin2, rdec, rinter, rtoend, rcdec, *prms, *states, *carried)


def _layer_norm(r, g, b):
    mu = jnp.mean(r, axis=1, keepdims=True)
    rc = r - mu
    var = jnp.mean(rc * rc, axis=1, keepdims=True)
    return rc * lax.rsqrt(var + LN_EPS) * g + b


def _merge_kernel(npb, xp_ref, xs_ref, yp_ref, ys_ref, wb_ref, wg_ref, bg_ref, wo_ref, g_ref, b_ref,
                  hp_ref, hs_ref):
    x = _group_load(npb, xp_ref, xs_ref)
    xb = x.astype(BF16)
    y = _group_load(npb, yp_ref, ys_ref)
    acc = None
    for n in range(N_BRANCH):
        proj = jnp.dot(y[:, n * BRANCH_W:(n + 1) * BRANCH_W], wb_ref[n], preferred_element_type=F32)
        gate = jax.nn.sigmoid(jnp.dot(xb, wg_ref[:, n * D_MODEL:(n + 1) * D_MODEL], preferred_element_type=F32)
                              + bg_ref[:, n * D_MODEL:(n + 1) * D_MODEL])
        acc = gate * proj if acc is None else acc + gate * proj
    mix = jnp.dot(acc.astype(BF16), wo_ref[...], preferred_element_type=F32)
    _group_store(npb, hp_ref, hs_ref, _layer_norm(ALPHA * x + mix, g_ref[...], b_ref[...]))


def _layer_spec(layer, *tail):
    return pl.BlockSpec((None,) + tail, lambda *_: (layer,) + (0,) * len(tail))


def _group_out(xp, xs, tm, npb, nsb):
    return dict(out_specs=_group_specs(tm, D_MODEL, npb, nsb),
                out_shape=[jax.ShapeDtypeStruct(xp.shape, F32), jax.ShapeDtypeStruct(xs.shape, F32)])


def _merge(xp, xs, y_p, y_s, layer, wb, wg, bg, wo, g, b, tm):
    npb, nsb = xp.shape[0] // tm, xs.shape[0] // tm
    est = (2 * (N_BRANCH * BRANCH_W * D_MODEL + D_MODEL * N_BRANCH * D_MODEL + D_MODEL * D_MODEL) * 2
           + 2 * 2 * tm * D_MODEL * 4 * 2 + 2 * 2 * tm * Y_W * 2 + 6 * tm * D_MODEL * 4)
    return pl.pallas_call(
        functools.partial(_merge_kernel, npb),
        grid=(npb + nsb,),
        in_specs=_group_specs(tm, D_MODEL, npb, nsb) + _group_specs(tm, Y_W, npb, nsb)
                 + [_layer_spec(layer, N_BRANCH, BRANCH_W, D_MODEL), _layer_spec(layer, D_MODEL, N_BRANCH * D_MODEL),
                    _const_spec(bg.shape), _layer_spec(layer, D_MODEL, D_MODEL),
                    _const_spec(g.shape), _const_spec(b.shape)],
        compiler_params=pltpu.CompilerParams(dimension_semantics=("arbitrary",),
                                             vmem_limit_bytes=_vmem_limit(est)),
        name="merge",
        **_group_out(xp, xs, tm, npb, nsb),
    )(xp, xs, y_p, y_s, wb, wg, bg, wo, g, b)


def _ffn_kernel(npb, hp_ref, hs_ref, wu_ref, wd_ref, g_ref, b_ref, op_ref, os_ref):
    h = _group_load(npb, hp_ref, hs_ref)
    u = jnp.maximum(jnp.dot(h.astype(BF16), wu_ref[...], preferred_element_type=F32), 0.0)
    ff = jnp.dot((u * u).astype(BF16), wd_ref[...], preferred_element_type=F32)
    _group_store(npb, op_ref, os_ref, _layer_norm(ALPHA * h + ff, g_ref[...], b_ref[...]))


def _ffn(hp, hs, layer, wu, wd, g, b, tm):
    npb, nsb = hp.shape[0] // tm, hs.shape[0] // tm
    est = 2 * 2 * D_MODEL * D_FF * 2 + 2 * 2 * 2 * tm * D_MODEL * 4 + tm * D_FF * 6 + 2 * tm * D_MODEL * 4
    return pl.pallas_call(
        functools.partial(_ffn_kernel, npb),
        grid=(npb + nsb,),
        in_specs=_group_specs(tm, D_MODEL, npb, nsb)
                 + [_layer_spec(layer, D_MODEL, D_FF), _layer_spec(layer, D_FF, D_MODEL),
                    _const_spec(g.shape), _const_spec(b.shape)],
        compiler_params=pltpu.CompilerParams(dimension_semantics=("arbitrary",),
                                             vmem_limit_bytes=_vmem_limit(est)),
        name="ffn",
        **_group_out(hp, hs, tm, npb, nsb),
    )(hp, hs, wu, wd, g, b)


def _pad_lanes(a):
    return jnp.pad(a, ((0, 0), (0, LANES - a.shape[1])))


def _position_tables(pos, c):
    half = HEAD_DIM // 2
    inv = ROPE_BASE ** (-jnp.arange(half, dtype=F32) / half)
    ang = pos[:, None] * inv[None, :]
    cos, sin = jnp.cos(ang), jnp.sin(ang)
    cos2 = jnp.concatenate([cos, cos], axis=1)
    sin2 = jnp.concatenate([-sin, sin], axis=1)
    log_gamma = jnp.log(1.0 - 2.0 ** (-5.0 - jnp.arange(N_HEADS, dtype=F32)))
    idx = jnp.arange(c, dtype=F32)
    rel = idx[:, None] - idx[None, :]
    dec = jnp.where(rel >= 0, jnp.exp(log_gamma[:, None, None] * jnp.maximum(rel, 0.0)), 0.0)
    inter = jnp.swapaxes(jnp.exp(log_gamma[:, None] * (idx + 1.0)[None, :]), 0, 1)
    to_end = jnp.swapaxes(jnp.exp(log_gamma[:, None] * (c - 1.0 - idx)[None, :]), 0, 1)
    cdec = jnp.broadcast_to(jnp.exp(log_gamma * c)[:, None], (N_HEADS, LANES))
    cdec = jnp.pad(cdec, ((0, SUBLANES - N_HEADS), (0, 0)))
    return cos2, sin2, dec, _pad_lanes(inter), _pad_lanes(to_end), cdec


def _gla_tables(c):
    r = np.arange(c)[:, None]
    j = np.arange(c)[None, :]
    mats = [j <= r]
    n = c // 2
    while n > 1:
        start = (r // n) * n
        odd = (r // n) % 2 == 1
        mats.append(np.where(odd, (j >= start) & (j <= r), (j > r) & (j <= start + n - 1)))
        n //= 2
    x = np.maximum(r ^ j, 1)
    level = np.where(j > r, -1, np.where(j == r, 0, 1 << np.floor(np.log2(x)).astype(np.int64)))
    return jnp.asarray(np.tile(np.concatenate(mats, 0), (1, 3)), BF16), jnp.asarray(level, jnp.int32)


def _pack_w_in(w_in):
    a = 4 * BRANCH_W
    g = 2 * N_HEADS
    wb = w_in.astype(BF16)
    pad = jnp.zeros(wb.shape[:2] + (LANES - g,), BF16)
    return jnp.concatenate([wb[..., :a], wb[..., a + g:], wb[..., a:a + g], pad], axis=-1)


def kernel(x_prompt, x_sample, state_mlstm_C, state_mlstm_n, state_mlstm_m, state_ret, state_hgrn, state_conv,
           w_in, b_mlstm_gate, norm_a, norm_b, norm_c, lb_c, conv_w, w_branch, w_gate, b_gate, w_out,
           ln1_g, ln1_b, ln2_g, ln2_b, w_up, w_down):
    bp, tp, _ = x_prompt.shape
    bs, ts, _ = x_sample.shape
    n_p, n_s = bp * tp, bs * ts
    tm = 512
    tt = 256
    bb = 8
    assert n_p % tm == 0 and n_s % tm == 0 and tp % tt == 0 and bs % bb == 0 and n_p % (bb * ts) == 0

    xp = x_prompt.reshape(n_p, D_MODEL)
    xs = x_sample.reshape(n_s, D_MODEL)
    tabs_p = _position_tables(jnp.arange(tp, dtype=F32), math.gcd(tp, CHUNK))
    tabs_s = _position_tables(PAST_LEN + jnp.arange(ts, dtype=F32), math.gcd(ts, CHUNK))
    gla_p = _gla_tables(math.gcd(tp, CHUNK))
    gla_s = _gla_tables(math.gcd(ts, CHUNK))
    lbc = lb_c.astype(F32)
    row = lambda a: a.reshape(1, -1).astype(F32)
    w_in_p = _pack_w_in(w_in)
    wb, wg, wo, wu, wd = (w.astype(BF16) for w in (w_branch, w_gate, w_out, w_up, w_down))
    st_in = (state_mlstm_C.astype(F32), state_mlstm_n.astype(F32),
             jnp.broadcast_to(state_mlstm_m.astype(F32)[..., None], (DEPTH, bs, N_HEADS, LANES)),
             state_ret.astype(F32), state_hgrn.astype(F32), state_conv.astype(F32))

    p_states, s_states = [], ()
    for l in range(DEPTH):
        prms = (_pad_lanes(row(b_mlstm_gate[l])), row(norm_a[l]), row(norm_b[l]), row(norm_c[l]), lbc,
                conv_w[l].astype(F32))
        z = _inproj(xp, xs, l, w_in_p, 256)
        outs_p = _mixer_prompt(z, bp, tp, tt, l, tabs_p, prms + gla_p)
        outs_s = _mixer_sample(z, n_p, bs, ts, bb, l, tabs_s, prms + gla_s, st_in, s_states)
        hp, hs = _merge(xp, xs, outs_p[0], outs_s[0], l, wb, wg, row(b_gate[l]), wo, row(ln1_g[l]), row(ln1_b[l]), tm)
        xp, xs = _ffn(hp, hs, l, wu, wd, row(ln2_g[l]), row(ln2_b[l]), tm)
        p_states.append(outs_p[1:])
        s_states = tuple(outs_s[1:])

    cs, ns, ms, srs, shs, bufs = zip(*p_states)
    p_out = (jnp.stack(cs), jnp.stack(ns), jnp.stack([m[..., 0] for m in ms]),
             jnp.stack(srs), jnp.stack(shs), jnp.stack(bufs))
    s_out = s_states[:2] + (s_states[2][..., 0],) + s_states[3:]
    return (xp.reshape(bp, tp, D_MODEL), xs.reshape(bs, ts, D_MODEL)) + p_out + s_out
```

```python
import functools
import math

import jax
import jax.numpy as jnp
import numpy as np
from jax import lax
from jax.experimental import pallas as pl
from jax.experimental.pallas import tpu as pltpu

F32 = jnp.float32
BF16 = jnp.bfloat16

D_MODEL = 1024
DEPTH = 2
BRANCH_W = D_MODEL // 2
N_HEADS = 4
HEAD_DIM = BRANCH_W // N_HEADS
CONV_W = 3
N_BRANCH = 4
D_FF = 4 * D_MODEL
CHUNK = 64
ROPE_BASE = 10000.0
LN_EPS = 1e-5
NEG_BIG = -1e30
GATE_CLAMP = 1.0 - 1e-6
ALPHA = (2 * DEPTH) ** 0.25
PAST_LEN = 16384
K_SCALE = HEAD_DIM ** -0.5

LANES = 128
SUBLANES = 8
VMEM_BYTES = 64 * 1024 * 1024

(QA, KA, VA, OA, QB, KB, VB, GB, QC, FC, IC, GC, XD, BGD, CGD) = [i * BRANCH_W for i in range(15)]
GATE_OFF = 15 * BRANCH_W
P_PAD = GATE_OFF + LANES
Y_W = N_BRANCH * BRANCH_W


def _vmem_limit(est_bytes):
    return int(min(est_bytes + est_bytes // 4 + (4 << 20), VMEM_BYTES - (6 << 20)))


def _dot(a, b):
    return jnp.dot(a.astype(BF16), b.astype(BF16), preferred_element_type=F32)


def _dot_nt(a, b):
    return lax.dot_general(a.astype(BF16), b.astype(BF16), (((1,), (1,)), ((), ())),
                           preferred_element_type=F32)


def _dot_tn(a, b):
    return lax.dot_general(a.astype(BF16), b.astype(BF16), (((0,), (0,)), ((), ())),
                           preferred_element_type=F32)


def _split3(x):
    hi = x.astype(BF16)
    r = x - hi.astype(F32)
    mid = r.astype(BF16)
    lo = (r - mid.astype(F32)).astype(BF16)
    return hi, mid, lo


def _dot_exact_rhs(a01x3, x):
    return jnp.dot(a01x3, _stack3(x), preferred_element_type=F32)


def _stack3(x):
    parts = _split3(x)
    if x.shape[0] % (2 * SUBLANES) == 0:
        return jnp.concatenate(parts, axis=0)
    return jnp.concatenate([p.astype(F32) for p in parts], axis=0).astype(BF16)


def _dot_nt_exact_rhs(a01, x):
    hi, mid, lo = _split3(x)
    d = lambda p: lax.dot_general(a01, p, (((1,), (1,)), ((), ())), preferred_element_type=F32)
    return (d(hi) + d(mid)) + d(lo)


def _silu(x):
    return x * jax.nn.sigmoid(x)


def _group_specs(tm, width, npb, nsb):
    return [pl.BlockSpec((tm, width), lambda i: (jnp.minimum(i, npb - 1), 0)),
            pl.BlockSpec((tm, width), lambda i: (jnp.clip(i - npb, 0, nsb - 1), 0))]


def _group_load(npb, p_ref, s_ref):
    return jnp.where(pl.program_id(0) < npb, p_ref[...], s_ref[...])


def _group_store(npb, p_ref, s_ref, val):
    is_p = pl.program_id(0) < npb

    @pl.when(is_p)
    def _():
        p_ref[...] = val

    @pl.when(jnp.logical_not(is_p))
    def _():
        s_ref[...] = val


def _inproj_kernel(npb, xp_ref, xs_ref, w_ref, z_ref):
    xb = _group_load(npb, xp_ref, xs_ref).astype(BF16)
    for off in range(0, P_PAD, BRANCH_W):
        wd = min(BRANCH_W, P_PAD - off)
        z_ref[:, off:off + wd] = jnp.dot(xb, w_ref[:, off:off + wd], preferred_element_type=F32)


def _inproj(xp, xs, layer, w, tm):
    npb, nsb = xp.shape[0] // tm, xs.shape[0] // tm
    est = D_MODEL * P_PAD * 2 * 2 + 2 * tm * P_PAD * 4 + 2 * 2 * tm * D_MODEL * 4
    return pl.pallas_call(
        functools.partial(_inproj_kernel, npb),
        grid=(npb + nsb,),
        in_specs=_group_specs(tm, D_MODEL, npb, nsb) + [_layer_spec(layer, D_MODEL, P_PAD)],
        out_specs=pl.BlockSpec((tm, P_PAD), lambda i: (i, 0)),
        out_shape=jax.ShapeDtypeStruct(((npb + nsb) * tm, P_PAD), F32),
        compiler_params=pltpu.CompilerParams(dimension_semantics=("arbitrary",),
                                             vmem_limit_bytes=_vmem_limit(est)),
        name="inproj",
    )(xp, xs, w)


class _State:
    def __init__(self, get, put):
        self.get = get
        self.put = put


LEVEL1_DONE, STATE_STORED = "level 1 done", "state stored"


def _advance(g, until=None):
    for label in g:
        if until is not None and label == until:
            return


def _pipelined(chunk_generators):
    pending_tail = None
    for g in chunk_generators:
        _advance(g, LEVEL1_DONE)
        if pending_tail is not None:
            _advance(pending_tail)
        _advance(g, STATE_STORED)
        pending_tail = g
    _advance(pending_tail)


def _mixer_chunk(zc, c, st, cst, prm, put_y):
    tril, causal, sel8 = cst["tril"], cst["causal"], cst["sel8"]
    old = {(name, h): st.get(name, h) for name in ("C", "n", "m", "Sr", "ShT") for h in range(N_HEADS)}
    old[("buf", 0)] = st.get("buf", 0)
    new = {}

    heads = range(N_HEADS)
    hd = lambda off, h: zc(off + h * HEAD_DIM, HEAD_DIM)
    hsl = lambda a, h: a[:, h * HEAD_DIM:(h + 1) * HEAD_DIM]
    cos2, sin2 = cst["cos2"], cst["sin2"]
    rope = lambda x: x * cos2 + pltpu.roll(x, HEAD_DIM // 2, axis=1) * sin2
    lv = cst["gla_lv"][...]

    gp = zc(GATE_OFF, LANES) + prm["bias_if"]
    lf = jnp.minimum(gp, 0.0) - jnp.log1p(jnp.exp(-jnp.abs(gp)))
    b_all = _dot_exact_rhs(tril, lf)
    gp_t = _dot_nt_exact_rhs(sel8, gp)
    s_a, qc_a, s_b, qs_b, a_c = [], [], [], [], []
    for h in heads:
        qa = hd(QA, h)
        s_a.append(_dot_nt(qa, hd(KA, h) * K_SCALE))
        qc_a.append(_dot(qa, old[("C", h)]))
    for h in heads:
        qb = rope(hd(QB, h))
        kb = rope(hd(KB, h)) * K_SCALE
        s_b.append(_dot_nt(qb, kb))
        qs_b.append(_dot(qb, old[("Sr", h)]))
        new[("Sr", h)] = (cst["ret_cdec"][h:h + 1, :] * old[("Sr", h)]
                          + _dot_tn(kb * cst["ret_toend"][:, h:h + 1], hd(VB, h)))
    kc_all = (1.0 - prm["lb"]) * jax.nn.sigmoid(-zc(FC, BRANCH_W))
    g_all = jnp.log1p(-jnp.minimum(kc_all, GATE_CLAMP))
    xs = _dot_exact_rhs(cst["gla_m"][...], g_all)
    bc_all = xs[0:c]
    qg_all = _silu(zc(QC, BRANCH_W))
    for h in heads:
        qg, kc = hsl(qg_all, h), hsl(kc_all, h)
        a = jnp.where(lv == 0, _dot_nt(qg, kc), 0.0)
        a_c.append(jnp.where(lv == 1, _dot_nt(qg * jnp.exp(hsl(g_all, h)), kc), a))

    yield LEVEL1_DONE
    b_t = _dot_nt_exact_rhs(sel8, b_all)
    qn_a = [jnp.sum(hd(QA, h) * old[("n", h)], axis=1, keepdims=True) for h in heads]
    o_b = []
    for h in heads:
        o_b.append(_dot(s_b[h] * cst["ret_dec"](h), hd(VB, h)) + qs_b[h] * cst["ret_inter"][:, h:h + 1])
    num_a, den_a, oi_c = [], [], []
    for h in heads:
        b_col = b_all[:, N_HEADS + h:N_HEADS + h + 1]
        b_row = b_t[N_HEADS + h:N_HEADS + h + 1, :]
        a_col = b_col + old[("m", h)][:, 0:1]
        dmat = jnp.where(causal, (b_col - b_row) + gp_t[h:h + 1, :], NEG_BIG)
        m_t = jnp.maximum(a_col, jnp.max(dmat, axis=1, keepdims=True))
        w_inter = jnp.exp(a_col - m_t)
        s = s_a[h] * jnp.exp(dmat - m_t)
        num_a.append(_dot(s, hd(VA, h)) + qc_a[h] * w_inter)
        den = jnp.sum(s, axis=1, keepdims=True) + qn_a[h] * w_inter
        den_a.append(jnp.maximum(jnp.abs(den), jnp.exp(-m_t)))
    for h in heads:
        b_col = b_all[:, N_HEADS + h:N_HEADS + h + 1]
        m_prev = old[("m", h)][:, 0:1]
        b_last = b_col[c - 1:c, :]
        g_col = (b_last - b_col) + gp[:, h:h + 1]
        m_new = jnp.maximum(b_last + m_prev, jnp.max(g_col, axis=0, keepdims=True))
        dec = jnp.exp(b_last + m_prev - m_new)
        kw = (hd(KA, h) * K_SCALE) * jnp.exp(g_col - m_new)
        new[("C", h)] = dec * old[("C", h)] + _dot_tn(kw, hd(VA, h))
        new[("n", h)] = dec * old[("n", h)] + jnp.sum(kw, axis=0, keepdims=True)
        new[("m", h)] = jnp.broadcast_to(m_new, (1, LANES))
    for h in heads:
        qg, kc, bc = hsl(qg_all, h), hsl(kc_all, h), hsl(bc_all, h)
        a, sec, n = a_c[h], 1, c // 2
        while n > 1:
            e = jnp.exp(hsl(xs[sec * c:(sec + 1) * c], h))
            a = jnp.where(lv == n, _dot_nt(qg * e, kc * e), a)
            sec, n = sec + 1, n // 2
        a_c[h] = a
        sh_t = old[("ShT", h)]
        oi_c.append(_dot_nt(qg * jnp.exp(bc), sh_t))
        bc_last = bc[c - 1:c, :]
        new[("ShT", h)] = jnp.exp(bc_last) * sh_t + _dot_tn(hd(IC, h), kc * jnp.exp(bc_last - bc))
    o_c = [_dot(a_c[h], hd(IC, h)) + oi_c[h] for h in heads]
    u = zc(CGD, BRANCH_W) * zc(XD, BRANCH_W)
    new[("buf", 0)] = u[c - 2:c, :]
    for (name, h), val in new.items():
        st.put(name, h, val)

    yield STATE_STORED
    ya, yb, yc = [], [], []
    for h in heads:
        hh = num_a[h] / den_a[h]
        hn = hh * lax.rsqrt(jnp.mean(hh * hh, axis=1, keepdims=True) + LN_EPS)
        ya.append(jax.nn.sigmoid(hd(OA, h)) * (hn * hsl(prm["norm_a"], h)))
    put_y(0, jnp.concatenate(ya, axis=1))
    buf = old[("buf", 0)]
    row = lax.broadcasted_iota(jnp.int32, (c, BRANCH_W), 0)
    u1 = jnp.where(row == 0, buf[1:2, :], pltpu.roll(u, 1, axis=0))
    u2 = jnp.where(row == 0, buf[0:1, :], jnp.where(row == 1, buf[1:2, :], pltpu.roll(u, 2, axis=0)))
    cw = prm["conv_w"]
    yconv = (u2 * cw[0:1, :] + u1 * cw[1:2, :]) + u * cw[2:3, :]
    put_y(3 * BRANCH_W, zc(BGD, BRANCH_W) * yconv)
    for h in heads:
        oc = o_b[h] - jnp.mean(o_b[h], axis=1, keepdims=True)
        on = oc * lax.rsqrt(jnp.mean(oc * oc, axis=1, keepdims=True) + LN_EPS)
        yb.append(_silu(hd(GB, h)) * (on * hsl(prm["norm_b"], h)))
    put_y(BRANCH_W, jnp.concatenate(yb, axis=1))
    for h in heads:
        on = o_c[h] * lax.rsqrt(jnp.mean(o_c[h] * o_c[h], axis=1, keepdims=True) + LN_EPS)
        yc.append(_silu(hd(GC, h)) * (on * hsl(prm["norm_c"], h)))
    put_y(2 * BRANCH_W, jnp.concatenate(yc, axis=1))


def _chunk_consts(c):
    r = lax.broadcasted_iota(jnp.int32, (c, c), 0)
    l = lax.broadcasted_iota(jnp.int32, (c, c), 1)
    causal = r >= l
    r3 = lax.broadcasted_iota(jnp.int32, (c, 3 * c), 0)
    l3 = lax.broadcasted_iota(jnp.int32, (c, 3 * c), 1)
    l3 = jnp.where(l3 >= 2 * c, l3 - 2 * c, jnp.where(l3 >= c, l3 - c, l3))
    r8 = lax.broadcasted_iota(jnp.int32, (SUBLANES, LANES), 0)
    l8 = lax.broadcasted_iota(jnp.int32, (SUBLANES, LANES), 1)
    return {"causal": causal,
            "tril": jnp.where(r3 >= l3, 1.0, 0.0).astype(BF16),
            "sel8": jnp.where(r8 == l8, 1.0, 0.0).astype(BF16)}


def _layer_lb(lbc, layer):
    e = jnp.exp(lbc - jnp.max(lbc, axis=0, keepdims=True))
    sm = e / jnp.sum(e, axis=0, keepdims=True)
    cum = sm[0:1, :]
    for j in range(1, layer + 1):
        cum = cum + sm[j:j + 1, :]
    return cum - sm[0:1, :]


def _load_params(bias_ref, na_ref, nb_ref, nc_ref, lbc_ref, cw_ref, layer):
    return {"bias_if": bias_ref[...], "norm_a": na_ref[...], "norm_b": nb_ref[...], "norm_c": nc_ref[...],
            "lb": _layer_lb(lbc_ref[...], layer), "conv_w": cw_ref[...]}


PROMPT_LOCKSTEP = 1


def _mixer_prompt_kernel(layer, c, tt, *refs):
    ns = PROMPT_LOCKSTEP
    z_refs = refs[:ns]
    (cos_ref, sin_ref, rdec_ref, rinter_ref, rtoend_ref, rcdec_ref,
     bias_ref, na_ref, nb_ref, nc_ref, lbc_ref, cw_ref, gm_ref, lv_ref,
     y_ref, c_out, n_out, m_out, sr_out, sh_out, buf_out,
     c_s, n_s, m_s, sr_s, sht_s, buf_s) = refs[ns:]
    j = pl.program_id(1)

    @pl.when(j == 0)
    def _():
        c_s[...] = jnp.zeros_like(c_s)
        n_s[...] = jnp.zeros_like(n_s)
        m_s[...] = jnp.zeros_like(m_s)
        sr_s[...] = jnp.zeros_like(sr_s)
        sht_s[...] = jnp.zeros_like(sht_s)
        buf_s[...] = jnp.zeros_like(buf_s)

    prm = _load_params(bias_ref, na_ref, nb_ref, nc_ref, lbc_ref, cw_ref, layer)
    cst = _chunk_consts(c)
    cst["ret_dec"] = lambda h: rdec_ref[h]
    cst["ret_inter"] = rinter_ref[...]
    cst["ret_toend"] = rtoend_ref[...]
    cst["ret_cdec"] = rcdec_ref[...]
    cst["gla_m"] = gm_ref
    cst["gla_lv"] = lv_ref

    def state(s):
        def get(name, h):
            if name == "C":
                return c_s[s, h]
            if name == "n":
                return n_s[s, h:h + 1, :]
            if name == "m":
                return m_s[s, h:h + 1, :]
            if name == "Sr":
                return sr_s[s, h]
            if name == "ShT":
                return sht_s[s, h]
            return buf_s[s, 0:CONV_W - 1, :]

        def put(name, h, val):
            if name == "C":
                c_s[s, h] = val
            elif name == "n":
                n_s[s, h:h + 1, :] = val
            elif name == "m":
                m_s[s, h:h + 1, :] = val
            elif name == "Sr":
                sr_s[s, h] = val
            elif name == "ShT":
                sht_s[s, h] = val
            else:
                buf_s[s, 0:CONV_W - 1, :] = val

        return _State(get, put)

    def chunk(s, k):
        r = k * c
        cc = dict(cst)
        cc["cos2"] = cos_ref[r:r + c, :]
        cc["sin2"] = sin_ref[r:r + c, :]
        zc = lambda off, w: z_refs[s][r:r + c, off:off + w]

        def put_y(off, val):
            y_ref[s, r:r + c, off:off + BRANCH_W] = val.astype(BF16)

        return _mixer_chunk(zc, c, state(s), cc, prm, put_y)

    for s in range(ns):
        _pipelined(chunk(s, k) for k in range(tt // c))

    @pl.when(j == pl.num_programs(1) - 1)
    def _():
        c_out[...] = c_s[...]
        n_out[...] = n_s[:, 0:N_HEADS, :]
        m_out[...] = m_s[:, 0:N_HEADS, :]
        sr_out[...] = sr_s[...]
        for s in range(ns):
            for h in range(N_HEADS):
                sh_out[s, h] = sht_s[s, h].T
        buf_out[...] = buf_s[:, 0:CONV_W - 1, :]


def _const_spec(shape):
    nd = len(shape)
    return pl.BlockSpec(shape, lambda *_: (0,) * nd)


def _mixer_prompt(z, bsz, t, tt, layer, tabs, prms):
    c = math.gcd(t, CHUNK)
    nj = t // tt
    ns = PROMPT_LOCKSTEP
    assert bsz % ns == 0
    cos2, sin2, rdec, rinter, rtoend, rcdec = tabs
    state_shapes = [
        jax.ShapeDtypeStruct((bsz, N_HEADS, HEAD_DIM, HEAD_DIM), F32),
        jax.ShapeDtypeStruct((bsz, N_HEADS, HEAD_DIM), F32),
        jax.ShapeDtypeStruct((bsz, N_HEADS, LANES), F32),
        jax.ShapeDtypeStruct((bsz, N_HEADS, HEAD_DIM, HEAD_DIM), F32),
        jax.ShapeDtypeStruct((bsz, N_HEADS, HEAD_DIM, HEAD_DIM), F32),
        jax.ShapeDtypeStruct((bsz, CONV_W - 1, BRANCH_W), F32),
    ]
    big = pl.BlockSpec((ns, N_HEADS, HEAD_DIM, HEAD_DIM), lambda g, j: (g, 0, 0, 0))
    small = pl.BlockSpec((ns, N_HEADS, LANES), lambda g, j: (g, 0, 0))
    est = (2 * ns * tt * P_PAD * 4 + 2 * ns * tt * Y_W * 2 + 10 * ns * N_HEADS * HEAD_DIM * HEAD_DIM * 4 + (6 << 20))

    def z_spec(s, width):
        return pl.BlockSpec((tt, width), lambda g, j: ((g * ns + s) * nj + j, 0))

    outs = pl.pallas_call(
        functools.partial(_mixer_prompt_kernel, layer, c, tt),
        grid=(bsz // ns, nj),
        in_specs=[z_spec(s, P_PAD) for s in range(ns)]
                 + [pl.BlockSpec((tt, LANES), lambda g, j: (j, 0)),
                    pl.BlockSpec((tt, LANES), lambda g, j: (j, 0)),
                    _const_spec(rdec.shape), _const_spec(rinter.shape), _const_spec(rtoend.shape),
                    _const_spec(rcdec.shape)] + [_const_spec(p.shape) for p in prms],
        out_specs=[pl.BlockSpec((None, ns, tt, Y_W), lambda g, j: (g, 0, j, 0)),
                   big, small, small, big, big,
                   pl.BlockSpec((ns, CONV_W - 1, BRANCH_W), lambda g, j: (g, 0, 0))],
        out_shape=[jax.ShapeDtypeStruct((bsz // ns, ns, t, Y_W), BF16)] + state_shapes,
        scratch_shapes=[pltpu.VMEM((ns, N_HEADS, HEAD_DIM, HEAD_DIM), F32),
                        pltpu.VMEM((ns, SUBLANES, LANES), F32),
                        pltpu.VMEM((ns, SUBLANES, LANES), F32),
                        pltpu.VMEM((ns, N_HEADS, HEAD_DIM, HEAD_DIM), F32),
                        pltpu.VMEM((ns, N_HEADS, HEAD_DIM, HEAD_DIM), F32),
                        pltpu.VMEM((ns, SUBLANES, BRANCH_W), F32)],
        compiler_params=pltpu.CompilerParams(dimension_semantics=("arbitrary", "arbitrary"),
                                             vmem_limit_bytes=_vmem_limit(est)),
        name=f"mixer_prompt_l{layer}",
    )(*([z] * ns), cos2, sin2, rdec, rinter, rtoend, rcdec, *prms)
    return [outs[0].reshape(bsz * t, Y_W)] + list(outs[1:])


N_STATES = 6
N_MIXER_CONST_INPUTS = 15


def _mixer_sample_kernel(layer, c, bb, n_carried, *refs):
    (z_ref, cos_ref, sin_ref, rdec_ref, rinter_ref, rtoend_ref, rcdec_ref,
     bias_ref, na_ref, nb_ref, nc_ref, lbc_ref, cw_ref, gm_ref, lv_ref) = refs[:N_MIXER_CONST_INPUTS]
    c_in, n_in, m_in, sr_in, sh_in, buf_in = refs[N_MIXER_CONST_INPUTS:N_MIXER_CONST_INPUTS + N_STATES]
    y_ref, c_out, n_out, m_out, sr_out, sh_out, buf_out, ybuf = refs[N_MIXER_CONST_INPUTS + N_STATES + n_carried:]
    prm = _load_params(bias_ref, na_ref, nb_ref, nc_ref, lbc_ref, cw_ref, layer)
    cst = _chunk_consts(c)
    cst["ret_dec"] = lambda h: rdec_ref[h]
    cst["ret_inter"] = rinter_ref[...]
    cst["ret_toend"] = rtoend_ref[...]
    cst["ret_cdec"] = rcdec_ref[...]
    cst["gla_m"] = gm_ref
    cst["gla_lv"] = lv_ref
    cst["cos2"] = cos_ref[...]
    cst["sin2"] = sin_ref[...]

    def sequence(i):
        r = i * c
        zc = lambda off, w: z_ref[r:r + c, off:off + w]

        def get(name, h):
            if name == "C":
                return c_in[i, h]
            if name == "n":
                return n_in[i, pl.ds(h, 1), :]
            if name == "m":
                return m_in[i, pl.ds(h, 1), :]
            if name == "Sr":
                return sr_in[i, h]
            if name == "ShT":
                return sh_in[i, h].T
            return buf_in[i]

        def put(name, h, val):
            if name == "C":
                c_out[i, h] = val
            elif name == "n":
                n_out[i, pl.ds(h, 1), :] = val
            elif name == "m":
                m_out[i, pl.ds(h, 1), :] = val
            elif name == "Sr":
                sr_out[i, h] = val
            elif name == "ShT":
                sh_out[i, h] = val.T
            else:
                buf_out[i] = val

        def put_y(off, val):
            ybuf[r:r + c, off:off + BRANCH_W] = val

        return _mixer_chunk(zc, c, _State(get, put), cst, prm, put_y)

    _pipelined(sequence(i) for i in range(bb))
    y_ref[...] = ybuf[...].astype(BF16)


def _mixer_sample(z, row0, bsz, t, bb, layer, tabs, prms, states, carried):
    c = t
    rows = bb * t
    blk0 = row0 // rows
    cos2, sin2, rdec, rinter, rtoend, rcdec = tabs

    def lspec(*tail):
        zeros = (0,) * (len(tail) - 1)
        return pl.BlockSpec((None,) + tail, lambda i: (layer, i) + zeros)

    big = lspec(bb, N_HEADS, HEAD_DIM, HEAD_DIM)
    small = lspec(bb, N_HEADS, LANES)
    state_specs = [big, small, small, big, big, lspec(bb, CONV_W - 1, BRANCH_W)]
    state_shapes = [jax.ShapeDtypeStruct(s.shape, F32) for s in states]
    n_in = N_MIXER_CONST_INPUTS + N_STATES
    est = (2 * rows * P_PAD * 4 + 2 * rows * Y_W * 2 + rows * Y_W * 4
           + 2 * 2 * 3 * bb * N_HEADS * HEAD_DIM * HEAD_DIM * 4)
    return pl.pallas_call(
        functools.partial(_mixer_sample_kernel, layer, c, bb, len(carried)),
        grid=(bsz // bb,),
        in_specs=[pl.BlockSpec((rows, P_PAD), lambda i: (blk0 + i, 0)),
                  _const_spec(cos2.shape), _const_spec(sin2.shape),
                  _const_spec(rdec.shape), _const_spec(rinter.shape), _const_spec(rtoend.shape),
                  _const_spec(rcdec.shape)] + [_const_spec(p.shape) for p in prms] + state_specs
                 + [pl.BlockSpec(memory_space=pl.ANY)] * len(carried),
        out_specs=[pl.BlockSpec((rows, Y_W), lambda i: (i, 0))] + state_specs,
        out_shape=[jax.ShapeDtypeStruct((bsz * t, Y_W), BF16)] + state_shapes,
        input_output_aliases={n_in + k: 1 + k for k in range(len(carried))},
        scratch_shapes=[pltpu.VMEM((rows, Y_W), F32)],
        compiler_params=pltpu.CompilerParams(dimension_semantics=("arbitrary",),
                                             vmem_limit_bytes=_vmem_limit(est)),
        name=f"mixer_sample_l{layer}",
    )(z, cos2, sin2, rdec, rinter, rtoend, rcdec, *prms, *states, *carried)


def _layer_norm(r, g, b):
    mu = jnp.mean(r, axis=1, keepdims=True)
    rc = r - mu
    var = jnp.mean(rc * rc, axis=1, keepdims=True)
    return rc * lax.rsqrt(var + LN_EPS) * g + b


def _merge_kernel(npb, xp_ref, xs_ref, yp_ref, ys_ref, wb_ref, wg_ref, bg_ref, wo_ref, g_ref, b_ref,
                  hp_ref, hs_ref):
    x = _group_load(npb, xp_ref, xs_ref)
    xb = x.astype(BF16)
    y = _group_load(npb, yp_ref, ys_ref)
    acc = None
    for n in range(N_BRANCH):
        proj = jnp.dot(y[:, n * BRANCH_W:(n + 1) * BRANCH_W], wb_ref[n], preferred_element_type=F32)
        gate = jax.nn.sigmoid(jnp.dot(xb, wg_ref[:, n * D_MODEL:(n + 1) * D_MODEL], preferred_element_type=F32)
                              + bg_ref[:, n * D_MODEL:(n + 1) * D_MODEL])
        acc = gate * proj if acc is None else acc + gate * proj
    mix = jnp.dot(acc.astype(BF16), wo_ref[...], preferred_element_type=F32)
    _group_store(npb, hp_ref, hs_ref, _layer_norm(ALPHA * x + mix, g_ref[...], b_ref[...]))


def _layer_spec(layer, *tail):
    return pl.BlockSpec((None,) + tail, lambda *_: (layer,) + (0,) * len(tail))


def _group_out(xp, xs, tm, npb, nsb):
    return dict(out_specs=_group_specs(tm, D_MODEL, npb, nsb),
                out_shape=[jax.ShapeDtypeStruct(xp.shape, F32), jax.ShapeDtypeStruct(xs.shape, F32)])


def _merge(xp, xs, y_p, y_s, layer, wb, wg, bg, wo, g, b, tm):
    npb, nsb = xp.shape[0] // tm, xs.shape[0] // tm
    est = (2 * (N_BRANCH * BRANCH_W * D_MODEL + D_MODEL * N_BRANCH * D_MODEL + D_MODEL * D_MODEL) * 2
           + 2 * 2 * tm * D_MODEL * 4 * 2 + 2 * 2 * tm * Y_W * 2 + 6 * tm * D_MODEL * 4)
    return pl.pallas_call(
        functools.partial(_merge_kernel, npb),
        grid=(npb + nsb,),
        in_specs=_group_specs(tm, D_MODEL, npb, nsb) + _group_specs(tm, Y_W, npb, nsb)
                 + [_layer_spec(layer, N_BRANCH, BRANCH_W, D_MODEL), _layer_spec(layer, D_MODEL, N_BRANCH * D_MODEL),
                    _const_spec(bg.shape), _layer_spec(layer, D_MODEL, D_MODEL),
                    _const_spec(g.shape), _const_spec(b.shape)],
        compiler_params=pltpu.CompilerParams(dimension_semantics=("arbitrary",),
                                             vmem_limit_bytes=_vmem_limit(est)),
        name="merge",
        **_group_out(xp, xs, tm, npb, nsb),
    )(xp, xs, y_p, y_s, wb, wg, bg, wo, g, b)


def _ffn_kernel(npb, hp_ref, hs_ref, wu_ref, wd_ref, g_ref, b_ref, op_ref, os_ref):
    h = _group_load(npb, hp_ref, hs_ref)
    u = jnp.maximum(jnp.dot(h.astype(BF16), wu_ref[...], preferred_element_type=F32), 0.0)
    ff = jnp.dot((u * u).astype(BF16), wd_ref[...], preferred_element_type=F32)
    _group_store(npb, op_ref, os_ref, _layer_norm(ALPHA * h + ff, g_ref[...], b_ref[...]))


def _ffn(hp, hs, layer, wu, wd, g, b, tm):
    npb, nsb = hp.shape[0] // tm, hs.shape[0] // tm
    est = 2 * 2 * D_MODEL * D_FF * 2 + 2 * 2 * 2 * tm * D_MODEL * 4 + tm * D_FF * 6 + 2 * tm * D_MODEL * 4
    return pl.pallas_call(
        functools.partial(_ffn_kernel, npb),
        grid=(npb + nsb,),
        in_specs=_group_specs(tm, D_MODEL, npb, nsb)
                 + [_layer_spec(layer, D_MODEL, D_FF), _layer_spec(layer, D_FF, D_MODEL),
                    _const_spec(g.shape), _const_spec(b.shape)],
        compiler_params=pltpu.CompilerParams(dimension_semantics=("arbitrary",),
                                             vmem_limit_bytes=_vmem_limit(est)),
        name="ffn",
        **_group_out(hp, hs, tm, npb, nsb),
    )(hp, hs, wu, wd, g, b)


def _pad_lanes(a):
    return jnp.pad(a, ((0, 0), (0, LANES - a.shape[1])))


def _position_tables(pos, c):
    half = HEAD_DIM // 2
    inv = ROPE_BASE ** (-jnp.arange(half, dtype=F32) / half)
    ang = pos[:, None] * inv[None, :]
    cos, sin = jnp.cos(ang), jnp.sin(ang)
    cos2 = jnp.concatenate([cos, cos], axis=1)
    sin2 = jnp.concatenate([-sin, sin], axis=1)
    log_gamma = jnp.log(1.0 - 2.0 ** (-5.0 - jnp.arange(N_HEADS, dtype=F32)))
    idx = jnp.arange(c, dtype=F32)
    rel = idx[:, None] - idx[None, :]
    dec = jnp.where(rel >= 0, jnp.exp(log_gamma[:, None, None] * jnp.maximum(rel, 0.0)), 0.0)
    inter = jnp.swapaxes(jnp.exp(log_gamma[:, None] * (idx + 1.0)[None, :]), 0, 1)
    to_end = jnp.swapaxes(jnp.exp(log_gamma[:, None] * (c - 1.0 - idx)[None, :]), 0, 1)
    cdec = jnp.broadcast_to(jnp.exp(log_gamma * c)[:, None], (N_HEADS, LANES))
    cdec = jnp.pad(cdec, ((0, SUBLANES - N_HEADS), (0, 0)))
    return cos2, sin2, dec, _pad_lanes(inter), _pad_lanes(to_end), cdec


def _gla_tables(c):
    r = np.arange(c)[:, None]
    j = np.arange(c)[None, :]
    mats = [j <= r]
    n = c // 2
    while n > 1:
        start = (r // n) * n
        odd = (r // n) % 2 == 1
        mats.append(np.where(odd, (j >= start) & (j <= r), (j > r) & (j <= start + n - 1)))
        n //= 2
    x = np.maximum(r ^ j, 1)
    level = np.where(j > r, -1, np.where(j == r, 0, 1 << np.floor(np.log2(x)).astype(np.int64)))
    return jnp.asarray(np.tile(np.concatenate(mats, 0), (1, 3)), BF16), jnp.asarray(level, jnp.int32)


PACK_ROWS = 256


def _pack_w_in_kernel(w_ref, o_ref):
    a = 4 * BRANCH_W
    g = 2 * N_HEADS
    w = w_ref[...]
    o_ref[:, 0:a] = w[:, 0:a].astype(BF16)
    o_ref[:, a:GATE_OFF] = w[:, a + g:].astype(BF16)
    gates = jnp.concatenate([w[:, a:a + g], jnp.zeros((w.shape[0], LANES - g), F32)], axis=1)
    o_ref[:, GATE_OFF:P_PAD] = gates.astype(BF16)


def _pack_w_in(w_in):
    depth, d, p_in = w_in.shape
    return pl.pallas_call(
        _pack_w_in_kernel,
        grid=(depth, d // PACK_ROWS),
        in_specs=[pl.BlockSpec((None, PACK_ROWS, p_in), lambda l, i: (l, i, 0))],
        out_specs=pl.BlockSpec((None, PACK_ROWS, P_PAD), lambda l, i: (l, i, 0)),
        out_shape=jax.ShapeDtypeStruct((depth, d, P_PAD), BF16),
        compiler_params=pltpu.CompilerParams(
            dimension_semantics=("arbitrary", "arbitrary"),
            vmem_limit_bytes=_vmem_limit(2 * PACK_ROWS * (p_in * 4 + P_PAD * 2) + 4 * PACK_ROWS * p_in * 4)),
        name="pack_w_in",
    )(w_in)


def kernel(x_prompt, x_sample, state_mlstm_C, state_mlstm_n, state_mlstm_m, state_ret, state_hgrn, state_conv,
           w_in, b_mlstm_gate, norm_a, norm_b, norm_c, lb_c, conv_w, w_branch, w_gate, b_gate, w_out,
           ln1_g, ln1_b, ln2_g, ln2_b, w_up, w_down):
    bp, tp, _ = x_prompt.shape
    bs, ts, _ = x_sample.shape
    n_p, n_s = bp * tp, bs * ts
    tm = 512
    tt = 256
    bb = 8
    assert n_p % tm == 0 and n_s % tm == 0 and tp % tt == 0 and bs % bb == 0 and n_p % (bb * ts) == 0

    xp = x_prompt.reshape(n_p, D_MODEL)
    xs = x_sample.reshape(n_s, D_MODEL)
    tabs_p = _position_tables(jnp.arange(tp, dtype=F32), math.gcd(tp, CHUNK))
    tabs_s = _position_tables(PAST_LEN + jnp.arange(ts, dtype=F32), math.gcd(ts, CHUNK))
    gla_p = _gla_tables(math.gcd(tp, CHUNK))
    gla_s = _gla_tables(math.gcd(ts, CHUNK))
    lbc = lb_c.astype(F32)
    row = lambda a: a.reshape(1, -1).astype(F32)
    w_in_p = _pack_w_in(w_in)
    wb, wg, wo, wu, wd = (w.astype(BF16) for w in (w_branch, w_gate, w_out, w_up, w_down))
    st_in = (state_mlstm_C.astype(F32), state_mlstm_n.astype(F32),
             jnp.broadcast_to(state_mlstm_m.astype(F32)[..., None], (DEPTH, bs, N_HEADS, LANES)),
             state_ret.astype(F32), state_hgrn.astype(F32), state_conv.astype(F32))

    p_states, s_states = [], ()
    for l in range(DEPTH):
        prms = (_pad_lanes(row(b_mlstm_gate[l])), row(norm_a[l]), row(norm_b[l]), row(norm_c[l]), lbc,
                conv_w[l].astype(F32))
        z = _inproj(xp, xs, l, w_in_p, 256)
        outs_p = _mixer_prompt(z, bp, tp, tt, l, tabs_p, prms + gla_p)
        outs_s = _mixer_sample(z, n_p, bs, ts, bb, l, tabs_s, prms + gla_s, st_in, s_states)
        hp, hs = _merge(xp, xs, outs_p[0], outs_s[0], l, wb, wg, row(b_gate[l]), wo, row(ln1_g[l]), row(ln1_b[l]), tm)
        xp, xs = _ffn(hp, hs, l, wu, wd, row(ln2_g[l]), row(ln2_b[l]), tm)
        p_states.append(outs_p[1:])
        s_states = tuple(outs_s[1:])

    cs, ns, ms, srs, shs, bufs = zip(*p_states)
    p_out = (jnp.stack(cs), jnp.stack(ns), jnp.stack([m[..., 0] for m in ms]),
             jnp.stack(srs), jnp.stack(shs), jnp.stack(bufs))
    s_out = s_states[:2] + (s_states[2][..., 0],) + s_states[3:]
    return (xp.reshape(bp, tp, D_MODEL), xs.reshape(bs, ts, D_MODEL)) + p_out + s_out
```

```python
import functools
import math

import jax
import jax.numpy as jnp
import numpy as np
from jax import lax
from jax.experimental import pallas as pl
from jax.experimental.pallas import tpu as pltpu

F32 = jnp.float32
BF16 = jnp.bfloat16

D_MODEL = 1024
DEPTH = 2
BRANCH_W = D_MODEL // 2
N_HEADS = 4
HEAD_DIM = BRANCH_W // N_HEADS
CONV_W = 3
N_BRANCH = 4
D_FF = 4 * D_MODEL
CHUNK = 64
ROPE_BASE = 10000.0
LN_EPS = 1e-5
NEG_BIG = -1e30
GATE_CLAMP = 1.0 - 1e-6
ALPHA = (2 * DEPTH) ** 0.25
PAST_LEN = 16384
K_SCALE = HEAD_DIM ** -0.5

LANES = 128
SUBLANES = 8
VMEM_BYTES = 64 * 1024 * 1024

(QA, KA, VA, OA, QB, KB, VB, GB, QC, FC, IC, GC, XD, BGD, CGD) = [i * BRANCH_W for i in range(15)]
GATE_OFF = 15 * BRANCH_W
P_PAD = GATE_OFF + LANES
Y_W = N_BRANCH * BRANCH_W


def _vmem_limit(est_bytes):
    return int(min(est_bytes + est_bytes // 4 + (4 << 20), VMEM_BYTES - (6 << 20)))


def _dot(a, b):
    return jnp.dot(a.astype(BF16), b.astype(BF16), preferred_element_type=F32)


def _dot_nt(a, b):
    return lax.dot_general(a.astype(BF16), b.astype(BF16), (((1,), (1,)), ((), ())),
                           preferred_element_type=F32)


def _dot_tn(a, b):
    return lax.dot_general(a.astype(BF16), b.astype(BF16), (((0,), (0,)), ((), ())),
                           preferred_element_type=F32)


def _split3(x):
    hi = x.astype(BF16)
    r = x - hi.astype(F32)
    mid = r.astype(BF16)
    lo = (r - mid.astype(F32)).astype(BF16)
    return hi, mid, lo


def _dot_exact_rhs(a01x3, x):
    return jnp.dot(a01x3, _stack3(x), preferred_element_type=F32)


def _stack3(x):
    parts = _split3(x)
    if x.shape[0] % (2 * SUBLANES) == 0:
        return jnp.concatenate(parts, axis=0)
    return jnp.concatenate([p.astype(F32) for p in parts], axis=0).astype(BF16)


def _dot_nt_exact_rhs(a01, x):
    hi, mid, lo = _split3(x)
    d = lambda p: lax.dot_general(a01, p, (((1,), (1,)), ((), ())), preferred_element_type=F32)
    return (d(hi) + d(mid)) + d(lo)


def _silu(x):
    return x * jax.nn.sigmoid(x)


def _group_specs(tm, width, npb, nsb):
    return [pl.BlockSpec((tm, width), lambda i: (jnp.minimum(i, npb - 1), 0)),
            pl.BlockSpec((tm, width), lambda i: (jnp.clip(i - npb, 0, nsb - 1), 0))]


def _group_load(npb, p_ref, s_ref):
    return jnp.where(pl.program_id(0) < npb, p_ref[...], s_ref[...])


def _group_store(npb, p_ref, s_ref, val):
    is_p = pl.program_id(0) < npb

    @pl.when(is_p)
    def _():
        p_ref[...] = val

    @pl.when(jnp.logical_not(is_p))
    def _():
        s_ref[...] = val


def _inproj_kernel(npb, xp_ref, xs_ref, w_ref, z_ref):
    xb = _group_load(npb, xp_ref, xs_ref).astype(BF16)
    for off in range(0, P_PAD, BRANCH_W):
        wd = min(BRANCH_W, P_PAD - off)
        z_ref[:, off:off + wd] = lax.dot_general(xb, w_ref[off:off + wd, :], (((1,), (1,)), ((), ())),
                                                 preferred_element_type=F32)


def _inproj(xp, xs, layer, w, tm):
    npb, nsb = xp.shape[0] // tm, xs.shape[0] // tm
    est = D_MODEL * P_PAD * 2 * 2 + 2 * tm * P_PAD * 4 + 2 * 2 * tm * D_MODEL * 4
    return pl.pallas_call(
        functools.partial(_inproj_kernel, npb),
        grid=(npb + nsb,),
        in_specs=_group_specs(tm, D_MODEL, npb, nsb) + [_layer_spec(layer, P_PAD, D_MODEL)],
        out_specs=pl.BlockSpec((tm, P_PAD), lambda i: (i, 0)),
        out_shape=jax.ShapeDtypeStruct(((npb + nsb) * tm, P_PAD), F32),
        compiler_params=pltpu.CompilerParams(dimension_semantics=("arbitrary",),
                                             vmem_limit_bytes=_vmem_limit(est)),
        name="inproj",
    )(xp, xs, w)


class _State:
    def __init__(self, get, put):
        self.get = get
        self.put = put


LEVEL1_DONE, STATE_STORED = "level 1 done", "state stored"


def _advance(g, until=None):
    for label in g:
        if until is not None and label == until:
            return


def _pipelined(chunk_generators):
    pending_tail = None
    for g in chunk_generators:
        _advance(g, LEVEL1_DONE)
        if pending_tail is not None:
            _advance(pending_tail)
        _advance(g, STATE_STORED)
        pending_tail = g
    _advance(pending_tail)


def _mixer_chunk(zc, c, st, cst, prm, put_y):
    tril, causal, sel8 = cst["tril"], cst["causal"], cst["sel8"]
    old = {(name, h): st.get(name, h) for name in ("C", "n", "m", "Sr", "ShT") for h in range(N_HEADS)}
    old[("buf", 0)] = st.get("buf", 0)
    new = {}

    heads = range(N_HEADS)
    hd = lambda off, h: zc(off + h * HEAD_DIM, HEAD_DIM)
    hsl = lambda a, h: a[:, h * HEAD_DIM:(h + 1) * HEAD_DIM]
    cos2, sin2 = cst["cos2"], cst["sin2"]
    rope = lambda x: x * cos2 + pltpu.roll(x, HEAD_DIM // 2, axis=1) * sin2
    lv = cst["gla_lv"][...]

    gp = zc(GATE_OFF, LANES) + prm["bias_if"]
    lf = jnp.minimum(gp, 0.0) - jnp.log1p(jnp.exp(-jnp.abs(gp)))
    b_all = _dot_exact_rhs(tril, lf)
    gp_t = _dot_nt_exact_rhs(sel8, gp)
    s_a, qc_a, s_b, qs_b, a_c = [], [], [], [], []
    for h in heads:
        qa = hd(QA, h)
        s_a.append(_dot_nt(qa, hd(KA, h) * K_SCALE))
        qc_a.append(_dot(qa, old[("C", h)]))
    for h in heads:
        qb = rope(hd(QB, h))
        kb = rope(hd(KB, h)) * K_SCALE
        s_b.append(_dot_nt(qb, kb))
        qs_b.append(_dot(qb, old[("Sr", h)]))
        new[("Sr", h)] = (cst["ret_cdec"][h:h + 1, :] * old[("Sr", h)]
                          + _dot_tn(kb * cst["ret_toend"][:, h:h + 1], hd(VB, h)))
    kc_all = (1.0 - prm["lb"]) * jax.nn.sigmoid(-zc(FC, BRANCH_W))
    g_all = jnp.log1p(-jnp.minimum(kc_all, GATE_CLAMP))
    xs = _dot_exact_rhs(cst["gla_m"][...], g_all)
    bc_all = xs[0:c]
    qg_all = _silu(zc(QC, BRANCH_W))
    for h in heads:
        qg, kc = hsl(qg_all, h), hsl(kc_all, h)
        a = jnp.where(lv == 0, _dot_nt(qg, kc), 0.0)
        a_c.append(jnp.where(lv == 1, _dot_nt(qg * jnp.exp(hsl(g_all, h)), kc), a))

    yield LEVEL1_DONE
    b_t = _dot_nt_exact_rhs(sel8, b_all)
    qn_a = [jnp.sum(hd(QA, h) * old[("n", h)], axis=1, keepdims=True) for h in heads]
    o_b = []
    for h in heads:
        o_b.append(_dot(s_b[h] * cst["ret_dec"](h), hd(VB, h)) + qs_b[h] * cst["ret_inter"][:, h:h + 1])
    num_a, den_a, oi_c = [], [], []
    for h in heads:
        b_col = b_all[:, N_HEADS + h:N_HEADS + h + 1]
        b_row = b_t[N_HEADS + h:N_HEADS + h + 1, :]
        a_col = b_col + old[("m", h)][:, 0:1]
        dmat = jnp.where(causal, (b_col - b_row) + gp_t[h:h + 1, :], NEG_BIG)
        m_t = jnp.maximum(a_col, jnp.max(dmat, axis=1, keepdims=True))
        w_inter = jnp.exp(a_col - m_t)
        s = s_a[h] * jnp.exp(dmat - m_t)
        num_a.append(_dot(s, hd(VA, h)) + qc_a[h] * w_inter)
        den = jnp.sum(s, axis=1, keepdims=True) + qn_a[h] * w_inter
        den_a.append(jnp.maximum(jnp.abs(den), jnp.exp(-m_t)))
    for h in heads:
        b_col = b_all[:, N_HEADS + h:N_HEADS + h + 1]
        m_prev = old[("m", h)][:, 0:1]
        b_last = b_col[c - 1:c, :]
        g_col = (b_last - b_col) + gp[:, h:h + 1]
        m_new = jnp.maximum(b_last + m_prev, jnp.max(g_col, axis=0, keepdims=True))
        dec = jnp.exp(b_last + m_prev - m_new)
        kw = (hd(KA, h) * K_SCALE) * jnp.exp(g_col - m_new)
        new[("C", h)] = dec * old[("C", h)] + _dot_tn(kw, hd(VA, h))
        new[("n", h)] = dec * old[("n", h)] + jnp.sum(kw, axis=0, keepdims=True)
        new[("m", h)] = jnp.broadcast_to(m_new, (1, LANES))
    for h in heads:
        qg, kc, bc = hsl(qg_all, h), hsl(kc_all, h), hsl(bc_all, h)
        a, sec, n = a_c[h], 1, c // 2
        while n > 1:
            e = jnp.exp(hsl(xs[sec * c:(sec + 1) * c], h))
            a = jnp.where(lv == n, _dot_nt(qg * e, kc * e), a)
            sec, n = sec + 1, n // 2
        a_c[h] = a
        sh_t = old[("ShT", h)]
        oi_c.append(_dot_nt(qg * jnp.exp(bc), sh_t))
        bc_last = bc[c - 1:c, :]
        new[("ShT", h)] = jnp.exp(bc_last) * sh_t + _dot_tn(hd(IC, h), kc * jnp.exp(bc_last - bc))
    o_c = [_dot(a_c[h], hd(IC, h)) + oi_c[h] for h in heads]
    u = zc(CGD, BRANCH_W) * zc(XD, BRANCH_W)
    new[("buf", 0)] = u[c - 2:c, :]
    for (name, h), val in new.items():
        st.put(name, h, val)

    yield STATE_STORED
    ya, yb, yc = [], [], []
    for h in heads:
        hh = num_a[h] / den_a[h]
        hn = hh * lax.rsqrt(jnp.mean(hh * hh, axis=1, keepdims=True) + LN_EPS)
        ya.append(jax.nn.sigmoid(hd(OA, h)) * (hn * hsl(prm["norm_a"], h)))
    put_y(0, jnp.concatenate(ya, axis=1))
    buf = old[("buf", 0)]
    row = lax.broadcasted_iota(jnp.int32, (c, BRANCH_W), 0)
    u1 = jnp.where(row == 0, buf[1:2, :], pltpu.roll(u, 1, axis=0))
    u2 = jnp.where(row == 0, buf[0:1, :], jnp.where(row == 1, buf[1:2, :], pltpu.roll(u, 2, axis=0)))
    cw = prm["conv_w"]
    yconv = (u2 * cw[0:1, :] + u1 * cw[1:2, :]) + u * cw[2:3, :]
    put_y(3 * BRANCH_W, zc(BGD, BRANCH_W) * yconv)
    for h in heads:
        oc = o_b[h] - jnp.mean(o_b[h], axis=1, keepdims=True)
        on = oc * lax.rsqrt(jnp.mean(oc * oc, axis=1, keepdims=True) + LN_EPS)
        yb.append(_silu(hd(GB, h)) * (on * hsl(prm["norm_b"], h)))
    put_y(BRANCH_W, jnp.concatenate(yb, axis=1))
    for h in heads:
        on = o_c[h] * lax.rsqrt(jnp.mean(o_c[h] * o_c[h], axis=1, keepdims=True) + LN_EPS)
        yc.append(_silu(hd(GC, h)) * (on * hsl(prm["norm_c"], h)))
    put_y(2 * BRANCH_W, jnp.concatenate(yc, axis=1))


def _chunk_consts(c):
    r = lax.broadcasted_iota(jnp.int32, (c, c), 0)
    l = lax.broadcasted_iota(jnp.int32, (c, c), 1)
    causal = r >= l
    r3 = lax.broadcasted_iota(jnp.int32, (c, 3 * c), 0)
    l3 = lax.broadcasted_iota(jnp.int32, (c, 3 * c), 1)
    l3 = jnp.where(l3 >= 2 * c, l3 - 2 * c, jnp.where(l3 >= c, l3 - c, l3))
    r8 = lax.broadcasted_iota(jnp.int32, (SUBLANES, LANES), 0)
    l8 = lax.broadcasted_iota(jnp.int32, (SUBLANES, LANES), 1)
    return {"causal": causal,
            "tril": jnp.where(r3 >= l3, 1.0, 0.0).astype(BF16),
            "sel8": jnp.where(r8 == l8, 1.0, 0.0).astype(BF16)}


def _layer_lb(lbc, layer):
    e = jnp.exp(lbc - jnp.max(lbc, axis=0, keepdims=True))
    sm = e / jnp.sum(e, axis=0, keepdims=True)
    cum = sm[0:1, :]
    for j in range(1, layer + 1):
        cum = cum + sm[j:j + 1, :]
    return cum - sm[0:1, :]


def _load_params(bias_ref, na_ref, nb_ref, nc_ref, lbc_ref, cw_ref, layer):
    return {"bias_if": bias_ref[...], "norm_a": na_ref[...], "norm_b": nb_ref[...], "norm_c": nc_ref[...],
            "lb": _layer_lb(lbc_ref[...], layer), "conv_w": cw_ref[...]}


PROMPT_LOCKSTEP = 1


def _mixer_prompt_kernel(layer, c, tt, *refs):
    ns = PROMPT_LOCKSTEP
    z_refs = refs[:ns]
    (cos_ref, sin_ref, rdec_ref, rinter_ref, rtoend_ref, rcdec_ref,
     bias_ref, na_ref, nb_ref, nc_ref, lbc_ref, cw_ref, gm_ref, lv_ref,
     y_ref, c_out, n_out, m_out, sr_out, sh_out, buf_out,
     c_s, n_s, m_s, sr_s, sht_s, buf_s) = refs[ns:]
    j = pl.program_id(1)

    @pl.when(j == 0)
    def _():
        c_s[...] = jnp.zeros_like(c_s)
        n_s[...] = jnp.zeros_like(n_s)
        m_s[...] = jnp.zeros_like(m_s)
        sr_s[...] = jnp.zeros_like(sr_s)
        sht_s[...] = jnp.zeros_like(sht_s)
        buf_s[...] = jnp.zeros_like(buf_s)

    prm = _load_params(bias_ref, na_ref, nb_ref, nc_ref, lbc_ref, cw_ref, layer)
    cst = _chunk_consts(c)
    cst["ret_dec"] = lambda h: rdec_ref[h]
    cst["ret_inter"] = rinter_ref[...]
    cst["ret_toend"] = rtoend_ref[...]
    cst["ret_cdec"] = rcdec_ref[...]
    cst["gla_m"] = gm_ref
    cst["gla_lv"] = lv_ref

    def state(s):
        def get(name, h):
            if name == "C":
                return c_s[s, h]
            if name == "n":
                return n_s[s, h:h + 1, :]
            if name == "m":
                return m_s[s, h:h + 1, :]
            if name == "Sr":
                return sr_s[s, h]
            if name == "ShT":
                return sht_s[s, h]
            return buf_s[s, 0:CONV_W - 1, :]

        def put(name, h, val):
            if name == "C":
                c_s[s, h] = val
            elif name == "n":
                n_s[s, h:h + 1, :] = val
            elif name == "m":
                m_s[s, h:h + 1, :] = val
            elif name == "Sr":
                sr_s[s, h] = val
            elif name == "ShT":
                sht_s[s, h] = val
            else:
                buf_s[s, 0:CONV_W - 1, :] = val

        return _State(get, put)

    def chunk(s, k):
        r = k * c
        cc = dict(cst)
        cc["cos2"] = cos_ref[r:r + c, :]
        cc["sin2"] = sin_ref[r:r + c, :]
        zc = lambda off, w: z_refs[s][r:r + c, off:off + w]

        def put_y(off, val):
            y_ref[s, r:r + c, off:off + BRANCH_W] = val.astype(BF16)

        return _mixer_chunk(zc, c, state(s), cc, prm, put_y)

    for s in range(ns):
        _pipelined(chunk(s, k) for k in range(tt // c))

    @pl.when(j == pl.num_programs(1) - 1)
    def _():
        c_out[...] = c_s[...]
        n_out[...] = n_s[:, 0:N_HEADS, :]
        m_out[...] = m_s[:, 0:N_HEADS, :]
        sr_out[...] = sr_s[...]
        for s in range(ns):
            for h in range(N_HEADS):
                sh_out[s, h] = sht_s[s, h].T
        buf_out[...] = buf_s[:, 0:CONV_W - 1, :]


def _const_spec(shape):
    nd = len(shape)
    return pl.BlockSpec(shape, lambda *_: (0,) * nd)


def _mixer_prompt(z, bsz, t, tt, layer, tabs, prms):
    c = math.gcd(t, CHUNK)
    nj = t // tt
    ns = PROMPT_LOCKSTEP
    assert bsz % ns == 0
    cos2, sin2, rdec, rinter, rtoend, rcdec = tabs
    state_shapes = [
        jax.ShapeDtypeStruct((bsz, N_HEADS, HEAD_DIM, HEAD_DIM), F32),
        jax.ShapeDtypeStruct((bsz, N_HEADS, HEAD_DIM), F32),
        jax.ShapeDtypeStruct((bsz, N_HEADS, LANES), F32),
        jax.ShapeDtypeStruct((bsz, N_HEADS, HEAD_DIM, HEAD_DIM), F32),
        jax.ShapeDtypeStruct((bsz, N_HEADS, HEAD_DIM, HEAD_DIM), F32),
        jax.ShapeDtypeStruct((bsz, CONV_W - 1, BRANCH_W), F32),
    ]
    big = pl.BlockSpec((ns, N_HEADS, HEAD_DIM, HEAD_DIM), lambda g, j: (g, 0, 0, 0))
    small = pl.BlockSpec((ns, N_HEADS, LANES), lambda g, j: (g, 0, 0))
    est = (2 * ns * tt * P_PAD * 4 + 2 * ns * tt * Y_W * 2 + 10 * ns * N_HEADS * HEAD_DIM * HEAD_DIM * 4 + (6 << 20))

    def z_spec(s, width):
        return pl.BlockSpec((tt, width), lambda g, j: ((g * ns + s) * nj + j, 0))

    outs = pl.pallas_call(
        functools.partial(_mixer_prompt_kernel, layer, c, tt),
        grid=(bsz // ns, nj),
        in_specs=[z_spec(s, P_PAD) for s in range(ns)]
                 + [pl.BlockSpec((tt, LANES), lambda g, j: (j, 0)),
                    pl.BlockSpec((tt, LANES), lambda g, j: (j, 0)),
                    _const_spec(rdec.shape), _const_spec(rinter.shape), _const_spec(rtoend.shape),
                    _const_spec(rcdec.shape)] + [_const_spec(p.shape) for p in prms],
        out_specs=[pl.BlockSpec((None, ns, tt, Y_W), lambda g, j: (g, 0, j, 0)),
                   big, small, small, big, big,
                   pl.BlockSpec((ns, CONV_W - 1, BRANCH_W), lambda g, j: (g, 0, 0))],
        out_shape=[jax.ShapeDtypeStruct((bsz // ns, ns, t, Y_W), BF16)] + state_shapes,
        scratch_shapes=[pltpu.VMEM((ns, N_HEADS, HEAD_DIM, HEAD_DIM), F32),
                        pltpu.VMEM((ns, SUBLANES, LANES), F32),
                        pltpu.VMEM((ns, SUBLANES, LANES), F32),
                        pltpu.VMEM((ns, N_HEADS, HEAD_DIM, HEAD_DIM), F32),
                        pltpu.VMEM((ns, N_HEADS, HEAD_DIM, HEAD_DIM), F32),
                        pltpu.VMEM((ns, SUBLANES, BRANCH_W), F32)],
        compiler_params=pltpu.CompilerParams(dimension_semantics=("arbitrary", "arbitrary"),
                                             vmem_limit_bytes=_vmem_limit(est)),
        name=f"mixer_prompt_l{layer}",
    )(*([z] * ns), cos2, sin2, rdec, rinter, rtoend, rcdec, *prms)
    return [outs[0].reshape(bsz * t, Y_W)] + list(outs[1:])


N_STATES = 6
N_MIXER_CONST_INPUTS = 15


def _mixer_sample_kernel(layer, c, bb, n_carried, *refs):
    (z_ref, cos_ref, sin_ref, rdec_ref, rinter_ref, rtoend_ref, rcdec_ref,
     bias_ref, na_ref, nb_ref, nc_ref, lbc_ref, cw_ref, gm_ref, lv_ref) = refs[:N_MIXER_CONST_INPUTS]
    c_in, n_in, m_in, sr_in, sh_in, buf_in = refs[N_MIXER_CONST_INPUTS:N_MIXER_CONST_INPUTS + N_STATES]
    y_ref, c_out, n_out, m_out, sr_out, sh_out, buf_out, ybuf = refs[N_MIXER_CONST_INPUTS + N_STATES + n_carried:]
    prm = _load_params(bias_ref, na_ref, nb_ref, nc_ref, lbc_ref, cw_ref, layer)
    cst = _chunk_consts(c)
    cst["ret_dec"] = lambda h: rdec_ref[h]
    cst["ret_inter"] = rinter_ref[...]
    cst["ret_toend"] = rtoend_ref[...]
    cst["ret_cdec"] = rcdec_ref[...]
    cst["gla_m"] = gm_ref
    cst["gla_lv"] = lv_ref
    cst["cos2"] = cos_ref[...]
    cst["sin2"] = sin_ref[...]

    def sequence(i):
        r = i * c
        zc = lambda off, w: z_ref[r:r + c, off:off + w]

        def get(name, h):
            if name == "C":
                return c_in[i, h]
            if name == "n":
                return n_in[i, pl.ds(h, 1), :]
            if name == "m":
                return m_in[i, pl.ds(h, 1), :]
            if name == "Sr":
                return sr_in[i, h]
            if name == "ShT":
                return sh_in[i, h].T
            return buf_in[i]

        def put(name, h, val):
            if name == "C":
                c_out[i, h] = val
            elif name == "n":
                n_out[i, pl.ds(h, 1), :] = val
            elif name == "m":
                m_out[i, pl.ds(h, 1), :] = val
            elif name == "Sr":
                sr_out[i, h] = val
            elif name == "ShT":
                sh_out[i, h] = val.T
            else:
                buf_out[i] = val

        def put_y(off, val):
            ybuf[r:r + c, off:off + BRANCH_W] = val

        return _mixer_chunk(zc, c, _State(get, put), cst, prm, put_y)

    _pipelined(sequence(i) for i in range(bb))
    y_ref[...] = ybuf[...].astype(BF16)


def _mixer_sample(z, row0, bsz, t, bb, layer, tabs, prms, states, carried):
    c = t
    rows = bb * t
    blk0 = row0 // rows
    cos2, sin2, rdec, rinter, rtoend, rcdec = tabs

    def lspec(*tail):
        zeros = (0,) * (len(tail) - 1)
        return pl.BlockSpec((None,) + tail, lambda i: (layer, i) + zeros)

    big = lspec(bb, N_HEADS, HEAD_DIM, HEAD_DIM)
    small = lspec(bb, N_HEADS, LANES)
    state_specs = [big, small, small, big, big, lspec(bb, CONV_W - 1, BRANCH_W)]
    state_shapes = [jax.ShapeDtypeStruct(s.shape, F32) for s in states]
    n_in = N_MIXER_CONST_INPUTS + N_STATES
    est = (2 * rows * P_PAD * 4 + 2 * rows * Y_W * 2 + rows * Y_W * 4
           + 2 * 2 * 3 * bb * N_HEADS * HEAD_DIM * HEAD_DIM * 4)
    return pl.pallas_call(
        functools.partial(_mixer_sample_kernel, layer, c, bb, len(carried)),
        grid=(bsz // bb,),
        in_specs=[pl.BlockSpec((rows, P_PAD), lambda i: (blk0 + i, 0)),
                  _const_spec(cos2.shape), _const_spec(sin2.shape),
                  _const_spec(rdec.shape), _const_spec(rinter.shape), _const_spec(rtoend.shape),
                  _const_spec(rcdec.shape)] + [_const_spec(p.shape) for p in prms] + state_specs
                 + [pl.BlockSpec(memory_space=pl.ANY)] * len(carried),
        out_specs=[pl.BlockSpec((rows, Y_W), lambda i: (i, 0))] + state_specs,
        out_shape=[jax.ShapeDtypeStruct((bsz * t, Y_W), BF16)] + state_shapes,
        input_output_aliases={n_in + k: 1 + k for k in range(len(carried))},
        scratch_shapes=[pltpu.VMEM((rows, Y_W), F32)],
        compiler_params=pltpu.CompilerParams(dimension_semantics=("arbitrary",),
                                             vmem_limit_bytes=_vmem_limit(est)),
        name=f"mixer_sample_l{layer}",
    )(z, cos2, sin2, rdec, rinter, rtoend, rcdec, *prms, *states, *carried)


def _layer_norm(r, g, b):
    mu = jnp.mean(r, axis=1, keepdims=True)
    rc = r - mu
    var = jnp.mean(rc * rc, axis=1, keepdims=True)
    return rc * lax.rsqrt(var + LN_EPS) * g + b


def _merge_kernel(npb, xp_ref, xs_ref, yp_ref, ys_ref, wb_ref, wg_ref, bg_ref, wo_ref, g_ref, b_ref,
                  hp_ref, hs_ref):
    x = _group_load(npb, xp_ref, xs_ref)
    xb = x.astype(BF16)
    y = _group_load(npb, yp_ref, ys_ref)
    acc = None
    for n in range(N_BRANCH):
        proj = jnp.dot(y[:, n * BRANCH_W:(n + 1) * BRANCH_W], wb_ref[n], preferred_element_type=F32)
        gate = jax.nn.sigmoid(jnp.dot(xb, wg_ref[:, n * D_MODEL:(n + 1) * D_MODEL], preferred_element_type=F32)
                              + bg_ref[:, n * D_MODEL:(n + 1) * D_MODEL])
        acc = gate * proj if acc is None else acc + gate * proj
    mix = jnp.dot(acc.astype(BF16), wo_ref[...], preferred_element_type=F32)
    _group_store(npb, hp_ref, hs_ref, _layer_norm(ALPHA * x + mix, g_ref[...], b_ref[...]))


def _layer_spec(layer, *tail):
    return pl.BlockSpec((None,) + tail, lambda *_: (layer,) + (0,) * len(tail))


def _group_out(xp, xs, tm, npb, nsb):
    return dict(out_specs=_group_specs(tm, D_MODEL, npb, nsb),
                out_shape=[jax.ShapeDtypeStruct(xp.shape, F32), jax.ShapeDtypeStruct(xs.shape, F32)])


def _merge(xp, xs, y_p, y_s, layer, wb, wg, bg, wo, g, b, tm):
    npb, nsb = xp.shape[0] // tm, xs.shape[0] // tm
    est = (2 * (N_BRANCH * BRANCH_W * D_MODEL + D_MODEL * N_BRANCH * D_MODEL + D_MODEL * D_MODEL) * 2
           + 2 * 2 * tm * D_MODEL * 4 * 2 + 2 * 2 * tm * Y_W * 2 + 6 * tm * D_MODEL * 4)
    return pl.pallas_call(
        functools.partial(_merge_kernel, npb),
        grid=(npb + nsb,),
        in_specs=_group_specs(tm, D_MODEL, npb, nsb) + _group_specs(tm, Y_W, npb, nsb)
                 + [_layer_spec(layer, N_BRANCH, BRANCH_W, D_MODEL), _layer_spec(layer, D_MODEL, N_BRANCH * D_MODEL),
                    _const_spec(bg.shape), _layer_spec(layer, D_MODEL, D_MODEL),
                    _const_spec(g.shape), _const_spec(b.shape)],
        compiler_params=pltpu.CompilerParams(dimension_semantics=("arbitrary",),
                                             vmem_limit_bytes=_vmem_limit(est)),
        name="merge",
        **_group_out(xp, xs, tm, npb, nsb),
    )(xp, xs, y_p, y_s, wb, wg, bg, wo, g, b)


def _ffn_kernel(npb, hp_ref, hs_ref, wu_ref, wd_ref, g_ref, b_ref, op_ref, os_ref):
    h = _group_load(npb, hp_ref, hs_ref)
    u = jnp.maximum(jnp.dot(h.astype(BF16), wu_ref[...], preferred_element_type=F32), 0.0)
    ff = jnp.dot((u * u).astype(BF16), wd_ref[...], preferred_element_type=F32)
    _group_store(npb, op_ref, os_ref, _layer_norm(ALPHA * h + ff, g_ref[...], b_ref[...]))


def _ffn(hp, hs, layer, wu, wd, g, b, tm):
    npb, nsb = hp.shape[0] // tm, hs.shape[0] // tm
    est = 2 * 2 * D_MODEL * D_FF * 2 + 2 * 2 * 2 * tm * D_MODEL * 4 + tm * D_FF * 6 + 2 * tm * D_MODEL * 4
    return pl.pallas_call(
        functools.partial(_ffn_kernel, npb),
        grid=(npb + nsb,),
        in_specs=_group_specs(tm, D_MODEL, npb, nsb)
                 + [_layer_spec(layer, D_MODEL, D_FF), _layer_spec(layer, D_FF, D_MODEL),
                    _const_spec(g.shape), _const_spec(b.shape)],
        compiler_params=pltpu.CompilerParams(dimension_semantics=("arbitrary",),
                                             vmem_limit_bytes=_vmem_limit(est)),
        name="ffn",
        **_group_out(hp, hs, tm, npb, nsb),
    )(hp, hs, wu, wd, g, b)


def _pad_lanes(a):
    return jnp.pad(a, ((0, 0), (0, LANES - a.shape[1])))


def _position_tables(pos, c):
    half = HEAD_DIM // 2
    inv = ROPE_BASE ** (-jnp.arange(half, dtype=F32) / half)
    ang = pos[:, None] * inv[None, :]
    cos, sin = jnp.cos(ang), jnp.sin(ang)
    cos2 = jnp.concatenate([cos, cos], axis=1)
    sin2 = jnp.concatenate([-sin, sin], axis=1)
    log_gamma = jnp.log(1.0 - 2.0 ** (-5.0 - jnp.arange(N_HEADS, dtype=F32)))
    idx = jnp.arange(c, dtype=F32)
    rel = idx[:, None] - idx[None, :]
    dec = jnp.where(rel >= 0, jnp.exp(log_gamma[:, None, None] * jnp.maximum(rel, 0.0)), 0.0)
    inter = jnp.swapaxes(jnp.exp(log_gamma[:, None] * (idx + 1.0)[None, :]), 0, 1)
    to_end = jnp.swapaxes(jnp.exp(log_gamma[:, None] * (c - 1.0 - idx)[None, :]), 0, 1)
    cdec = jnp.broadcast_to(jnp.exp(log_gamma * c)[:, None], (N_HEADS, LANES))
    cdec = jnp.pad(cdec, ((0, SUBLANES - N_HEADS), (0, 0)))
    return cos2, sin2, dec, _pad_lanes(inter), _pad_lanes(to_end), cdec


def _gla_tables(c):
    r = np.arange(c)[:, None]
    j = np.arange(c)[None, :]
    mats = [j <= r]
    n = c // 2
    while n > 1:
        start = (r // n) * n
        odd = (r // n) % 2 == 1
        mats.append(np.where(odd, (j >= start) & (j <= r), (j > r) & (j <= start + n - 1)))
        n //= 2
    x = np.maximum(r ^ j, 1)
    level = np.where(j > r, -1, np.where(j == r, 0, 1 << np.floor(np.log2(x)).astype(np.int64)))
    return jnp.asarray(np.tile(np.concatenate(mats, 0), (1, 3)), BF16), jnp.asarray(level, jnp.int32)


def _pack_w_in(w_in):
    a = 4 * BRANCH_W
    g = 2 * N_HEADS
    wt = jnp.swapaxes(w_in, 1, 2).astype(BF16)
    pad = jnp.zeros((wt.shape[0], LANES - g, wt.shape[2]), BF16)
    return jnp.concatenate([wt[:, :a], wt[:, a + g:], wt[:, a:a + g], pad], axis=1)


def kernel(x_prompt, x_sample, state_mlstm_C, state_mlstm_n, state_mlstm_m, state_ret, state_hgrn, state_conv,
           w_in, b_mlstm_gate, norm_a, norm_b, norm_c, lb_c, conv_w, w_branch, w_gate, b_gate, w_out,
           ln1_g, ln1_b, ln2_g, ln2_b, w_up, w_down):
    bp, tp, _ = x_prompt.shape
    bs, ts, _ = x_sample.shape
    n_p, n_s = bp * tp, bs * ts
    tm = 512
    tt = 256
    bb = 8
    assert n_p % tm == 0 and n_s % tm == 0 and tp % tt == 0 and bs % bb == 0 and n_p % (bb * ts) == 0

    xp = x_prompt.reshape(n_p, D_MODEL)
    xs = x_sample.reshape(n_s, D_MODEL)
    tabs_p = _position_tables(jnp.arange(tp, dtype=F32), math.gcd(tp, CHUNK))
    tabs_s = _position_tables(PAST_LEN + jnp.arange(ts, dtype=F32), math.gcd(ts, CHUNK))
    gla_p = _gla_tables(math.gcd(tp, CHUNK))
    gla_s = _gla_tables(math.gcd(ts, CHUNK))
    lbc = lb_c.astype(F32)
    row = lambda a: a.reshape(1, -1).astype(F32)
    w_in_p = _pack_w_in(w_in)
    wb, wg, wo, wu, wd = (w.astype(BF16) for w in (w_branch, w_gate, w_out, w_up, w_down))
    st_in = (state_mlstm_C.astype(F32), state_mlstm_n.astype(F32),
             jnp.broadcast_to(state_mlstm_m.astype(F32)[..., None], (DEPTH, bs, N_HEADS, LANES)),
             state_ret.astype(F32), state_hgrn.astype(F32), state_conv.astype(F32))

    p_states, s_states = [], ()
    for l in range(DEPTH):
        prms = (_pad_lanes(row(b_mlstm_gate[l])), row(norm_a[l]), row(norm_b[l]), row(norm_c[l]), lbc,
                conv_w[l].astype(F32))
        z = _inproj(xp, xs, l, w_in_p, 256)
        outs_p = _mixer_prompt(z, bp, tp, tt, l, tabs_p, prms + gla_p)
        outs_s = _mixer_sample(z, n_p, bs, ts, bb, l, tabs_s, prms + gla_s, st_in, s_states)
        hp, hs = _merge(xp, xs, outs_p[0], outs_s[0], l, wb, wg, row(b_gate[l]), wo, row(ln1_g[l]), row(ln1_b[l]), tm)
        xp, xs = _ffn(hp, hs, l, wu, wd, row(ln2_g[l]), row(ln2_b[l]), tm)
        p_states.append(outs_p[1:])
        s_states = tuple(outs_s[1:])

    cs, ns, ms, srs, shs, bufs = zip(*p_states)
    p_out = (jnp.stack(cs), jnp.stack(ns), jnp.stack([m[..., 0] for m in ms]),
             jnp.stack(srs), jnp.stack(shs), jnp.stack(bufs))
    s_out = s_states[:2] + (s_states[2][..., 0],) + s_states[3:]
    return (xp.reshape(bp, tp, D_MODEL), xs.reshape(bs, ts, D_MODEL)) + p_out + s_out
```

```python
import functools
import math

import jax
import jax.numpy as jnp
import numpy as np
from jax import lax
from jax.experimental import pallas as pl
from jax.experimental.pallas import tpu as pltpu

F32 = jnp.float32
BF16 = jnp.bfloat16

D_MODEL = 1024
DEPTH = 2
BRANCH_W = D_MODEL // 2
N_HEADS = 4
HEAD_DIM = BRANCH_W // N_HEADS
CONV_W = 3
N_BRANCH = 4
D_FF = 4 * D_MODEL
CHUNK = 64
ROPE_BASE = 10000.0
LN_EPS = 1e-5
NEG_BIG = -1e30
GATE_CLAMP = 1.0 - 1e-6
ALPHA = (2 * DEPTH) ** 0.25
PAST_LEN = 16384
K_SCALE = HEAD_DIM ** -0.5

LANES = 128
SUBLANES = 8
VMEM_BYTES = 64 * 1024 * 1024

(QA, KA, VA, OA, QB, KB, VB, GB, QC, FC, IC, GC, XD, BGD, CGD) = [i * BRANCH_W for i in range(15)]
GATE_OFF = 15 * BRANCH_W
P_PAD = GATE_OFF + LANES
Y_W = N_BRANCH * BRANCH_W


def _vmem_limit(est_bytes):
    return int(min(est_bytes + est_bytes // 4 + (4 << 20), VMEM_BYTES - (6 << 20)))


def _dot(a, b):
    return jnp.dot(a.astype(BF16), b.astype(BF16), preferred_element_type=F32)


def _dot_nt(a, b):
    return lax.dot_general(a.astype(BF16), b.astype(BF16), (((1,), (1,)), ((), ())),
                           preferred_element_type=F32)


def _dot_tn(a, b):
    return lax.dot_general(a.astype(BF16), b.astype(BF16), (((0,), (0,)), ((), ())),
                           preferred_element_type=F32)


def _split3(x):
    hi = x.astype(BF16)
    r = x - hi.astype(F32)
    mid = r.astype(BF16)
    lo = (r - mid.astype(F32)).astype(BF16)
    return hi, mid, lo


def _dot_exact_rhs(a01x3, x):
    return jnp.dot(a01x3, _stack3(x), preferred_element_type=F32)


def _stack3(x):
    parts = _split3(x)
    if x.shape[0] % (2 * SUBLANES) == 0:
        return jnp.concatenate(parts, axis=0)
    return jnp.concatenate([p.astype(F32) for p in parts], axis=0).astype(BF16)


def _dot_nt_exact_rhs(a01, x):
    hi, mid, lo = _split3(x)
    d = lambda p: lax.dot_general(a01, p, (((1,), (1,)), ((), ())), preferred_element_type=F32)
    return (d(hi) + d(mid)) + d(lo)


def _silu(x):
    return x * jax.nn.sigmoid(x)


def _group_specs(tm, width, npb, nsb):
    return [pl.BlockSpec((tm, width), lambda i: (jnp.minimum(i, npb - 1), 0)),
            pl.BlockSpec((tm, width), lambda i: (jnp.clip(i - npb, 0, nsb - 1), 0))]


def _group_load(npb, p_ref, s_ref):
    return jnp.where(pl.program_id(0) < npb, p_ref[...], s_ref[...])


def _group_store(npb, p_ref, s_ref, val):
    is_p = pl.program_id(0) < npb

    @pl.when(is_p)
    def _():
        p_ref[...] = val

    @pl.when(jnp.logical_not(is_p))
    def _():
        s_ref[...] = val


P_IN = 15 * BRANCH_W + 2 * N_HEADS
GATE_FEATURES = (4 * BRANCH_W, 4 * BRANCH_W + 2 * N_HEADS)


def _inproj_kernel(npb, xp_ref, xs_ref, wt_ref, z_ref):
    xb = _group_load(npb, xp_ref, xs_ref).astype(BF16)
    nt = lambda w: lax.dot_general(xb, w.astype(BF16), (((1,), (1,)), ((), ())), preferred_element_type=F32)
    g0, g1 = GATE_FEATURES
    for off in range(0, GATE_OFF, BRANCH_W):
        src = off if off < g0 else off + (g1 - g0)
        z_ref[:, off:off + BRANCH_W] = nt(wt_ref[src:src + BRANCH_W, :])
    gates = nt(wt_ref[g0:g1, :])
    z_ref[:, GATE_OFF:P_PAD] = jnp.concatenate(
        [gates, jnp.zeros((gates.shape[0], P_PAD - GATE_OFF - (g1 - g0)), F32)], axis=1)


def _inproj(xp, xs, layer, wt, tm):
    npb, nsb = xp.shape[0] // tm, xs.shape[0] // tm
    est = D_MODEL * P_IN * 4 + 2 * tm * P_PAD * 4 + 2 * 2 * tm * D_MODEL * 4
    return pl.pallas_call(
        functools.partial(_inproj_kernel, npb),
        grid=(npb + nsb,),
        in_specs=_group_specs(tm, D_MODEL, npb, nsb) + [_layer_spec(layer, P_IN, D_MODEL)],
        out_specs=pl.BlockSpec((tm, P_PAD), lambda i: (i, 0)),
        out_shape=jax.ShapeDtypeStruct(((npb + nsb) * tm, P_PAD), F32),
        compiler_params=pltpu.CompilerParams(dimension_semantics=("arbitrary",),
                                             vmem_limit_bytes=_vmem_limit(est)),
        name="inproj",
    )(xp, xs, wt)


class _State:
    def __init__(self, get, put):
        self.get = get
        self.put = put


LEVEL1_DONE, STATE_STORED = "level 1 done", "state stored"


def _advance(g, until=None):
    for label in g:
        if until is not None and label == until:
            return


def _pipelined(chunk_generators):
    pending_tail = None
    for g in chunk_generators:
        _advance(g, LEVEL1_DONE)
        if pending_tail is not None:
            _advance(pending_tail)
        _advance(g, STATE_STORED)
        pending_tail = g
    _advance(pending_tail)


def _mixer_chunk(zc, c, st, cst, prm, put_y):
    tril, causal, sel8 = cst["tril"], cst["causal"], cst["sel8"]
    old = {(name, h): st.get(name, h) for name in ("C", "n", "m", "Sr", "ShT") for h in range(N_HEADS)}
    old[("buf", 0)] = st.get("buf", 0)
    new = {}

    heads = range(N_HEADS)
    hd = lambda off, h: zc(off + h * HEAD_DIM, HEAD_DIM)
    hsl = lambda a, h: a[:, h * HEAD_DIM:(h + 1) * HEAD_DIM]
    cos2, sin2 = cst["cos2"], cst["sin2"]
    rope = lambda x: x * cos2 + pltpu.roll(x, HEAD_DIM // 2, axis=1) * sin2
    lv = cst["gla_lv"][...]

    gp = zc(GATE_OFF, LANES) + prm["bias_if"]
    lf = jnp.minimum(gp, 0.0) - jnp.log1p(jnp.exp(-jnp.abs(gp)))
    b_all = _dot_exact_rhs(tril, lf)
    gp_t = _dot_nt_exact_rhs(sel8, gp)
    s_a, qc_a, s_b, qs_b, a_c = [], [], [], [], []
    for h in heads:
        qa = hd(QA, h)
        s_a.append(_dot_nt(qa, hd(KA, h) * K_SCALE))
        qc_a.append(_dot(qa, old[("C", h)]))
    for h in heads:
        qb = rope(hd(QB, h))
        kb = rope(hd(KB, h)) * K_SCALE
        s_b.append(_dot_nt(qb, kb))
        qs_b.append(_dot(qb, old[("Sr", h)]))
        new[("Sr", h)] = (cst["ret_cdec"][h:h + 1, :] * old[("Sr", h)]
                          + _dot_tn(kb * cst["ret_toend"][:, h:h + 1], hd(VB, h)))
    kc_all = (1.0 - prm["lb"]) * jax.nn.sigmoid(-zc(FC, BRANCH_W))
    g_all = jnp.log1p(-jnp.minimum(kc_all, GATE_CLAMP))
    xs = _dot_exact_rhs(cst["gla_m"][...], g_all)
    bc_all = xs[0:c]
    qg_all = _silu(zc(QC, BRANCH_W))
    for h in heads:
        qg, kc = hsl(qg_all, h), hsl(kc_all, h)
        a = jnp.where(lv == 0, _dot_nt(qg, kc), 0.0)
        a_c.append(jnp.where(lv == 1, _dot_nt(qg * jnp.exp(hsl(g_all, h)), kc), a))

    yield LEVEL1_DONE
    b_t = _dot_nt_exact_rhs(sel8, b_all)
    qn_a = [jnp.sum(hd(QA, h) * old[("n", h)], axis=1, keepdims=True) for h in heads]
    o_b = []
    for h in heads:
        o_b.append(_dot(s_b[h] * cst["ret_dec"](h), hd(VB, h)) + qs_b[h] * cst["ret_inter"][:, h:h + 1])
    num_a, den_a, oi_c = [], [], []
    for h in heads:
        b_col = b_all[:, N_HEADS + h:N_HEADS + h + 1]
        b_row = b_t[N_HEADS + h:N_HEADS + h + 1, :]
        a_col = b_col + old[("m", h)][:, 0:1]
        dmat = jnp.where(causal, (b_col - b_row) + gp_t[h:h + 1, :], NEG_BIG)
        m_t = jnp.maximum(a_col, jnp.max(dmat, axis=1, keepdims=True))
        w_inter = jnp.exp(a_col - m_t)
        s = s_a[h] * jnp.exp(dmat - m_t)
        num_a.append(_dot(s, hd(VA, h)) + qc_a[h] * w_inter)
        den = jnp.sum(s, axis=1, keepdims=True) + qn_a[h] * w_inter
        den_a.append(jnp.maximum(jnp.abs(den), jnp.exp(-m_t)))
    for h in heads:
        b_col = b_all[:, N_HEADS + h:N_HEADS + h + 1]
        m_prev = old[("m", h)][:, 0:1]
        b_last = b_col[c - 1:c, :]
        g_col = (b_last - b_col) + gp[:, h:h + 1]
        m_new = jnp.maximum(b_last + m_prev, jnp.max(g_col, axis=0, keepdims=True))
        dec = jnp.exp(b_last + m_prev - m_new)
        kw = (hd(KA, h) * K_SCALE) * jnp.exp(g_col - m_new)
        new[("C", h)] = dec * old[("C", h)] + _dot_tn(kw, hd(VA, h))
        new[("n", h)] = dec * old[("n", h)] + jnp.sum(kw, axis=0, keepdims=True)
        new[("m", h)] = jnp.broadcast_to(m_new, (1, LANES))
    for h in heads:
        qg, kc, bc = hsl(qg_all, h), hsl(kc_all, h), hsl(bc_all, h)
        a, sec, n = a_c[h], 1, c // 2
        while n > 1:
            e = jnp.exp(hsl(xs[sec * c:(sec + 1) * c], h))
            a = jnp.where(lv == n, _dot_nt(qg * e, kc * e), a)
            sec, n = sec + 1, n // 2
        a_c[h] = a
        sh_t = old[("ShT", h)]
        oi_c.append(_dot_nt(qg * jnp.exp(bc), sh_t))
        bc_last = bc[c - 1:c, :]
        new[("ShT", h)] = jnp.exp(bc_last) * sh_t + _dot_tn(hd(IC, h), kc * jnp.exp(bc_last - bc))
    o_c = [_dot(a_c[h], hd(IC, h)) + oi_c[h] for h in heads]
    u = zc(CGD, BRANCH_W) * zc(XD, BRANCH_W)
    new[("buf", 0)] = u[c - 2:c, :]
    for (name, h), val in new.items():
        st.put(name, h, val)

    yield STATE_STORED
    ya, yb, yc = [], [], []
    for h in heads:
        hh = num_a[h] / den_a[h]
        hn = hh * lax.rsqrt(jnp.mean(hh * hh, axis=1, keepdims=True) + LN_EPS)
        ya.append(jax.nn.sigmoid(hd(OA, h)) * (hn * hsl(prm["norm_a"], h)))
    put_y(0, jnp.concatenate(ya, axis=1))
    buf = old[("buf", 0)]
    row = lax.broadcasted_iota(jnp.int32, (c, BRANCH_W), 0)
    u1 = jnp.where(row == 0, buf[1:2, :], pltpu.roll(u, 1, axis=0))
    u2 = jnp.where(row == 0, buf[0:1, :], jnp.where(row == 1, buf[1:2, :], pltpu.roll(u, 2, axis=0)))
    cw = prm["conv_w"]
    yconv = (u2 * cw[0:1, :] + u1 * cw[1:2, :]) + u * cw[2:3, :]
    put_y(3 * BRANCH_W, zc(BGD, BRANCH_W) * yconv)
    for h in heads:
        oc = o_b[h] - jnp.mean(o_b[h], axis=1, keepdims=True)
        on = oc * lax.rsqrt(jnp.mean(oc * oc, axis=1, keepdims=True) + LN_EPS)
        yb.append(_silu(hd(GB, h)) * (on * hsl(prm["norm_b"], h)))
    put_y(BRANCH_W, jnp.concatenate(yb, axis=1))
    for h in heads:
        on = o_c[h] * lax.rsqrt(jnp.mean(o_c[h] * o_c[h], axis=1, keepdims=True) + LN_EPS)
        yc.append(_silu(hd(GC, h)) * (on * hsl(prm["norm_c"], h)))
    put_y(2 * BRANCH_W, jnp.concatenate(yc, axis=1))


def _chunk_consts(c):
    r = lax.broadcasted_iota(jnp.int32, (c, c), 0)
    l = lax.broadcasted_iota(jnp.int32, (c, c), 1)
    causal = r >= l
    r3 = lax.broadcasted_iota(jnp.int32, (c, 3 * c), 0)
    l3 = lax.broadcasted_iota(jnp.int32, (c, 3 * c), 1)
    l3 = jnp.where(l3 >= 2 * c, l3 - 2 * c, jnp.where(l3 >= c, l3 - c, l3))
    r8 = lax.broadcasted_iota(jnp.int32, (SUBLANES, LANES), 0)
    l8 = lax.broadcasted_iota(jnp.int32, (SUBLANES, LANES), 1)
    return {"causal": causal,
            "tril": jnp.where(r3 >= l3, 1.0, 0.0).astype(BF16),
            "sel8": jnp.where(r8 == l8, 1.0, 0.0).astype(BF16)}


def _layer_lb(lbc, layer):
    e = jnp.exp(lbc - jnp.max(lbc, axis=0, keepdims=True))
    sm = e / jnp.sum(e, axis=0, keepdims=True)
    cum = sm[0:1, :]
    for j in range(1, layer + 1):
        cum = cum + sm[j:j + 1, :]
    return cum - sm[0:1, :]


def _load_params(bias_ref, na_ref, nb_ref, nc_ref, lbc_ref, cw_ref, layer):
    return {"bias_if": bias_ref[...], "norm_a": na_ref[...], "norm_b": nb_ref[...], "norm_c": nc_ref[...],
            "lb": _layer_lb(lbc_ref[...], layer), "conv_w": cw_ref[...]}


PROMPT_LOCKSTEP = 1


def _mixer_prompt_kernel(layer, c, tt, *refs):
    ns = PROMPT_LOCKSTEP
    z_refs = refs[:ns]
    (cos_ref, sin_ref, rdec_ref, rinter_ref, rtoend_ref, rcdec_ref,
     bias_ref, na_ref, nb_ref, nc_ref, lbc_ref, cw_ref, gm_ref, lv_ref,
     y_ref, c_out, n_out, m_out, sr_out, sh_out, buf_out,
     c_s, n_s, m_s, sr_s, sht_s, buf_s) = refs[ns:]
    j = pl.program_id(1)

    @pl.when(j == 0)
    def _():
        c_s[...] = jnp.zeros_like(c_s)
        n_s[...] = jnp.zeros_like(n_s)
        m_s[...] = jnp.zeros_like(m_s)
        sr_s[...] = jnp.zeros_like(sr_s)
        sht_s[...] = jnp.zeros_like(sht_s)
        buf_s[...] = jnp.zeros_like(buf_s)

    prm = _load_params(bias_ref, na_ref, nb_ref, nc_ref, lbc_ref, cw_ref, layer)
    cst = _chunk_consts(c)
    cst["ret_dec"] = lambda h: rdec_ref[h]
    cst["ret_inter"] = rinter_ref[...]
    cst["ret_toend"] = rtoend_ref[...]
    cst["ret_cdec"] = rcdec_ref[...]
    cst["gla_m"] = gm_ref
    cst["gla_lv"] = lv_ref

    def state(s):
        def get(name, h):
            if name == "C":
                return c_s[s, h]
            if name == "n":
                return n_s[s, h:h + 1, :]
            if name == "m":
                return m_s[s, h:h + 1, :]
            if name == "Sr":
                return sr_s[s, h]
            if name == "ShT":
                return sht_s[s, h]
            return buf_s[s, 0:CONV_W - 1, :]

        def put(name, h, val):
            if name == "C":
                c_s[s, h] = val
            elif name == "n":
                n_s[s, h:h + 1, :] = val
            elif name == "m":
                m_s[s, h:h + 1, :] = val
            elif name == "Sr":
                sr_s[s, h] = val
            elif name == "ShT":
                sht_s[s, h] = val
            else:
                buf_s[s, 0:CONV_W - 1, :] = val

        return _State(get, put)

    def chunk(s, k):
        r = k * c
        cc = dict(cst)
        cc["cos2"] = cos_ref[r:r + c, :]
        cc["sin2"] = sin_ref[r:r + c, :]
        zc = lambda off, w: z_refs[s][r:r + c, off:off + w]

        def put_y(off, val):
            y_ref[s, r:r + c, off:off + BRANCH_W] = val.astype(BF16)

        return _mixer_chunk(zc, c, state(s), cc, prm, put_y)

    for s in range(ns):
        _pipelined(chunk(s, k) for k in range(tt // c))

    @pl.when(j == pl.num_programs(1) - 1)
    def _():
        c_out[...] = c_s[...]
        n_out[...] = n_s[:, 0:N_HEADS, :]
        m_out[...] = m_s[:, 0:N_HEADS, :]
        sr_out[...] = sr_s[...]
        for s in range(ns):
            for h in range(N_HEADS):
                sh_out[s, h] = sht_s[s, h].T
        buf_out[...] = buf_s[:, 0:CONV_W - 1, :]


def _const_spec(shape):
    nd = len(shape)
    return pl.BlockSpec(shape, lambda *_: (0,) * nd)


def _mixer_prompt(z, bsz, t, tt, layer, tabs, prms):
    c = math.gcd(t, CHUNK)
    nj = t // tt
    ns = PROMPT_LOCKSTEP
    assert bsz % ns == 0
    cos2, sin2, rdec, rinter, rtoend, rcdec = tabs
    state_shapes = [
        jax.ShapeDtypeStruct((bsz, N_HEADS, HEAD_DIM, HEAD_DIM), F32),
        jax.ShapeDtypeStruct((bsz, N_HEADS, HEAD_DIM), F32),
        jax.ShapeDtypeStruct((bsz, N_HEADS, LANES), F32),
        jax.ShapeDtypeStruct((bsz, N_HEADS, HEAD_DIM, HEAD_DIM), F32),
        jax.ShapeDtypeStruct((bsz, N_HEADS, HEAD_DIM, HEAD_DIM), F32),
        jax.ShapeDtypeStruct((bsz, CONV_W - 1, BRANCH_W), F32),
    ]
    big = pl.BlockSpec((ns, N_HEADS, HEAD_DIM, HEAD_DIM), lambda g, j: (g, 0, 0, 0))
    small = pl.BlockSpec((ns, N_HEADS, LANES), lambda g, j: (g, 0, 0))
    est = (2 * ns * tt * P_PAD * 4 + 2 * ns * tt * Y_W * 2 + 10 * ns * N_HEADS * HEAD_DIM * HEAD_DIM * 4 + (6 << 20))

    def z_spec(s, width):
        return pl.BlockSpec((tt, width), lambda g, j: ((g * ns + s) * nj + j, 0))

    outs = pl.pallas_call(
        functools.partial(_mixer_prompt_kernel, layer, c, tt),
        grid=(bsz // ns, nj),
        in_specs=[z_spec(s, P_PAD) for s in range(ns)]
                 + [pl.BlockSpec((tt, LANES), lambda g, j: (j, 0)),
                    pl.BlockSpec((tt, LANES), lambda g, j: (j, 0)),
                    _const_spec(rdec.shape), _const_spec(rinter.shape), _const_spec(rtoend.shape),
                    _const_spec(rcdec.shape)] + [_const_spec(p.shape) for p in prms],
        out_specs=[pl.BlockSpec((None, ns, tt, Y_W), lambda g, j: (g, 0, j, 0)),
                   big, small, small, big, big,
                   pl.BlockSpec((ns, CONV_W - 1, BRANCH_W), lambda g, j: (g, 0, 0))],
        out_shape=[jax.ShapeDtypeStruct((bsz // ns, ns, t, Y_W), BF16)] + state_shapes,
        scratch_shapes=[pltpu.VMEM((ns, N_HEADS, HEAD_DIM, HEAD_DIM), F32),
                        pltpu.VMEM((ns, SUBLANES, LANES), F32),
                        pltpu.VMEM((ns, SUBLANES, LANES), F32),
                        pltpu.VMEM((ns, N_HEADS, HEAD_DIM, HEAD_DIM), F32),
                        pltpu.VMEM((ns, N_HEADS, HEAD_DIM, HEAD_DIM), F32),
                        pltpu.VMEM((ns, SUBLANES, BRANCH_W), F32)],
        compiler_params=pltpu.CompilerParams(dimension_semantics=("arbitrary", "arbitrary"),
                                             vmem_limit_bytes=_vmem_limit(est)),
        name=f"mixer_prompt_l{layer}",
    )(*([z] * ns), cos2, sin2, rdec, rinter, rtoend, rcdec, *prms)
    return [outs[0].reshape(bsz * t, Y_W)] + list(outs[1:])


N_STATES = 6
N_MIXER_CONST_INPUTS = 15


def _mixer_sample_kernel(layer, c, bb, n_carried, *refs):
    (z_ref, cos_ref, sin_ref, rdec_ref, rinter_ref, rtoend_ref, rcdec_ref,
     bias_ref, na_ref, nb_ref, nc_ref, lbc_ref, cw_ref, gm_ref, lv_ref) = refs[:N_MIXER_CONST_INPUTS]
    c_in, n_in, m_in, sr_in, sh_in, buf_in = refs[N_MIXER_CONST_INPUTS:N_MIXER_CONST_INPUTS + N_STATES]
    y_ref, c_out, n_out, m_out, sr_out, sh_out, buf_out, ybuf = refs[N_MIXER_CONST_INPUTS + N_STATES + n_carried:]
    prm = _load_params(bias_ref, na_ref, nb_ref, nc_ref, lbc_ref, cw_ref, layer)
    cst = _chunk_consts(c)
    cst["ret_dec"] = lambda h: rdec_ref[h]
    cst["ret_inter"] = rinter_ref[...]
    cst["ret_toend"] = rtoend_ref[...]
    cst["ret_cdec"] = rcdec_ref[...]
    cst["gla_m"] = gm_ref
    cst["gla_lv"] = lv_ref
    cst["cos2"] = cos_ref[...]
    cst["sin2"] = sin_ref[...]

    def sequence(i):
        r = i * c
        zc = lambda off, w: z_ref[r:r + c, off:off + w]

        def get(name, h):
            if name == "C":
                return c_in[i, h]
            if name == "n":
                return n_in[i, pl.ds(h, 1), :]
            if name == "m":
                return m_in[i, pl.ds(h, 1), :]
            if name == "Sr":
                return sr_in[i, h]
            if name == "ShT":
                return sh_in[i, h].T
            return buf_in[i]

        def put(name, h, val):
            if name == "C":
                c_out[i, h] = val
            elif name == "n":
                n_out[i, pl.ds(h, 1), :] = val
            elif name == "m":
                m_out[i, pl.ds(h, 1), :] = val
            elif name == "Sr":
                sr_out[i, h] = val
            elif name == "ShT":
                sh_out[i, h] = val.T
            else:
                buf_out[i] = val

        def put_y(off, val):
            ybuf[r:r + c, off:off + BRANCH_W] = val

        return _mixer_chunk(zc, c, _State(get, put), cst, prm, put_y)

    _pipelined(sequence(i) for i in range(bb))
    y_ref[...] = ybuf[...].astype(BF16)


def _mixer_sample(z, row0, bsz, t, bb, layer, tabs, prms, states, carried):
    c = t
    rows = bb * t
    blk0 = row0 // rows
    cos2, sin2, rdec, rinter, rtoend, rcdec = tabs

    def lspec(*tail):
        zeros = (0,) * (len(tail) - 1)
        return pl.BlockSpec((None,) + tail, lambda i: (layer, i) + zeros)

    big = lspec(bb, N_HEADS, HEAD_DIM, HEAD_DIM)
    small = lspec(bb, N_HEADS, LANES)
    state_specs = [big, small, small, big, big, lspec(bb, CONV_W - 1, BRANCH_W)]
    state_shapes = [jax.ShapeDtypeStruct(s.shape, F32) for s in states]
    n_in = N_MIXER_CONST_INPUTS + N_STATES
    est = (2 * rows * P_PAD * 4 + 2 * rows * Y_W * 2 + rows * Y_W * 4
           + 2 * 2 * 3 * bb * N_HEADS * HEAD_DIM * HEAD_DIM * 4)
    return pl.pallas_call(
        functools.partial(_mixer_sample_kernel, layer, c, bb, len(carried)),
        grid=(bsz // bb,),
        in_specs=[pl.BlockSpec((rows, P_PAD), lambda i: (blk0 + i, 0)),
                  _const_spec(cos2.shape), _const_spec(sin2.shape),
                  _const_spec(rdec.shape), _const_spec(rinter.shape), _const_spec(rtoend.shape),
                  _const_spec(rcdec.shape)] + [_const_spec(p.shape) for p in prms] + state_specs
                 + [pl.BlockSpec(memory_space=pl.ANY)] * len(carried),
        out_specs=[pl.BlockSpec((rows, Y_W), lambda i: (i, 0))] + state_specs,
        out_shape=[jax.ShapeDtypeStruct((bsz * t, Y_W), BF16)] + state_shapes,
        input_output_aliases={n_in + k: 1 + k for k in range(len(carried))},
        scratch_shapes=[pltpu.VMEM((rows, Y_W), F32)],
        compiler_params=pltpu.CompilerParams(dimension_semantics=("arbitrary",),
                                             vmem_limit_bytes=_vmem_limit(est)),
        name=f"mixer_sample_l{layer}",
    )(z, cos2, sin2, rdec, rinter, rtoend, rcdec, *prms, *states, *carried)


def _layer_norm(r, g, b):
    mu = jnp.mean(r, axis=1, keepdims=True)
    rc = r - mu
    var = jnp.mean(rc * rc, axis=1, keepdims=True)
    return rc * lax.rsqrt(var + LN_EPS) * g + b


def _merge_kernel(npb, xp_ref, xs_ref, yp_ref, ys_ref, wb_ref, wg_ref, bg_ref, wo_ref, g_ref, b_ref,
                  hp_ref, hs_ref):
    x = _group_load(npb, xp_ref, xs_ref)
    xb = x.astype(BF16)
    y = _group_load(npb, yp_ref, ys_ref)
    acc = None
    for n in range(N_BRANCH):
        proj = jnp.dot(y[:, n * BRANCH_W:(n + 1) * BRANCH_W], wb_ref[n], preferred_element_type=F32)
        gate = jax.nn.sigmoid(jnp.dot(xb, wg_ref[:, n * D_MODEL:(n + 1) * D_MODEL], preferred_element_type=F32)
                              + bg_ref[:, n * D_MODEL:(n + 1) * D_MODEL])
        acc = gate * proj if acc is None else acc + gate * proj
    mix = jnp.dot(acc.astype(BF16), wo_ref[...], preferred_element_type=F32)
    _group_store(npb, hp_ref, hs_ref, _layer_norm(ALPHA * x + mix, g_ref[...], b_ref[...]))


def _layer_spec(layer, *tail):
    return pl.BlockSpec((None,) + tail, lambda *_: (layer,) + (0,) * len(tail))


def _group_out(xp, xs, tm, npb, nsb):
    return dict(out_specs=_group_specs(tm, D_MODEL, npb, nsb),
                out_shape=[jax.ShapeDtypeStruct(xp.shape, F32), jax.ShapeDtypeStruct(xs.shape, F32)])


def _merge(xp, xs, y_p, y_s, layer, wb, wg, bg, wo, g, b, tm):
    npb, nsb = xp.shape[0] // tm, xs.shape[0] // tm
    est = (2 * (N_BRANCH * BRANCH_W * D_MODEL + D_MODEL * N_BRANCH * D_MODEL + D_MODEL * D_MODEL) * 2
           + 2 * 2 * tm * D_MODEL * 4 * 2 + 2 * 2 * tm * Y_W * 2 + 6 * tm * D_MODEL * 4)
    return pl.pallas_call(
        functools.partial(_merge_kernel, npb),
        grid=(npb + nsb,),
        in_specs=_group_specs(tm, D_MODEL, npb, nsb) + _group_specs(tm, Y_W, npb, nsb)
                 + [_layer_spec(layer, N_BRANCH, BRANCH_W, D_MODEL), _layer_spec(layer, D_MODEL, N_BRANCH * D_MODEL),
                    _const_spec(bg.shape), _layer_spec(layer, D_MODEL, D_MODEL),
                    _const_spec(g.shape), _const_spec(b.shape)],
        compiler_params=pltpu.CompilerParams(dimension_semantics=("arbitrary",),
                                             vmem_limit_bytes=_vmem_limit(est)),
        name="merge",
        **_group_out(xp, xs, tm, npb, nsb),
    )(xp, xs, y_p, y_s, wb, wg, bg, wo, g, b)


def _ffn_kernel(npb, hp_ref, hs_ref, wu_ref, wd_ref, g_ref, b_ref, op_ref, os_ref):
    h = _group_load(npb, hp_ref, hs_ref)
    u = jnp.maximum(jnp.dot(h.astype(BF16), wu_ref[...], preferred_element_type=F32), 0.0)
    ff = jnp.dot((u * u).astype(BF16), wd_ref[...], preferred_element_type=F32)
    _group_store(npb, op_ref, os_ref, _layer_norm(ALPHA * h + ff, g_ref[...], b_ref[...]))


def _ffn(hp, hs, layer, wu, wd, g, b, tm):
    npb, nsb = hp.shape[0] // tm, hs.shape[0] // tm
    est = 2 * 2 * D_MODEL * D_FF * 2 + 2 * 2 * 2 * tm * D_MODEL * 4 + tm * D_FF * 6 + 2 * tm * D_MODEL * 4
    return pl.pallas_call(
        functools.partial(_ffn_kernel, npb),
        grid=(npb + nsb,),
        in_specs=_group_specs(tm, D_MODEL, npb, nsb)
                 + [_layer_spec(layer, D_MODEL, D_FF), _layer_spec(layer, D_FF, D_MODEL),
                    _const_spec(g.shape), _const_spec(b.shape)],
        compiler_params=pltpu.CompilerParams(dimension_semantics=("arbitrary",),
                                             vmem_limit_bytes=_vmem_limit(est)),
        name="ffn",
        **_group_out(hp, hs, tm, npb, nsb),
    )(hp, hs, wu, wd, g, b)


def _pad_lanes(a):
    return jnp.pad(a, ((0, 0), (0, LANES - a.shape[1])))


def _position_tables(pos, c):
    half = HEAD_DIM // 2
    inv = ROPE_BASE ** (-jnp.arange(half, dtype=F32) / half)
    ang = pos[:, None] * inv[None, :]
    cos, sin = jnp.cos(ang), jnp.sin(ang)
    cos2 = jnp.concatenate([cos, cos], axis=1)
    sin2 = jnp.concatenate([-sin, sin], axis=1)
    log_gamma = jnp.log(1.0 - 2.0 ** (-5.0 - jnp.arange(N_HEADS, dtype=F32)))
    idx = jnp.arange(c, dtype=F32)
    rel = idx[:, None] - idx[None, :]
    dec = jnp.where(rel >= 0, jnp.exp(log_gamma[:, None, None] * jnp.maximum(rel, 0.0)), 0.0)
    inter = jnp.swapaxes(jnp.exp(log_gamma[:, None] * (idx + 1.0)[None, :]), 0, 1)
    to_end = jnp.swapaxes(jnp.exp(log_gamma[:, None] * (c - 1.0 - idx)[None, :]), 0, 1)
    cdec = jnp.broadcast_to(jnp.exp(log_gamma * c)[:, None], (N_HEADS, LANES))
    cdec = jnp.pad(cdec, ((0, SUBLANES - N_HEADS), (0, 0)))
    return cos2, sin2, dec, _pad_lanes(inter), _pad_lanes(to_end), cdec


def _gla_tables(c):
    r = np.arange(c)[:, None]
    j = np.arange(c)[None, :]
    mats = [j <= r]
    n = c // 2
    while n > 1:
        start = (r // n) * n
        odd = (r // n) % 2 == 1
        mats.append(np.where(odd, (j >= start) & (j <= r), (j > r) & (j <= start + n - 1)))
        n //= 2
    x = np.maximum(r ^ j, 1)
    level = np.where(j > r, -1, np.where(j == r, 0, 1 << np.floor(np.log2(x)).astype(np.int64)))
    return jnp.asarray(np.tile(np.concatenate(mats, 0), (1, 3)), BF16), jnp.asarray(level, jnp.int32)


def kernel(x_prompt, x_sample, state_mlstm_C, state_mlstm_n, state_mlstm_m, state_ret, state_hgrn, state_conv,
           w_in, b_mlstm_gate, norm_a, norm_b, norm_c, lb_c, conv_w, w_branch, w_gate, b_gate, w_out,
           ln1_g, ln1_b, ln2_g, ln2_b, w_up, w_down):
    bp, tp, _ = x_prompt.shape
    bs, ts, _ = x_sample.shape
    n_p, n_s = bp * tp, bs * ts
    tm = 512
    tt = 256
    bb = 8
    assert n_p % tm == 0 and n_s % tm == 0 and tp % tt == 0 and bs % bb == 0 and n_p % (bb * ts) == 0

    xp = x_prompt.reshape(n_p, D_MODEL)
    xs = x_sample.reshape(n_s, D_MODEL)
    tabs_p = _position_tables(jnp.arange(tp, dtype=F32), math.gcd(tp, CHUNK))
    tabs_s = _position_tables(PAST_LEN + jnp.arange(ts, dtype=F32), math.gcd(ts, CHUNK))
    gla_p = _gla_tables(math.gcd(tp, CHUNK))
    gla_s = _gla_tables(math.gcd(ts, CHUNK))
    lbc = lb_c.astype(F32)
    row = lambda a: a.reshape(1, -1).astype(F32)
    assert w_in.shape[1:] == (D_MODEL, P_IN)
    w_in_t = jnp.swapaxes(w_in, 1, 2).astype(F32)
    wb, wg, wo, wu, wd = (w.astype(BF16) for w in (w_branch, w_gate, w_out, w_up, w_down))
    st_in = (state_mlstm_C.astype(F32), state_mlstm_n.astype(F32),
             jnp.broadcast_to(state_mlstm_m.astype(F32)[..., None], (DEPTH, bs, N_HEADS, LANES)),
             state_ret.astype(F32), state_hgrn.astype(F32), state_conv.astype(F32))

    p_states, s_states = [], ()
    for l in range(DEPTH):
        prms = (_pad_lanes(row(b_mlstm_gate[l])), row(norm_a[l]), row(norm_b[l]), row(norm_c[l]), lbc,
                conv_w[l].astype(F32))
        z = _inproj(xp, xs, l, w_in_t, 256)
        outs_p = _mixer_prompt(z, bp, tp, tt, l, tabs_p, prms + gla_p)
        outs_s = _mixer_sample(z, n_p, bs, ts, bb, l, tabs_s, prms + gla_s, st_in, s_states)
        hp, hs = _merge(xp, xs, outs_p[0], outs_s[0], l, wb, wg, row(b_gate[l]), wo, row(ln1_g[l]), row(ln1_b[l]), tm)
        xp, xs = _ffn(hp, hs, l, wu, wd, row(ln2_g[l]), row(ln2_b[l]), tm)
        p_states.append(outs_p[1:])
        s_states = tuple(outs_s[1:])

    cs, ns, ms, srs, shs, bufs = zip(*p_states)
    p_out = (jnp.stack(cs), jnp.stack(ns), jnp.stack([m[..., 0] for m in ms]),
             jnp.stack(srs), jnp.stack(shs), jnp.stack(bufs))
    s_out = s_states[:2] + (s_states[2][..., 0],) + s_states[3:]
    return (xp.reshape(bp, tp, D_MODEL), xs.reshape(bs, ts, D_MODEL)) + p_out + s_out
```

```python
import functools
import math

import jax
import jax.numpy as jnp
import numpy as np
from jax import lax
from jax.experimental import pallas as pl
from jax.experimental.pallas import tpu as pltpu

F32 = jnp.float32
BF16 = jnp.bfloat16

D_MODEL = 1024
DEPTH = 2
BRANCH_W = D_MODEL // 2
N_HEADS = 4
HEAD_DIM = BRANCH_W // N_HEADS
CONV_W = 3
N_BRANCH = 4
D_FF = 4 * D_MODEL
CHUNK = 64
ROPE_BASE = 10000.0
LN_EPS = 1e-5
NEG_BIG = -1e30
GATE_CLAMP = 1.0 - 1e-6
ALPHA = (2 * DEPTH) ** 0.25
PAST_LEN = 16384
K_SCALE = HEAD_DIM ** -0.5

LANES = 128
SUBLANES = 8
VMEM_BYTES = 64 * 1024 * 1024

(QA, KA, VA, OA, QB, KB, VB, GB, QC, FC, IC, GC, XD, BGD, CGD) = [i * BRANCH_W for i in range(15)]
GATE_OFF = 15 * BRANCH_W
P_PAD = GATE_OFF + LANES
Y_W = N_BRANCH * BRANCH_W


def _vmem_limit(est_bytes):
    return int(min(est_bytes + est_bytes // 4 + (4 << 20), VMEM_BYTES - (6 << 20)))


def _dot(a, b):
    return jnp.dot(a.astype(BF16), b.astype(BF16), preferred_element_type=F32)


def _dot_nt(a, b):
    return lax.dot_general(a.astype(BF16), b.astype(BF16), (((1,), (1,)), ((), ())),
                           preferred_element_type=F32)


def _dot_tn(a, b):
    return lax.dot_general(a.astype(BF16), b.astype(BF16), (((0,), (0,)), ((), ())),
                           preferred_element_type=F32)


def _split3(x):
    hi = x.astype(BF16)
    r = x - hi.astype(F32)
    mid = r.astype(BF16)
    lo = (r - mid.astype(F32)).astype(BF16)
    return hi, mid, lo


def _dot_exact_rhs(a01x3, x):
    return jnp.dot(a01x3, _stack3(x), preferred_element_type=F32)


def _stack3(x):
    parts = _split3(x)
    if x.shape[0] % (2 * SUBLANES) == 0:
        return jnp.concatenate(parts, axis=0)
    return jnp.concatenate([p.astype(F32) for p in parts], axis=0).astype(BF16)


def _dot_nt_exact_rhs(a01, x):
    hi, mid, lo = _split3(x)
    d = lambda p: lax.dot_general(a01, p, (((1,), (1,)), ((), ())), preferred_element_type=F32)
    return (d(hi) + d(mid)) + d(lo)


def _silu(x):
    return x * jax.nn.sigmoid(x)


def _group_specs(tm, width, npb, nsb):
    return [pl.BlockSpec((tm, width), lambda i: (jnp.minimum(i, npb - 1), 0)),
            pl.BlockSpec((tm, width), lambda i: (jnp.clip(i - npb, 0, nsb - 1), 0))]


def _group_load(npb, p_ref, s_ref):
    return jnp.where(pl.program_id(0) < npb, p_ref[...], s_ref[...])


def _group_store(npb, p_ref, s_ref, val):
    is_p = pl.program_id(0) < npb

    @pl.when(is_p)
    def _():
        p_ref[...] = val

    @pl.when(jnp.logical_not(is_p))
    def _():
        s_ref[...] = val


P_IN = 15 * BRANCH_W + 2 * N_HEADS
GATE_FEATURES = (4 * BRANCH_W, 4 * BRANCH_W + 2 * N_HEADS)


def _inproj_kernel(npb, xp_ref, xs_ref, wt_ref, z_ref):
    xb = _group_load(npb, xp_ref, xs_ref).astype(BF16)
    nt = lambda w: lax.dot_general(xb, w.astype(BF16), (((1,), (1,)), ((), ())), preferred_element_type=F32)
    g0, g1 = GATE_FEATURES
    for off in range(0, GATE_OFF, BRANCH_W):
        src = off if off < g0 else off + (g1 - g0)
        z_ref[:, off:off + BRANCH_W] = nt(wt_ref[src:src + BRANCH_W, :])
    gates = nt(wt_ref[g0:g1, :])
    z_ref[:, GATE_OFF:P_PAD] = jnp.concatenate(
        [gates, jnp.zeros((gates.shape[0], P_PAD - GATE_OFF - (g1 - g0)), F32)], axis=1)


def _inproj(xp, xs, layer, wt, tm):
    npb, nsb = xp.shape[0] // tm, xs.shape[0] // tm
    est = D_MODEL * P_IN * 4 + 2 * tm * P_PAD * 4 + 2 * 2 * tm * D_MODEL * 4
    return pl.pallas_call(
        functools.partial(_inproj_kernel, npb),
        grid=(npb + nsb,),
        in_specs=_group_specs(tm, D_MODEL, npb, nsb) + [_layer_spec(layer, P_IN, D_MODEL)],
        out_specs=pl.BlockSpec((tm, P_PAD), lambda i: (i, 0)),
        out_shape=jax.ShapeDtypeStruct(((npb + nsb) * tm, P_PAD), F32),
        compiler_params=pltpu.CompilerParams(dimension_semantics=("arbitrary",),
                                             vmem_limit_bytes=_vmem_limit(est)),
        name="inproj",
    )(xp, xs, wt)


class _State:
    def __init__(self, get, put):
        self.get = get
        self.put = put


LEVEL1_DONE, STATE_STORED = "level 1 done", "state stored"


def _advance(g, until=None):
    for label in g:
        if until is not None and label == until:
            return


def _pipelined(chunk_generators):
    pending_tail = None
    for g in chunk_generators:
        _advance(g, LEVEL1_DONE)
        if pending_tail is not None:
            _advance(pending_tail)
        _advance(g, STATE_STORED)
        pending_tail = g
    _advance(pending_tail)


def _mixer_chunk(zc, c, st, cst, prm, put_y):
    tril, causal, sel8 = cst["tril"], cst["causal"], cst["sel8"]
    old = {(name, h): st.get(name, h) for name in ("C", "n", "m", "Sr", "ShT") for h in range(N_HEADS)}
    old[("buf", 0)] = st.get("buf", 0)
    new = {}

    heads = range(N_HEADS)
    hd = lambda off, h: zc(off + h * HEAD_DIM, HEAD_DIM)
    hsl = lambda a, h: a[:, h * HEAD_DIM:(h + 1) * HEAD_DIM]
    cos2, sin2 = cst["cos2"], cst["sin2"]
    rope = lambda x: x * cos2 + pltpu.roll(x, HEAD_DIM // 2, axis=1) * sin2
    lv = cst["gla_lv"][...]

    gp = zc(GATE_OFF, LANES) + prm["bias_if"]
    lf = jnp.minimum(gp, 0.0) - jnp.log1p(jnp.exp(-jnp.abs(gp)))
    b_all = _dot_exact_rhs(tril, lf)
    gp_t = _dot_nt_exact_rhs(sel8, gp)
    s_a, qc_a, s_b, qs_b, a_c = [], [], [], [], []
    for h in heads:
        qa = hd(QA, h)
        s_a.append(_dot_nt(qa, hd(KA, h) * K_SCALE))
        qc_a.append(_dot(qa, old[("C", h)]))
    for h in heads:
        qb = rope(hd(QB, h))
        kb = rope(hd(KB, h)) * K_SCALE
        s_b.append(_dot_nt(qb, kb))
        qs_b.append(_dot(qb, old[("Sr", h)]))
        new[("Sr", h)] = (cst["ret_cdec"][h:h + 1, :] * old[("Sr", h)]
                          + _dot_tn(kb * cst["ret_toend"][:, h:h + 1], hd(VB, h)))
    kc_all = (1.0 - prm["lb"]) * jax.nn.sigmoid(-zc(FC, BRANCH_W))
    g_all = jnp.log1p(-jnp.minimum(kc_all, GATE_CLAMP))
    xs = _dot_exact_rhs(cst["gla_m"][...], g_all)
    bc_all = xs[0:c]
    qg_all = _silu(zc(QC, BRANCH_W))
    for h in heads:
        qg, kc = hsl(qg_all, h), hsl(kc_all, h)
        a = jnp.where(lv == 0, _dot_nt(qg, kc), 0.0)
        a_c.append(jnp.where(lv == 1, _dot_nt(qg * jnp.exp(hsl(g_all, h)), kc), a))

    yield LEVEL1_DONE
    b_t = _dot_nt_exact_rhs(sel8, b_all)
    qn_a = [jnp.sum(hd(QA, h) * old[("n", h)], axis=1, keepdims=True) for h in heads]
    o_b = []
    for h in heads:
        o_b.append(_dot(s_b[h] * cst["ret_dec"](h), hd(VB, h)) + qs_b[h] * cst["ret_inter"][:, h:h + 1])
    num_a, den_a, oi_c = [], [], []
    for h in heads:
        b_col = b_all[:, N_HEADS + h:N_HEADS + h + 1]
        b_row = b_t[N_HEADS + h:N_HEADS + h + 1, :]
        a_col = b_col + old[("m", h)][:, 0:1]
        dmat = jnp.where(causal, (b_col - b_row) + gp_t[h:h + 1, :], NEG_BIG)
        m_t = jnp.maximum(a_col, jnp.max(dmat, axis=1, keepdims=True))
        w_inter = jnp.exp(a_col - m_t)
        s = s_a[h] * jnp.exp(dmat - m_t)
        num_a.append(_dot(s, hd(VA, h)) + qc_a[h] * w_inter)
        den = jnp.sum(s, axis=1, keepdims=True) + qn_a[h] * w_inter
        den_a.append(jnp.maximum(jnp.abs(den), jnp.exp(-m_t)))
    for h in heads:
        b_col = b_all[:, N_HEADS + h:N_HEADS + h + 1]
        m_prev = old[("m", h)][:, 0:1]
        b_last = b_col[c - 1:c, :]
        g_col = (b_last - b_col) + gp[:, h:h + 1]
        m_new = jnp.maximum(b_last + m_prev, jnp.max(g_col, axis=0, keepdims=True))
        dec = jnp.exp(b_last + m_prev - m_new)
        kw = (hd(KA, h) * K_SCALE) * jnp.exp(g_col - m_new)
        new[("C", h)] = dec * old[("C", h)] + _dot_tn(kw, hd(VA, h))
        new[("n", h)] = dec * old[("n", h)] + jnp.sum(kw, axis=0, keepdims=True)
        new[("m", h)] = jnp.broadcast_to(m_new, (1, LANES))
    for h in heads:
        qg, kc, bc = hsl(qg_all, h), hsl(kc_all, h), hsl(bc_all, h)
        a, sec, n = a_c[h], 1, c // 2
        while n > 1:
            e = jnp.exp(hsl(xs[sec * c:(sec + 1) * c], h))
            a = jnp.where(lv == n, _dot_nt(qg * e, kc * e), a)
            sec, n = sec + 1, n // 2
        a_c[h] = a
        sh_t = old[("ShT", h)]
        oi_c.append(_dot_nt(qg * jnp.exp(bc), sh_t))
        bc_last = bc[c - 1:c, :]
        new[("ShT", h)] = jnp.exp(bc_last) * sh_t + _dot_tn(hd(IC, h), kc * jnp.exp(bc_last - bc))
    o_c = [_dot(a_c[h], hd(IC, h)) + oi_c[h] for h in heads]
    u = zc(CGD, BRANCH_W) * zc(XD, BRANCH_W)
    new[("buf", 0)] = u[c - 2:c, :]
    for (name, h), val in new.items():
        st.put(name, h, val)

    yield STATE_STORED
    ya, yb, yc = [], [], []
    for h in heads:
        hh = num_a[h] / den_a[h]
        hn = hh * lax.rsqrt(jnp.mean(hh * hh, axis=1, keepdims=True) + LN_EPS)
        ya.append(jax.nn.sigmoid(hd(OA, h)) * (hn * hsl(prm["norm_a"], h)))
    put_y(0, jnp.concatenate(ya, axis=1))
    buf = old[("buf", 0)]
    row = lax.broadcasted_iota(jnp.int32, (c, BRANCH_W), 0)
    u1 = jnp.where(row == 0, buf[1:2, :], pltpu.roll(u, 1, axis=0))
    u2 = jnp.where(row == 0, buf[0:1, :], jnp.where(row == 1, buf[1:2, :], pltpu.roll(u, 2, axis=0)))
    cw = prm["conv_w"]
    yconv = (u2 * cw[0:1, :] + u1 * cw[1:2, :]) + u * cw[2:3, :]
    put_y(3 * BRANCH_W, zc(BGD, BRANCH_W) * yconv)
    for h in heads:
        oc = o_b[h] - jnp.mean(o_b[h], axis=1, keepdims=True)
        on = oc * lax.rsqrt(jnp.mean(oc * oc, axis=1, keepdims=True) + LN_EPS)
        yb.append(_silu(hd(GB, h)) * (on * hsl(prm["norm_b"], h)))
    put_y(BRANCH_W, jnp.concatenate(yb, axis=1))
    for h in heads:
        on = o_c[h] * lax.rsqrt(jnp.mean(o_c[h] * o_c[h], axis=1, keepdims=True) + LN_EPS)
        yc.append(_silu(hd(GC, h)) * (on * hsl(prm["norm_c"], h)))
    put_y(2 * BRANCH_W, jnp.concatenate(yc, axis=1))


def _chunk_consts(c):
    r = lax.broadcasted_iota(jnp.int32, (c, c), 0)
    l = lax.broadcasted_iota(jnp.int32, (c, c), 1)
    causal = r >= l
    r3 = lax.broadcasted_iota(jnp.int32, (c, 3 * c), 0)
    l3 = lax.broadcasted_iota(jnp.int32, (c, 3 * c), 1)
    l3 = jnp.where(l3 >= 2 * c, l3 - 2 * c, jnp.where(l3 >= c, l3 - c, l3))
    r8 = lax.broadcasted_iota(jnp.int32, (SUBLANES, LANES), 0)
    l8 = lax.broadcasted_iota(jnp.int32, (SUBLANES, LANES), 1)
    return {"causal": causal,
            "tril": jnp.where(r3 >= l3, 1.0, 0.0).astype(BF16),
            "sel8": jnp.where(r8 == l8, 1.0, 0.0).astype(BF16)}


def _layer_lb(lbc, layer):
    e = jnp.exp(lbc - jnp.max(lbc, axis=0, keepdims=True))
    sm = e / jnp.sum(e, axis=0, keepdims=True)
    cum = sm[0:1, :]
    for j in range(1, layer + 1):
        cum = cum + sm[j:j + 1, :]
    return cum - sm[0:1, :]


def _load_params(bias_ref, na_ref, nb_ref, nc_ref, lbc_ref, cw_ref, layer):
    return {"bias_if": bias_ref[...], "norm_a": na_ref[...], "norm_b": nb_ref[...], "norm_c": nc_ref[...],
            "lb": _layer_lb(lbc_ref[...], layer), "conv_w": cw_ref[...]}


PROMPT_LOCKSTEP = 1


def _mixer_prompt_kernel(layer, c, tt, *refs):
    ns = PROMPT_LOCKSTEP
    z_refs = refs[:ns]
    (cos_ref, sin_ref, rdec_ref, rinter_ref, rtoend_ref, rcdec_ref,
     bias_ref, na_ref, nb_ref, nc_ref, lbc_ref, cw_ref, gm_ref, lv_ref,
     y_ref, c_out, n_out, m_out, sr_out, sh_out, buf_out,
     c_s, n_s, m_s, sr_s, sht_s, buf_s) = refs[ns:]
    j = pl.program_id(1)

    @pl.when(j == 0)
    def _():
        c_s[...] = jnp.zeros_like(c_s)
        n_s[...] = jnp.zeros_like(n_s)
        m_s[...] = jnp.zeros_like(m_s)
        sr_s[...] = jnp.zeros_like(sr_s)
        sht_s[...] = jnp.zeros_like(sht_s)
        buf_s[...] = jnp.zeros_like(buf_s)

    prm = _load_params(bias_ref, na_ref, nb_ref, nc_ref, lbc_ref, cw_ref, layer)
    cst = _chunk_consts(c)
    cst["ret_dec"] = lambda h: rdec_ref[h]
    cst["ret_inter"] = rinter_ref[...]
    cst["ret_toend"] = rtoend_ref[...]
    cst["ret_cdec"] = rcdec_ref[...]
    cst["gla_m"] = gm_ref
    cst["gla_lv"] = lv_ref

    def state(s):
        def get(name, h):
            if name == "C":
                return c_s[s, h]
            if name == "n":
                return n_s[s, h:h + 1, :]
            if name == "m":
                return m_s[s, h:h + 1, :]
            if name == "Sr":
                return sr_s[s, h]
            if name == "ShT":
                return sht_s[s, h]
            return buf_s[s, 0:CONV_W - 1, :]

        def put(name, h, val):
            if name == "C":
                c_s[s, h] = val
            elif name == "n":
                n_s[s, h:h + 1, :] = val
            elif name == "m":
                m_s[s, h:h + 1, :] = val
            elif name == "Sr":
                sr_s[s, h] = val
            elif name == "ShT":
                sht_s[s, h] = val
            else:
                buf_s[s, 0:CONV_W - 1, :] = val

        return _State(get, put)

    def chunk(s, k):
        r = k * c
        cc = dict(cst)
        cc["cos2"] = cos_ref[r:r + c, :]
        cc["sin2"] = sin_ref[r:r + c, :]
        zc = lambda off, w: z_refs[s][r:r + c, off:off + w]

        def put_y(off, val):
            y_ref[s, r:r + c, off:off + BRANCH_W] = val.astype(BF16)

        return _mixer_chunk(zc, c, state(s), cc, prm, put_y)

    for s in range(ns):
        _pipelined(chunk(s, k) for k in range(tt // c))

    @pl.when(j == pl.num_programs(1) - 1)
    def _():
        c_out[...] = c_s[...]
        n_out[...] = n_s[:, 0:N_HEADS, :]
        m_out[...] = m_s[:, 0:N_HEADS, :]
        sr_out[...] = sr_s[...]
        for s in range(ns):
            for h in range(N_HEADS):
                sh_out[s, h] = sht_s[s, h].T
        buf_out[...] = buf_s[:, 0:CONV_W - 1, :]


def _const_spec(shape):
    nd = len(shape)
    return pl.BlockSpec(shape, lambda *_: (0,) * nd)


def _mixer_prompt(z, bsz, t, tt, layer, tabs, prms):
    c = math.gcd(t, CHUNK)
    nj = t // tt
    ns = PROMPT_LOCKSTEP
    assert bsz % ns == 0
    cos2, sin2, rdec, rinter, rtoend, rcdec = tabs
    state_shapes = [
        jax.ShapeDtypeStruct((bsz, N_HEADS, HEAD_DIM, HEAD_DIM), F32),
        jax.ShapeDtypeStruct((bsz, N_HEADS, HEAD_DIM), F32),
        jax.ShapeDtypeStruct((bsz, N_HEADS, LANES), F32),
        jax.ShapeDtypeStruct((bsz, N_HEADS, HEAD_DIM, HEAD_DIM), F32),
        jax.ShapeDtypeStruct((bsz, N_HEADS, HEAD_DIM, HEAD_DIM), F32),
        jax.ShapeDtypeStruct((bsz, CONV_W - 1, BRANCH_W), F32),
    ]
    big = pl.BlockSpec((ns, N_HEADS, HEAD_DIM, HEAD_DIM), lambda g, j: (g, 0, 0, 0))
    small = pl.BlockSpec((ns, N_HEADS, LANES), lambda g, j: (g, 0, 0))
    est = (2 * ns * tt * P_PAD * 4 + 2 * ns * tt * Y_W * 2 + 10 * ns * N_HEADS * HEAD_DIM * HEAD_DIM * 4 + (6 << 20))

    def z_spec(s, width):
        return pl.BlockSpec((tt, width), lambda g, j: ((g * ns + s) * nj + j, 0))

    outs = pl.pallas_call(
        functools.partial(_mixer_prompt_kernel, layer, c, tt),
        grid=(bsz // ns, nj),
        in_specs=[z_spec(s, P_PAD) for s in range(ns)]
                 + [pl.BlockSpec((tt, LANES), lambda g, j: (j, 0)),
                    pl.BlockSpec((tt, LANES), lambda g, j: (j, 0)),
                    _const_spec(rdec.shape), _const_spec(rinter.shape), _const_spec(rtoend.shape),
                    _const_spec(rcdec.shape)] + [_const_spec(p.shape) for p in prms],
        out_specs=[pl.BlockSpec((None, ns, tt, Y_W), lambda g, j: (g, 0, j, 0)),
                   big, small, small, big, big,
                   pl.BlockSpec((ns, CONV_W - 1, BRANCH_W), lambda g, j: (g, 0, 0))],
        out_shape=[jax.ShapeDtypeStruct((bsz // ns, ns, t, Y_W), BF16)] + state_shapes,
        scratch_shapes=[pltpu.VMEM((ns, N_HEADS, HEAD_DIM, HEAD_DIM), F32),
                        pltpu.VMEM((ns, SUBLANES, LANES), F32),
                        pltpu.VMEM((ns, SUBLANES, LANES), F32),
                        pltpu.VMEM((ns, N_HEADS, HEAD_DIM, HEAD_DIM), F32),
                        pltpu.VMEM((ns, N_HEADS, HEAD_DIM, HEAD_DIM), F32),
                        pltpu.VMEM((ns, SUBLANES, BRANCH_W), F32)],
        compiler_params=pltpu.CompilerParams(dimension_semantics=("arbitrary", "arbitrary"),
                                             vmem_limit_bytes=_vmem_limit(est)),
        name=f"mixer_prompt_l{layer}",
    )(*([z] * ns), cos2, sin2, rdec, rinter, rtoend, rcdec, *prms)
    return [outs[0].reshape(bsz * t, Y_W)] + list(outs[1:])


N_STATES = 6
N_MIXER_CONST_INPUTS = 15


def _mixer_sample_kernel(layer, c, bb, n_carried, *refs):
    (z_ref, cos_ref, sin_ref, rdec_ref, rinter_ref, rtoend_ref, rcdec_ref,
     bias_ref, na_ref, nb_ref, nc_ref, lbc_ref, cw_ref, gm_ref, lv_ref) = refs[:N_MIXER_CONST_INPUTS]
    c_in, n_in, m_in, sr_in, sh_in, buf_in = refs[N_MIXER_CONST_INPUTS:N_MIXER_CONST_INPUTS + N_STATES]
    y_ref, c_out, n_out, m_out, sr_out, sh_out, buf_out, ybuf = refs[N_MIXER_CONST_INPUTS + N_STATES + n_carried:]
    prm = _load_params(bias_ref, na_ref, nb_ref, nc_ref, lbc_ref, cw_ref, layer)
    cst = _chunk_consts(c)
    cst["ret_dec"] = lambda h: rdec_ref[h]
    cst["ret_inter"] = rinter_ref[...]
    cst["ret_toend"] = rtoend_ref[...]
    cst["ret_cdec"] = rcdec_ref[...]
    cst["gla_m"] = gm_ref
    cst["gla_lv"] = lv_ref
    cst["cos2"] = cos_ref[...]
    cst["sin2"] = sin_ref[...]

    def sequence(i):
        r = i * c
        zc = lambda off, w: z_ref[r:r + c, off:off + w]

        def get(name, h):
            if name == "C":
                return c_in[i, h]
            if name == "n":
                return n_in[i, pl.ds(h, 1), :]
            if name == "m":
                return m_in[i, pl.ds(h, 1), :]
            if name == "Sr":
                return sr_in[i, h]
            if name == "ShT":
                return sh_in[i, h].T
            return buf_in[i]

        def put(name, h, val):
            if name == "C":
                c_out[i, h] = val
            elif name == "n":
                n_out[i, pl.ds(h, 1), :] = val
            elif name == "m":
                m_out[i, pl.ds(h, 1), :] = val
            elif name == "Sr":
                sr_out[i, h] = val
            elif name == "ShT":
                sh_out[i, h] = val.T
            else:
                buf_out[i] = val

        def put_y(off, val):
            ybuf[r:r + c, off:off + BRANCH_W] = val

        return _mixer_chunk(zc, c, _State(get, put), cst, prm, put_y)

    _pipelined(sequence(i) for i in range(bb))
    y_ref[...] = ybuf[...].astype(BF16)


def _mixer_sample(z, row0, bsz, t, bb, layer, tabs, prms, states, carried):
    c = t
    rows = bb * t
    blk0 = row0 // rows
    cos2, sin2, rdec, rinter, rtoend, rcdec = tabs

    def lspec(*tail):
        zeros = (0,) * (len(tail) - 1)
        return pl.BlockSpec((None,) + tail, lambda i: (layer, i) + zeros)

    big = lspec(bb, N_HEADS, HEAD_DIM, HEAD_DIM)
    small = lspec(bb, N_HEADS, LANES)
    state_specs = [big, small, small, big, big, lspec(bb, CONV_W - 1, BRANCH_W)]
    state_shapes = [jax.ShapeDtypeStruct(s.shape, F32) for s in states]
    n_in = N_MIXER_CONST_INPUTS + N_STATES
    est = (2 * rows * P_PAD * 4 + 2 * rows * Y_W * 2 + rows * Y_W * 4
           + 2 * 2 * 3 * bb * N_HEADS * HEAD_DIM * HEAD_DIM * 4)
    return pl.pallas_call(
        functools.partial(_mixer_sample_kernel, layer, c, bb, len(carried)),
        grid=(bsz // bb,),
        in_specs=[pl.BlockSpec((rows, P_PAD), lambda i: (blk0 + i, 0)),
                  _const_spec(cos2.shape), _const_spec(sin2.shape),
                  _const_spec(rdec.shape), _const_spec(rinter.shape), _const_spec(rtoend.shape),
                  _const_spec(rcdec.shape)] + [_const_spec(p.shape) for p in prms] + state_specs
                 + [pl.BlockSpec(memory_space=pl.ANY)] * len(carried),
        out_specs=[pl.BlockSpec((rows, Y_W), lambda i: (i, 0))] + state_specs,
        out_shape=[jax.ShapeDtypeStruct((bsz * t, Y_W), BF16)] + state_shapes,
        input_output_aliases={n_in + k: 1 + k for k in range(len(carried))},
        scratch_shapes=[pltpu.VMEM((rows, Y_W), F32)],
        compiler_params=pltpu.CompilerParams(dimension_semantics=("arbitrary",),
                                             vmem_limit_bytes=_vmem_limit(est)),
        name=f"mixer_sample_l{layer}",
    )(z, cos2, sin2, rdec, rinter, rtoend, rcdec, *prms, *states, *carried)


def _layer_norm(r, g, b):
    mu = jnp.mean(r, axis=1, keepdims=True)
    rc = r - mu
    var = jnp.mean(rc * rc, axis=1, keepdims=True)
    return rc * lax.rsqrt(var + LN_EPS) * g + b


def _merge_kernel(npb, xp_ref, xs_ref, yp_ref, ys_ref, wb_ref, wg_ref, bg_ref, wo_ref, g_ref, b_ref,
                  hp_ref, hs_ref):
    x = _group_load(npb, xp_ref, xs_ref)
    xb = x.astype(BF16)
    y = _group_load(npb, yp_ref, ys_ref)
    acc = None
    for n in range(N_BRANCH):
        proj = jnp.dot(y[:, n * BRANCH_W:(n + 1) * BRANCH_W], wb_ref[n].astype(BF16), preferred_element_type=F32)
        gate = jax.nn.sigmoid(jnp.dot(xb, wg_ref[:, n * D_MODEL:(n + 1) * D_MODEL].astype(BF16),
                                      preferred_element_type=F32)
                              + bg_ref[:, n * D_MODEL:(n + 1) * D_MODEL])
        acc = gate * proj if acc is None else acc + gate * proj
    mix = jnp.dot(acc.astype(BF16), wo_ref[...].astype(BF16), preferred_element_type=F32)
    _group_store(npb, hp_ref, hs_ref, _layer_norm(ALPHA * x + mix, g_ref[...], b_ref[...]))


def _layer_spec(layer, *tail):
    return pl.BlockSpec((None,) + tail, lambda *_: (layer,) + (0,) * len(tail))


def _group_out(xp, xs, tm, npb, nsb):
    return dict(out_specs=_group_specs(tm, D_MODEL, npb, nsb),
                out_shape=[jax.ShapeDtypeStruct(xp.shape, F32), jax.ShapeDtypeStruct(xs.shape, F32)])


def _merge(xp, xs, y_p, y_s, layer, wb, wg, bg, wo, g, b, tm):
    npb, nsb = xp.shape[0] // tm, xs.shape[0] // tm
    est = ((N_BRANCH * BRANCH_W * D_MODEL + D_MODEL * N_BRANCH * D_MODEL + D_MODEL * D_MODEL) * 4
           + 2 * 2 * tm * D_MODEL * 4 * 2 + 2 * 2 * tm * Y_W * 2 + 8 * tm * D_MODEL * 4)
    return pl.pallas_call(
        functools.partial(_merge_kernel, npb),
        grid=(npb + nsb,),
        in_specs=_group_specs(tm, D_MODEL, npb, nsb) + _group_specs(tm, Y_W, npb, nsb)
                 + [_layer_spec(layer, N_BRANCH, BRANCH_W, D_MODEL), _layer_spec(layer, D_MODEL, N_BRANCH * D_MODEL),
                    _const_spec(bg.shape), _layer_spec(layer, D_MODEL, D_MODEL),
                    _const_spec(g.shape), _const_spec(b.shape)],
        compiler_params=pltpu.CompilerParams(dimension_semantics=("arbitrary",),
                                             vmem_limit_bytes=_vmem_limit(est)),
        name="merge",
        **_group_out(xp, xs, tm, npb, nsb),
    )(xp, xs, y_p, y_s, wb, wg, bg, wo, g, b)


FFN_CHUNK = 1024


def _ffn_kernel(npb, hp_ref, hs_ref, wu_ref, wd_ref, g_ref, b_ref, op_ref, os_ref):
    h = _group_load(npb, hp_ref, hs_ref)
    hb = h.astype(BF16)
    ff = None
    for j in range(0, D_FF, FFN_CHUNK):
        u = jnp.maximum(jnp.dot(hb, wu_ref[:, j:j + FFN_CHUNK].astype(BF16), preferred_element_type=F32), 0.0)
        part = jnp.dot((u * u).astype(BF16), wd_ref[j:j + FFN_CHUNK, :].astype(BF16), preferred_element_type=F32)
        ff = part if ff is None else ff + part
    _group_store(npb, op_ref, os_ref, _layer_norm(ALPHA * h + ff, g_ref[...], b_ref[...]))


def _ffn(hp, hs, layer, wu, wd, g, b, tm):
    npb, nsb = hp.shape[0] // tm, hs.shape[0] // tm
    est = (2 * D_MODEL * D_FF * 4 + 2 * 2 * 2 * tm * D_MODEL * 4 + tm * FFN_CHUNK * 6 + 4 * tm * D_MODEL * 4
           + 2 * D_MODEL * FFN_CHUNK * 2)
    return pl.pallas_call(
        functools.partial(_ffn_kernel, npb),
        grid=(npb + nsb,),
        in_specs=_group_specs(tm, D_MODEL, npb, nsb)
                 + [_layer_spec(layer, D_MODEL, D_FF), _layer_spec(layer, D_FF, D_MODEL),
                    _const_spec(g.shape), _const_spec(b.shape)],
        compiler_params=pltpu.CompilerParams(dimension_semantics=("arbitrary",),
                                             vmem_limit_bytes=_vmem_limit(est)),
        name="ffn",
        **_group_out(hp, hs, tm, npb, nsb),
    )(hp, hs, wu, wd, g, b)


def _pad_lanes(a):
    return jnp.pad(a, ((0, 0), (0, LANES - a.shape[1])))


def _position_tables(pos, c):
    half = HEAD_DIM // 2
    inv = ROPE_BASE ** (-jnp.arange(half, dtype=F32) / half)
    ang = pos[:, None] * inv[None, :]
    cos, sin = jnp.cos(ang), jnp.sin(ang)
    cos2 = jnp.concatenate([cos, cos], axis=1)
    sin2 = jnp.concatenate([-sin, sin], axis=1)
    log_gamma = jnp.log(1.0 - 2.0 ** (-5.0 - jnp.arange(N_HEADS, dtype=F32)))
    idx = jnp.arange(c, dtype=F32)
    rel = idx[:, None] - idx[None, :]
    dec = jnp.where(rel >= 0, jnp.exp(log_gamma[:, None, None] * jnp.maximum(rel, 0.0)), 0.0)
    inter = jnp.swapaxes(jnp.exp(log_gamma[:, None] * (idx + 1.0)[None, :]), 0, 1)
    to_end = jnp.swapaxes(jnp.exp(log_gamma[:, None] * (c - 1.0 - idx)[None, :]), 0, 1)
    cdec = jnp.broadcast_to(jnp.exp(log_gamma * c)[:, None], (N_HEADS, LANES))
    cdec = jnp.pad(cdec, ((0, SUBLANES - N_HEADS), (0, 0)))
    return cos2, sin2, dec, _pad_lanes(inter), _pad_lanes(to_end), cdec


def _gla_tables(c):
    r = np.arange(c)[:, None]
    j = np.arange(c)[None, :]
    mats = [j <= r]
    n = c // 2
    while n > 1:
        start = (r // n) * n
        odd = (r // n) % 2 == 1
        mats.append(np.where(odd, (j >= start) & (j <= r), (j > r) & (j <= start + n - 1)))
        n //= 2
    x = np.maximum(r ^ j, 1)
    level = np.where(j > r, -1, np.where(j == r, 0, 1 << np.floor(np.log2(x)).astype(np.int64)))
    return jnp.asarray(np.tile(np.concatenate(mats, 0), (1, 3)), BF16), jnp.asarray(level, jnp.int32)


def kernel(x_prompt, x_sample, state_mlstm_C, state_mlstm_n, state_mlstm_m, state_ret, state_hgrn, state_conv,
           w_in, b_mlstm_gate, norm_a, norm_b, norm_c, lb_c, conv_w, w_branch, w_gate, b_gate, w_out,
           ln1_g, ln1_b, ln2_g, ln2_b, w_up, w_down):
    bp, tp, _ = x_prompt.shape
    bs, ts, _ = x_sample.shape
    n_p, n_s = bp * tp, bs * ts
    tm = 512
    tt = 256
    bb = 8
    assert n_p % tm == 0 and n_s % tm == 0 and tp % tt == 0 and bs % bb == 0 and n_p % (bb * ts) == 0

    xp = x_prompt.reshape(n_p, D_MODEL)
    xs = x_sample.reshape(n_s, D_MODEL)
    tabs_p = _position_tables(jnp.arange(tp, dtype=F32), math.gcd(tp, CHUNK))
    tabs_s = _position_tables(PAST_LEN + jnp.arange(ts, dtype=F32), math.gcd(ts, CHUNK))
    gla_p = _gla_tables(math.gcd(tp, CHUNK))
    gla_s = _gla_tables(math.gcd(ts, CHUNK))
    lbc = lb_c.astype(F32)
    row = lambda a: a.reshape(1, -1).astype(F32)
    assert w_in.shape[1:] == (D_MODEL, P_IN)
    w_in_t = jnp.swapaxes(w_in, 1, 2).astype(F32)
    wb, wg, wo, wu, wd = (w.astype(F32) for w in (w_branch, w_gate, w_out, w_up, w_down))
    st_in = (state_mlstm_C.astype(F32), state_mlstm_n.astype(F32),
             jnp.broadcast_to(state_mlstm_m.astype(F32)[..., None], (DEPTH, bs, N_HEADS, LANES)),
             state_ret.astype(F32), state_hgrn.astype(F32), state_conv.astype(F32))

    p_states, s_states = [], ()
    for l in range(DEPTH):
        prms = (_pad_lanes(row(b_mlstm_gate[l])), row(norm_a[l]), row(norm_b[l]), row(norm_c[l]), lbc,
                conv_w[l].astype(F32))
        z = _inproj(xp, xs, l, w_in_t, 256)
        outs_p = _mixer_prompt(z, bp, tp, tt, l, tabs_p, prms + gla_p)
        outs_s = _mixer_sample(z, n_p, bs, ts, bb, l, tabs_s, prms + gla_s, st_in, s_states)
        hp, hs = _merge(xp, xs, outs_p[0], outs_s[0], l, wb, wg, row(b_gate[l]), wo, row(ln1_g[l]), row(ln1_b[l]), 256)
        xp, xs = _ffn(hp, hs, l, wu, wd, row(ln2_g[l]), row(ln2_b[l]), 256)
        p_states.append(outs_p[1:])
        s_states = tuple(outs_s[1:])

    cs, ns, ms, srs, shs, bufs = zip(*p_states)
    p_out = (jnp.stack(cs), jnp.stack(ns), jnp.stack([m[..., 0] for m in ms]),
             jnp.stack(srs), jnp.stack(shs), jnp.stack(bufs))
    s_out = s_states[:2] + (s_states[2][..., 0],) + s_states[3:]
    return (xp.reshape(bp, tp, D_MODEL), xs.reshape(bs, ts, D_MODEL)) + p_out + s_out
```

```python
import functools
import math

import jax
import jax.numpy as jnp
import numpy as np
from jax import lax
from jax.experimental import pallas as pl
from jax.experimental.pallas import tpu as pltpu

F32 = jnp.float32
BF16 = jnp.bfloat16

D_MODEL = 1024
DEPTH = 2
BRANCH_W = D_MODEL // 2
N_HEADS = 4
HEAD_DIM = BRANCH_W // N_HEADS
CONV_W = 3
N_BRANCH = 4
D_FF = 4 * D_MODEL
CHUNK = 64
ROPE_BASE = 10000.0
LN_EPS = 1e-5
NEG_BIG = -1e30
GATE_CLAMP = 1.0 - 1e-6
ALPHA = (2 * DEPTH) ** 0.25
PAST_LEN = 16384
K_SCALE = HEAD_DIM ** -0.5

LANES = 128
SUBLANES = 8
VMEM_BYTES = 64 * 1024 * 1024

(QA, KA, VA, OA, QB, KB, VB, GB, QC, FC, IC, GC, XD, BGD, CGD) = [i * BRANCH_W for i in range(15)]
GATE_OFF = 15 * BRANCH_W
P_PAD = GATE_OFF + LANES
Y_W = N_BRANCH * BRANCH_W


def _vmem_limit(est_bytes):
    return int(min(est_bytes + est_bytes // 4 + (4 << 20), VMEM_BYTES - (6 << 20)))


def _dot(a, b):
    return jnp.dot(a.astype(BF16), b.astype(BF16), preferred_element_type=F32)


def _dot_nt(a, b):
    return lax.dot_general(a.astype(BF16), b.astype(BF16), (((1,), (1,)), ((), ())),
                           preferred_element_type=F32)


def _dot_tn(a, b):
    return lax.dot_general(a.astype(BF16), b.astype(BF16), (((0,), (0,)), ((), ())),
                           preferred_element_type=F32)


def _split3(x):
    hi = x.astype(BF16)
    r = x - hi.astype(F32)
    mid = r.astype(BF16)
    lo = (r - mid.astype(F32)).astype(BF16)
    return hi, mid, lo


def _dot_exact_rhs(a01x3, x):
    return jnp.dot(a01x3, _stack3(x), preferred_element_type=F32)


def _stack3(x):
    parts = _split3(x)
    if x.shape[0] % (2 * SUBLANES) == 0:
        return jnp.concatenate(parts, axis=0)
    return jnp.concatenate([p.astype(F32) for p in parts], axis=0).astype(BF16)


def _dot_nt_exact_rhs(a01, x):
    hi, mid, lo = _split3(x)
    d = lambda p: lax.dot_general(a01, p, (((1,), (1,)), ((), ())), preferred_element_type=F32)
    return (d(hi) + d(mid)) + d(lo)


def _silu(x):
    return x * jax.nn.sigmoid(x)


def _row_sum(x, c):
    if c <= SUBLANES:
        return jnp.sum(x, axis=1, keepdims=True)
    hi = x.astype(BF16)
    lo = (x - hi.astype(F32)).astype(BF16)
    ones = jnp.ones((2 * x.shape[1], LANES), BF16)
    return jnp.dot(jnp.concatenate([hi, lo], axis=1), ones, preferred_element_type=F32)


def _group_specs(tm, width, npb, nsb):
    return [pl.BlockSpec((tm, width), lambda i: (jnp.minimum(i, npb - 1), 0)),
            pl.BlockSpec((tm, width), lambda i: (jnp.clip(i - npb, 0, nsb - 1), 0))]


def _group_load(npb, p_ref, s_ref):
    return jnp.where(pl.program_id(0) < npb, p_ref[...], s_ref[...])


def _group_store(npb, p_ref, s_ref, val):
    is_p = pl.program_id(0) < npb

    @pl.when(is_p)
    def _():
        p_ref[...] = val

    @pl.when(jnp.logical_not(is_p))
    def _():
        s_ref[...] = val


P_IN = 15 * BRANCH_W + 2 * N_HEADS
GATE_FEATURES = (4 * BRANCH_W, 4 * BRANCH_W + 2 * N_HEADS)


def _inproj_kernel(npb, xp_ref, xs_ref, wt_ref, z_ref):
    xb = _group_load(npb, xp_ref, xs_ref).astype(BF16)
    nt = lambda w: lax.dot_general(xb, w.astype(BF16), (((1,), (1,)), ((), ())), preferred_element_type=F32)
    g0, g1 = GATE_FEATURES
    for off in range(0, GATE_OFF, BRANCH_W):
        src = off if off < g0 else off + (g1 - g0)
        z_ref[:, off:off + BRANCH_W] = nt(wt_ref[src:src + BRANCH_W, :])
    gates = nt(wt_ref[g0:g1, :])
    z_ref[:, GATE_OFF:P_PAD] = jnp.concatenate(
        [gates, jnp.zeros((gates.shape[0], P_PAD - GATE_OFF - (g1 - g0)), F32)], axis=1)


def _inproj(xp, xs, layer, wt, tm):
    npb, nsb = xp.shape[0] // tm, xs.shape[0] // tm
    est = D_MODEL * P_IN * 4 + 2 * tm * P_PAD * 4 + 2 * 2 * tm * D_MODEL * 4
    return pl.pallas_call(
        functools.partial(_inproj_kernel, npb),
        grid=(npb + nsb,),
        in_specs=_group_specs(tm, D_MODEL, npb, nsb) + [_layer_spec(layer, P_IN, D_MODEL)],
        out_specs=pl.BlockSpec((tm, P_PAD), lambda i: (i, 0)),
        out_shape=jax.ShapeDtypeStruct(((npb + nsb) * tm, P_PAD), F32),
        compiler_params=pltpu.CompilerParams(dimension_semantics=("arbitrary",),
                                             vmem_limit_bytes=_vmem_limit(est)),
        name="inproj",
    )(xp, xs, wt)


class _State:
    def __init__(self, get, put):
        self.get = get
        self.put = put


LEVEL1_DONE, STATE_STORED = "level 1 done", "state stored"


def _advance(g, until=None):
    for label in g:
        if until is not None and label == until:
            return


def _pipelined(chunk_generators):
    pending_tail = None
    for g in chunk_generators:
        _advance(g, LEVEL1_DONE)
        if pending_tail is not None:
            _advance(pending_tail)
        _advance(g, STATE_STORED)
        pending_tail = g
    _advance(pending_tail)


def _mixer_chunk(zc, c, st, cst, prm, put_y):
    tril, causal, sel8 = cst["tril"], cst["causal"], cst["sel8"]
    old = {(name, h): st.get(name, h) for name in ("C", "n", "m", "Sr", "ShT") for h in range(N_HEADS)}
    old[("buf", 0)] = st.get("buf", 0)
    new = {}

    heads = range(N_HEADS)
    hd = lambda off, h: zc(off + h * HEAD_DIM, HEAD_DIM)
    hsl = lambda a, h: a[:, h * HEAD_DIM:(h + 1) * HEAD_DIM]
    cos2, sin2 = cst["cos2"], cst["sin2"]
    rope = lambda x: x * cos2 + pltpu.roll(x, HEAD_DIM // 2, axis=1) * sin2
    lv = cst["gla_lv"][...]

    gp = zc(GATE_OFF, LANES) + prm["bias_if"]
    lf = jnp.minimum(gp, 0.0) - jnp.log1p(jnp.exp(-jnp.abs(gp)))
    b_all = _dot_exact_rhs(tril, lf)
    gp_t = _dot_nt_exact_rhs(sel8, gp)
    s_a, qc_a, s_b, qs_b, a_c = [], [], [], [], []
    for h in heads:
        qa = hd(QA, h)
        s_a.append(_dot_nt(qa, hd(KA, h) * K_SCALE))
        qc_a.append(_dot(qa, old[("C", h)]))
    for h in heads:
        qb = rope(hd(QB, h))
        kb = rope(hd(KB, h)) * K_SCALE
        s_b.append(_dot_nt(qb, kb))
        qs_b.append(_dot(qb, old[("Sr", h)]))
        new[("Sr", h)] = (cst["ret_cdec"][h:h + 1, :] * old[("Sr", h)]
                          + _dot_tn(kb * cst["ret_toend"](h), hd(VB, h)))
    kc_all = (1.0 - prm["lb"]) * jax.nn.sigmoid(-zc(FC, BRANCH_W))
    g_all = jnp.log1p(-jnp.minimum(kc_all, GATE_CLAMP))
    xs = _dot_exact_rhs(cst["gla_m"][...], g_all)
    bc_all = xs[0:c]
    qg_all = _silu(zc(QC, BRANCH_W))
    for h in heads:
        qg, kc = hsl(qg_all, h), hsl(kc_all, h)
        a = jnp.where(lv == 0, _dot_nt(qg, kc), 0.0)
        a_c.append(jnp.where(lv == 1, _dot_nt(qg * jnp.exp(hsl(g_all, h)), kc), a))

    yield LEVEL1_DONE
    b_t = _dot_nt_exact_rhs(sel8, b_all)
    qn_a = [jnp.sum(hd(QA, h) * old[("n", h)], axis=1, keepdims=True) for h in heads]
    o_b = []
    for h in heads:
        o_b.append(_dot(s_b[h] * cst["ret_dec"](h), hd(VB, h)) + qs_b[h] * cst["ret_inter"](h))
    num_a, den_a, oi_c = [], [], []
    for h in heads:
        b_col = b_all[:, N_HEADS + h:N_HEADS + h + 1]
        b_row = b_t[N_HEADS + h:N_HEADS + h + 1, :]
        a_col = b_col + old[("m", h)][:, 0:1]
        dmat = jnp.where(causal, (b_col - b_row) + gp_t[h:h + 1, :], NEG_BIG)
        m_t = jnp.maximum(a_col, jnp.max(dmat, axis=1, keepdims=True))
        w_inter = jnp.exp(a_col - m_t)
        s = s_a[h] * jnp.exp(dmat - m_t)
        num_a.append(_dot(s, hd(VA, h)) + qc_a[h] * w_inter)
        den = jnp.sum(s, axis=1, keepdims=True) + qn_a[h] * w_inter
        den_a.append(jnp.maximum(jnp.abs(den), jnp.exp(-m_t)))
    for h in heads:
        b_col = b_all[:, N_HEADS + h:N_HEADS + h + 1]
        m_prev = old[("m", h)][:, 0:1]
        b_last = b_col[c - 1:c, :]
        g_col = (b_last - b_col) + gp[:, h:h + 1]
        m_new = jnp.maximum(b_last + m_prev, jnp.max(g_col, axis=0, keepdims=True))
        dec = jnp.exp(b_last + m_prev - m_new)
        kw = (hd(KA, h) * K_SCALE) * jnp.exp(g_col - m_new)
        new[("C", h)] = dec * old[("C", h)] + _dot_tn(kw, hd(VA, h))
        new[("n", h)] = dec * old[("n", h)] + jnp.sum(kw, axis=0, keepdims=True)
        new[("m", h)] = jnp.broadcast_to(m_new, (1, LANES))
    for h in heads:
        qg, kc, bc = hsl(qg_all, h), hsl(kc_all, h), hsl(bc_all, h)
        a, sec, n = a_c[h], 1, c // 2
        while n > 1:
            e = jnp.exp(hsl(xs[sec * c:(sec + 1) * c], h))
            a = jnp.where(lv == n, _dot_nt(qg * e, kc * e), a)
            sec, n = sec + 1, n // 2
        a_c[h] = a
        sh_t = old[("ShT", h)]
        oi_c.append(_dot_nt(qg * jnp.exp(bc), sh_t))
        bc_last = bc[c - 1:c, :]
        new[("ShT", h)] = jnp.exp(bc_last) * sh_t + _dot_tn(hd(IC, h), kc * jnp.exp(bc_last - bc))
    o_c = [_dot(a_c[h], hd(IC, h)) + oi_c[h] for h in heads]
    u = zc(CGD, BRANCH_W) * zc(XD, BRANCH_W)
    new[("buf", 0)] = u[c - 2:c, :]
    for (name, h), val in new.items():
        st.put(name, h, val)

    yield STATE_STORED
    ya, yb, yc = [], [], []
    lane_mean = lambda x: _row_sum(x, c) * (1.0 / x.shape[1])
    for h in heads:
        hh = num_a[h] / den_a[h]
        hn = hh * lax.rsqrt(lane_mean(hh * hh) + LN_EPS)
        ya.append(jax.nn.sigmoid(hd(OA, h)) * (hn * hsl(prm["norm_a"], h)))
    put_y(0, jnp.concatenate(ya, axis=1))
    buf = old[("buf", 0)]
    row = lax.broadcasted_iota(jnp.int32, (c, BRANCH_W), 0)
    u1 = jnp.where(row == 0, buf[1:2, :], pltpu.roll(u, 1, axis=0))
    u2 = jnp.where(row == 0, buf[0:1, :], jnp.where(row == 1, buf[1:2, :], pltpu.roll(u, 2, axis=0)))
    cw = prm["conv_w"]
    yconv = (u2 * cw[0:1, :] + u1 * cw[1:2, :]) + u * cw[2:3, :]
    put_y(3 * BRANCH_W, zc(BGD, BRANCH_W) * yconv)
    for h in heads:
        oc = o_b[h] - lane_mean(o_b[h])
        on = oc * lax.rsqrt(lane_mean(oc * oc) + LN_EPS)
        yb.append(_silu(hd(GB, h)) * (on * hsl(prm["norm_b"], h)))
    put_y(BRANCH_W, jnp.concatenate(yb, axis=1))
    for h in heads:
        on = o_c[h] * lax.rsqrt(lane_mean(o_c[h] * o_c[h]) + LN_EPS)
        yc.append(_silu(hd(GC, h)) * (on * hsl(prm["norm_c"], h)))
    put_y(2 * BRANCH_W, jnp.concatenate(yc, axis=1))


def _chunk_consts(c):
    r = lax.broadcasted_iota(jnp.int32, (c, c), 0)
    l = lax.broadcasted_iota(jnp.int32, (c, c), 1)
    causal = r >= l
    r3 = lax.broadcasted_iota(jnp.int32, (c, 3 * c), 0)
    l3 = lax.broadcasted_iota(jnp.int32, (c, 3 * c), 1)
    l3 = jnp.where(l3 >= 2 * c, l3 - 2 * c, jnp.where(l3 >= c, l3 - c, l3))
    r8 = lax.broadcasted_iota(jnp.int32, (SUBLANES, LANES), 0)
    l8 = lax.broadcasted_iota(jnp.int32, (SUBLANES, LANES), 1)
    return {"causal": causal,
            "tril": jnp.where(r3 >= l3, 1.0, 0.0).astype(BF16),
            "sel8": jnp.where(r8 == l8, 1.0, 0.0).astype(BF16)}


def _layer_lb(lbc, layer):
    e = jnp.exp(lbc - jnp.max(lbc, axis=0, keepdims=True))
    sm = e / jnp.sum(e, axis=0, keepdims=True)
    cum = sm[0:1, :]
    for j in range(1, layer + 1):
        cum = cum + sm[j:j + 1, :]
    return cum - sm[0:1, :]


def _load_params(bias_ref, na_ref, nb_ref, nc_ref, lbc_ref, cw_ref, layer):
    return {"bias_if": bias_ref[...], "norm_a": na_ref[...], "norm_b": nb_ref[...], "norm_c": nc_ref[...],
            "lb": _layer_lb(lbc_ref[...], layer), "conv_w": cw_ref[...]}


PROMPT_LOCKSTEP = 1


def _mixer_prompt_kernel(layer, c, tt, *refs):
    ns = PROMPT_LOCKSTEP
    z_refs = refs[:ns]
    (cos_ref, sin_ref, rdec_ref, rinter_ref, rtoend_ref, rcdec_ref,
     bias_ref, na_ref, nb_ref, nc_ref, lbc_ref, cw_ref, gm_ref, lv_ref,
     y_ref, c_out, n_out, m_out, sr_out, sh_out, buf_out,
     c_s, n_s, m_s, sr_s, sht_s, buf_s) = refs[ns:]
    j = pl.program_id(1)

    @pl.when(j == 0)
    def _():
        c_s[...] = jnp.zeros_like(c_s)
        n_s[...] = jnp.zeros_like(n_s)
        m_s[...] = jnp.zeros_like(m_s)
        sr_s[...] = jnp.zeros_like(sr_s)
        sht_s[...] = jnp.zeros_like(sht_s)
        buf_s[...] = jnp.zeros_like(buf_s)

    prm = _load_params(bias_ref, na_ref, nb_ref, nc_ref, lbc_ref, cw_ref, layer)
    cst = _chunk_consts(c)
    cst["ret_dec"] = lambda h: rdec_ref[h]
    cst["ret_inter"] = lambda h: rinter_ref[h]
    cst["ret_toend"] = lambda h: rtoend_ref[h]
    cst["ret_cdec"] = rcdec_ref[...]
    cst["gla_m"] = gm_ref
    cst["gla_lv"] = lv_ref

    def state(s):
        def get(name, h):
            if name == "C":
                return c_s[s, h]
            if name == "n":
                return n_s[s, h:h + 1, :]
            if name == "m":
                return m_s[s, h:h + 1, :]
            if name == "Sr":
                return sr_s[s, h]
            if name == "ShT":
                return sht_s[s, h]
            return buf_s[s, 0:CONV_W - 1, :]

        def put(name, h, val):
            if name == "C":
                c_s[s, h] = val
            elif name == "n":
                n_s[s, h:h + 1, :] = val
            elif name == "m":
                m_s[s, h:h + 1, :] = val
            elif name == "Sr":
                sr_s[s, h] = val
            elif name == "ShT":
                sht_s[s, h] = val
            else:
                buf_s[s, 0:CONV_W - 1, :] = val

        return _State(get, put)

    def chunk(s, k):
        r = k * c
        cc = dict(cst)
        cc["cos2"] = cos_ref[r:r + c, :]
        cc["sin2"] = sin_ref[r:r + c, :]
        zc = lambda off, w: z_refs[s][r:r + c, off:off + w]

        def put_y(off, val):
            y_ref[s, r:r + c, off:off + BRANCH_W] = val.astype(BF16)

        return _mixer_chunk(zc, c, state(s), cc, prm, put_y)

    for s in range(ns):
        _pipelined(chunk(s, k) for k in range(tt // c))

    @pl.when(j == pl.num_programs(1) - 1)
    def _():
        c_out[...] = c_s[...]
        n_out[...] = n_s[:, 0:N_HEADS, :]
        m_out[...] = m_s[:, 0:N_HEADS, :]
        sr_out[...] = sr_s[...]
        for s in range(ns):
            for h in range(N_HEADS):
                sh_out[s, h] = sht_s[s, h].T
        buf_out[...] = buf_s[:, 0:CONV_W - 1, :]


def _const_spec(shape):
    nd = len(shape)
    return pl.BlockSpec(shape, lambda *_: (0,) * nd)


def _mixer_prompt(z, bsz, t, tt, layer, tabs, prms):
    c = math.gcd(t, CHUNK)
    nj = t // tt
    ns = PROMPT_LOCKSTEP
    assert bsz % ns == 0
    cos2, sin2, rdec, rinter, rtoend, rcdec = tabs
    state_shapes = [
        jax.ShapeDtypeStruct((bsz, N_HEADS, HEAD_DIM, HEAD_DIM), F32),
        jax.ShapeDtypeStruct((bsz, N_HEADS, HEAD_DIM), F32),
        jax.ShapeDtypeStruct((bsz, N_HEADS, LANES), F32),
        jax.ShapeDtypeStruct((bsz, N_HEADS, HEAD_DIM, HEAD_DIM), F32),
        jax.ShapeDtypeStruct((bsz, N_HEADS, HEAD_DIM, HEAD_DIM), F32),
        jax.ShapeDtypeStruct((bsz, CONV_W - 1, BRANCH_W), F32),
    ]
    big = pl.BlockSpec((ns, N_HEADS, HEAD_DIM, HEAD_DIM), lambda g, j: (g, 0, 0, 0))
    small = pl.BlockSpec((ns, N_HEADS, LANES), lambda g, j: (g, 0, 0))
    est = (2 * ns * tt * P_PAD * 4 + 2 * ns * tt * Y_W * 2 + 10 * ns * N_HEADS * HEAD_DIM * HEAD_DIM * 4 + (6 << 20))

    def z_spec(s, width):
        return pl.BlockSpec((tt, width), lambda g, j: ((g * ns + s) * nj + j, 0))

    outs = pl.pallas_call(
        functools.partial(_mixer_prompt_kernel, layer, c, tt),
        grid=(bsz // ns, nj),
        in_specs=[z_spec(s, P_PAD) for s in range(ns)]
                 + [pl.BlockSpec((tt, LANES), lambda g, j: (j, 0)),
                    pl.BlockSpec((tt, LANES), lambda g, j: (j, 0)),
                    _const_spec(rdec.shape), _const_spec(rinter.shape), _const_spec(rtoend.shape),
                    _const_spec(rcdec.shape)] + [_const_spec(p.shape) for p in prms],
        out_specs=[pl.BlockSpec((None, ns, tt, Y_W), lambda g, j: (g, 0, j, 0)),
                   big, small, small, big, big,
                   pl.BlockSpec((ns, CONV_W - 1, BRANCH_W), lambda g, j: (g, 0, 0))],
        out_shape=[jax.ShapeDtypeStruct((bsz // ns, ns, t, Y_W), BF16)] + state_shapes,
        scratch_shapes=[pltpu.VMEM((ns, N_HEADS, HEAD_DIM, HEAD_DIM), F32),
                        pltpu.VMEM((ns, SUBLANES, LANES), F32),
                        pltpu.VMEM((ns, SUBLANES, LANES), F32),
                        pltpu.VMEM((ns, N_HEADS, HEAD_DIM, HEAD_DIM), F32),
                        pltpu.VMEM((ns, N_HEADS, HEAD_DIM, HEAD_DIM), F32),
                        pltpu.VMEM((ns, SUBLANES, BRANCH_W), F32)],
        compiler_params=pltpu.CompilerParams(dimension_semantics=("arbitrary", "arbitrary"),
                                             vmem_limit_bytes=_vmem_limit(est)),
        name=f"mixer_prompt_l{layer}",
    )(*([z] * ns), cos2, sin2, rdec, rinter, rtoend, rcdec, *prms)
    return [outs[0].reshape(bsz * t, Y_W)] + list(outs[1:])


N_STATES = 6
N_MIXER_CONST_INPUTS = 15


def _mixer_sample_kernel(layer, c, bb, n_carried, *refs):
    (z_ref, cos_ref, sin_ref, rdec_ref, rinter_ref, rtoend_ref, rcdec_ref,
     bias_ref, na_ref, nb_ref, nc_ref, lbc_ref, cw_ref, gm_ref, lv_ref) = refs[:N_MIXER_CONST_INPUTS]
    c_in, n_in, m_in, sr_in, sh_in, buf_in = refs[N_MIXER_CONST_INPUTS:N_MIXER_CONST_INPUTS + N_STATES]
    y_ref, c_out, n_out, m_out, sr_out, sh_out, buf_out, ybuf = refs[N_MIXER_CONST_INPUTS + N_STATES + n_carried:]
    prm = _load_params(bias_ref, na_ref, nb_ref, nc_ref, lbc_ref, cw_ref, layer)
    cst = _chunk_consts(c)
    cst["ret_dec"] = lambda h: rdec_ref[h]
    cst["ret_inter"] = lambda h: rinter_ref[h]
    cst["ret_toend"] = lambda h: rtoend_ref[h]
    cst["ret_cdec"] = rcdec_ref[...]
    cst["gla_m"] = gm_ref
    cst["gla_lv"] = lv_ref
    cst["cos2"] = cos_ref[...]
    cst["sin2"] = sin_ref[...]

    def sequence(i):
        r = i * c
        zc = lambda off, w: z_ref[r:r + c, off:off + w]

        def get(name, h):
            if name == "C":
                return c_in[i, h]
            if name == "n":
                return n_in[i, pl.ds(h, 1), :]
            if name == "m":
                return m_in[i, pl.ds(h, 1), :]
            if name == "Sr":
                return sr_in[i, h]
            if name == "ShT":
                return sh_in[i, h].T
            return buf_in[i]

        def put(name, h, val):
            if name == "C":
                c_out[i, h] = val
            elif name == "n":
                n_out[i, pl.ds(h, 1), :] = val
            elif name == "m":
                m_out[i, pl.ds(h, 1), :] = val
            elif name == "Sr":
                sr_out[i, h] = val
            elif name == "ShT":
                sh_out[i, h] = val.T
            else:
                buf_out[i] = val

        def put_y(off, val):
            ybuf[r:r + c, off:off + BRANCH_W] = val

        return _mixer_chunk(zc, c, _State(get, put), cst, prm, put_y)

    _pipelined(sequence(i) for i in range(bb))
    y_ref[...] = ybuf[...].astype(BF16)


def _mixer_sample(z, row0, bsz, t, bb, layer, tabs, prms, states, carried):
    c = t
    rows = bb * t
    blk0 = row0 // rows
    cos2, sin2, rdec, rinter, rtoend, rcdec = tabs

    def lspec(*tail):
        zeros = (0,) * (len(tail) - 1)
        return pl.BlockSpec((None,) + tail, lambda i: (layer, i) + zeros)

    big = lspec(bb, N_HEADS, HEAD_DIM, HEAD_DIM)
    small = lspec(bb, N_HEADS, LANES)
    state_specs = [big, small, small, big, big, lspec(bb, CONV_W - 1, BRANCH_W)]
    state_shapes = [jax.ShapeDtypeStruct(s.shape, F32) for s in states]
    n_in = N_MIXER_CONST_INPUTS + N_STATES
    est = (2 * rows * P_PAD * 4 + 2 * rows * Y_W * 2 + rows * Y_W * 4
           + 2 * 2 * 3 * bb * N_HEADS * HEAD_DIM * HEAD_DIM * 4)
    return pl.pallas_call(
        functools.partial(_mixer_sample_kernel, layer, c, bb, len(carried)),
        grid=(bsz // bb,),
        in_specs=[pl.BlockSpec((rows, P_PAD), lambda i: (blk0 + i, 0)),
                  _const_spec(cos2.shape), _const_spec(sin2.shape),
                  _const_spec(rdec.shape), _const_spec(rinter.shape), _const_spec(rtoend.shape),
                  _const_spec(rcdec.shape)] + [_const_spec(p.shape) for p in prms] + state_specs
                 + [pl.BlockSpec(memory_space=pl.ANY)] * len(carried),
        out_specs=[pl.BlockSpec((rows, Y_W), lambda i: (i, 0))] + state_specs,
        out_shape=[jax.ShapeDtypeStruct((bsz * t, Y_W), BF16)] + state_shapes,
        input_output_aliases={n_in + k: 1 + k for k in range(len(carried))},
        scratch_shapes=[pltpu.VMEM((rows, Y_W), F32)],
        compiler_params=pltpu.CompilerParams(dimension_semantics=("arbitrary",),
                                             vmem_limit_bytes=_vmem_limit(est)),
        name=f"mixer_sample_l{layer}",
    )(z, cos2, sin2, rdec, rinter, rtoend, rcdec, *prms, *states, *carried)


def _layer_norm(r, g, b):
    mu = jnp.mean(r, axis=1, keepdims=True)
    rc = r - mu
    var = jnp.mean(rc * rc, axis=1, keepdims=True)
    return rc * lax.rsqrt(var + LN_EPS) * g + b


def _merge_kernel(npb, xp_ref, xs_ref, yp_ref, ys_ref, wb_ref, wg_ref, bg_ref, wo_ref, g_ref, b_ref,
                  hp_ref, hs_ref):
    x = _group_load(npb, xp_ref, xs_ref)
    xb = x.astype(BF16)
    y = _group_load(npb, yp_ref, ys_ref)
    acc = None
    for n in range(N_BRANCH):
        proj = jnp.dot(y[:, n * BRANCH_W:(n + 1) * BRANCH_W], wb_ref[n].astype(BF16), preferred_element_type=F32)
        gate = jax.nn.sigmoid(jnp.dot(xb, wg_ref[:, n * D_MODEL:(n + 1) * D_MODEL].astype(BF16),
                                      preferred_element_type=F32)
                              + bg_ref[:, n * D_MODEL:(n + 1) * D_MODEL])
        acc = gate * proj if acc is None else acc + gate * proj
    mix = jnp.dot(acc.astype(BF16), wo_ref[...].astype(BF16), preferred_element_type=F32)
    _group_store(npb, hp_ref, hs_ref, _layer_norm(ALPHA * x + mix, g_ref[...], b_ref[...]))


def _layer_spec(layer, *tail):
    return pl.BlockSpec((None,) + tail, lambda *_: (layer,) + (0,) * len(tail))


def _group_out(xp, xs, tm, npb, nsb):
    return dict(out_specs=_group_specs(tm, D_MODEL, npb, nsb),
                out_shape=[jax.ShapeDtypeStruct(xp.shape, F32), jax.ShapeDtypeStruct(xs.shape, F32)])


def _merge(xp, xs, y_p, y_s, layer, wb, wg, bg, wo, g, b, tm):
    npb, nsb = xp.shape[0] // tm, xs.shape[0] // tm
    est = ((N_BRANCH * BRANCH_W * D_MODEL + D_MODEL * N_BRANCH * D_MODEL + D_MODEL * D_MODEL) * 4
           + 2 * 2 * tm * D_MODEL * 4 * 2 + 2 * 2 * tm * Y_W * 2 + 8 * tm * D_MODEL * 4)
    return pl.pallas_call(
        functools.partial(_merge_kernel, npb),
        grid=(npb + nsb,),
        in_specs=_group_specs(tm, D_MODEL, npb, nsb) + _group_specs(tm, Y_W, npb, nsb)
                 + [_layer_spec(layer, N_BRANCH, BRANCH_W, D_MODEL), _layer_spec(layer, D_MODEL, N_BRANCH * D_MODEL),
                    _const_spec(bg.shape), _layer_spec(layer, D_MODEL, D_MODEL),
                    _const_spec(g.shape), _const_spec(b.shape)],
        compiler_params=pltpu.CompilerParams(dimension_semantics=("arbitrary",),
                                             vmem_limit_bytes=_vmem_limit(est)),
        name="merge",
        **_group_out(xp, xs, tm, npb, nsb),
    )(xp, xs, y_p, y_s, wb, wg, bg, wo, g, b)


FFN_CHUNK = 1024


def _ffn_kernel(npb, hp_ref, hs_ref, wu_ref, wd_ref, g_ref, b_ref, op_ref, os_ref):
    h = _group_load(npb, hp_ref, hs_ref)
    hb = h.astype(BF16)
    ff = None
    for j in range(0, D_FF, FFN_CHUNK):
        u = jnp.maximum(jnp.dot(hb, wu_ref[:, j:j + FFN_CHUNK].astype(BF16), preferred_element_type=F32), 0.0)
        part = jnp.dot((u * u).astype(BF16), wd_ref[j:j + FFN_CHUNK, :].astype(BF16), preferred_element_type=F32)
        ff = part if ff is None else ff + part
    _group_store(npb, op_ref, os_ref, _layer_norm(ALPHA * h + ff, g_ref[...], b_ref[...]))


def _ffn(hp, hs, layer, wu, wd, g, b, tm):
    npb, nsb = hp.shape[0] // tm, hs.shape[0] // tm
    est = (2 * D_MODEL * D_FF * 4 + 2 * 2 * 2 * tm * D_MODEL * 4 + tm * FFN_CHUNK * 6 + 4 * tm * D_MODEL * 4
           + 2 * D_MODEL * FFN_CHUNK * 2)
    return pl.pallas_call(
        functools.partial(_ffn_kernel, npb),
        grid=(npb + nsb,),
        in_specs=_group_specs(tm, D_MODEL, npb, nsb)
                 + [_layer_spec(layer, D_MODEL, D_FF), _layer_spec(layer, D_FF, D_MODEL),
                    _const_spec(g.shape), _const_spec(b.shape)],
        compiler_params=pltpu.CompilerParams(dimension_semantics=("arbitrary",),
                                             vmem_limit_bytes=_vmem_limit(est)),
        name="ffn",
        **_group_out(hp, hs, tm, npb, nsb),
    )(hp, hs, wu, wd, g, b)


def _pad_lanes(a):
    return jnp.pad(a, ((0, 0), (0, LANES - a.shape[1])))


def _position_tables(pos, c):
    half = HEAD_DIM // 2
    inv = ROPE_BASE ** (-jnp.arange(half, dtype=F32) / half)
    ang = pos[:, None] * inv[None, :]
    cos, sin = jnp.cos(ang), jnp.sin(ang)
    cos2 = jnp.concatenate([cos, cos], axis=1)
    sin2 = jnp.concatenate([-sin, sin], axis=1)
    log_gamma = jnp.log(1.0 - 2.0 ** (-5.0 - jnp.arange(N_HEADS, dtype=F32)))
    idx = jnp.arange(c, dtype=F32)
    rel = idx[:, None] - idx[None, :]
    dec = jnp.where(rel >= 0, jnp.exp(log_gamma[:, None, None] * jnp.maximum(rel, 0.0)), 0.0)
    inter = jnp.swapaxes(jnp.exp(log_gamma[:, None] * (idx + 1.0)[None, :]), 0, 1)
    to_end = jnp.swapaxes(jnp.exp(log_gamma[:, None] * (c - 1.0 - idx)[None, :]), 0, 1)
    cdec = jnp.broadcast_to(jnp.exp(log_gamma * c)[:, None], (N_HEADS, LANES))
    cdec = jnp.pad(cdec, ((0, SUBLANES - N_HEADS), (0, 0)))
    lanes = lambda a: jnp.broadcast_to(jnp.swapaxes(a, 0, 1)[:, :, None], (N_HEADS, c, LANES))
    return cos2, sin2, dec, lanes(inter), lanes(to_end), cdec


def _gla_tables(c):
    r = np.arange(c)[:, None]
    j = np.arange(c)[None, :]
    mats = [j <= r]
    n = c // 2
    while n > 1:
        start = (r // n) * n
        odd = (r // n) % 2 == 1
        mats.append(np.where(odd, (j >= start) & (j <= r), (j > r) & (j <= start + n - 1)))
        n //= 2
    x = np.maximum(r ^ j, 1)
    level = np.where(j > r, -1, np.where(j == r, 0, 1 << np.floor(np.log2(x)).astype(np.int64)))
    return jnp.asarray(np.tile(np.concatenate(mats, 0), (1, 3)), BF16), jnp.asarray(level, jnp.int32)


def kernel(x_prompt, x_sample, state_mlstm_C, state_mlstm_n, state_mlstm_m, state_ret, state_hgrn, state_conv,
           w_in, b_mlstm_gate, norm_a, norm_b, norm_c, lb_c, conv_w, w_branch, w_gate, b_gate, w_out,
           ln1_g, ln1_b, ln2_g, ln2_b, w_up, w_down):
    bp, tp, _ = x_prompt.shape
    bs, ts, _ = x_sample.shape
    n_p, n_s = bp * tp, bs * ts
    tm = 512
    tt = 256
    bb = 8
    assert n_p % tm == 0 and n_s % tm == 0 and tp % tt == 0 and bs % bb == 0 and n_p % (bb * ts) == 0

    xp = x_prompt.reshape(n_p, D_MODEL)
    xs = x_sample.reshape(n_s, D_MODEL)
    tabs_p = _position_tables(jnp.arange(tp, dtype=F32), math.gcd(tp, CHUNK))
    tabs_s = _position_tables(PAST_LEN + jnp.arange(ts, dtype=F32), math.gcd(ts, CHUNK))
    gla_p = _gla_tables(math.gcd(tp, CHUNK))
    gla_s = _gla_tables(math.gcd(ts, CHUNK))
    lbc = lb_c.astype(F32)
    row = lambda a: a.reshape(1, -1).astype(F32)
    assert w_in.shape[1:] == (D_MODEL, P_IN)
    w_in_t = jnp.swapaxes(w_in, 1, 2).astype(F32)
    wb, wg, wo, wu, wd = (w.astype(F32) for w in (w_branch, w_gate, w_out, w_up, w_down))
    st_in = (state_mlstm_C.astype(F32), state_mlstm_n.astype(F32),
             jnp.broadcast_to(state_mlstm_m.astype(F32)[..., None], (DEPTH, bs, N_HEADS, LANES)),
             state_ret.astype(F32), state_hgrn.astype(F32), state_conv.astype(F32))

    p_states, s_states = [], ()
    for l in range(DEPTH):
        prms = (_pad_lanes(row(b_mlstm_gate[l])), row(norm_a[l]), row(norm_b[l]), row(norm_c[l]), lbc,
                conv_w[l].astype(F32))
        z = _inproj(xp, xs, l, w_in_t, 256)
        outs_p = _mixer_prompt(z, bp, tp, tt, l, tabs_p, prms + gla_p)
        outs_s = _mixer_sample(z, n_p, bs, ts, bb, l, tabs_s, prms + gla_s, st_in, s_states)
        hp, hs = _merge(xp, xs, outs_p[0], outs_s[0], l, wb, wg, row(b_gate[l]), wo, row(ln1_g[l]), row(ln1_b[l]), 256)
        xp, xs = _ffn(hp, hs, l, wu, wd, row(ln2_g[l]), row(ln2_b[l]), 256)
        p_states.append(outs_p[1:])
        s_states = tuple(outs_s[1:])

    cs, ns, ms, srs, shs, bufs = zip(*p_states)
    p_out = (jnp.stack(cs), jnp.stack(ns), jnp.stack([m[..., 0] for m in ms]),
             jnp.stack(srs), jnp.stack(shs), jnp.stack(bufs))
    s_out = s_states[:2] + (s_states[2][..., 0],) + s_states[3:]
    return (xp.reshape(bp, tp, D_MODEL), xs.reshape(bs, ts, D_MODEL)) + p_out + s_out
```

```python
import functools
import math

import jax
import jax.numpy as jnp
import numpy as np
from jax import lax
from jax.experimental import pallas as pl
from jax.experimental.pallas import tpu as pltpu

F32 = jnp.float32
BF16 = jnp.bfloat16

D_MODEL = 1024
DEPTH = 2
BRANCH_W = D_MODEL // 2
N_HEADS = 4
HEAD_DIM = BRANCH_W // N_HEADS
CONV_W = 3
N_BRANCH = 4
D_FF = 4 * D_MODEL
CHUNK = 64
ROPE_BASE = 10000.0
LN_EPS = 1e-5
NEG_BIG = -1e30
GATE_CLAMP = 1.0 - 1e-6
ALPHA = (2 * DEPTH) ** 0.25
PAST_LEN = 16384
K_SCALE = HEAD_DIM ** -0.5

LANES = 128
SUBLANES = 8
VMEM_BYTES = 64 * 1024 * 1024

(QA, KA, VA, OA, QB, KB, VB, GB, QC, FC, IC, GC, XD, BGD, CGD) = [i * BRANCH_W for i in range(15)]
GATE_OFF = 15 * BRANCH_W
P_PAD = GATE_OFF + LANES
Y_W = N_BRANCH * BRANCH_W


def _vmem_limit(est_bytes):
    return int(min(est_bytes + est_bytes // 4 + (4 << 20), VMEM_BYTES - (6 << 20)))


def _dot(a, b):
    return jnp.dot(a.astype(BF16), b.astype(BF16), preferred_element_type=F32)


def _dot_nt(a, b):
    return lax.dot_general(a.astype(BF16), b.astype(BF16), (((1,), (1,)), ((), ())),
                           preferred_element_type=F32)


def _dot_tn(a, b):
    return lax.dot_general(a.astype(BF16), b.astype(BF16), (((0,), (0,)), ((), ())),
                           preferred_element_type=F32)


def _split3(x):
    hi = x.astype(BF16)
    r = x - hi.astype(F32)
    mid = r.astype(BF16)
    lo = (r - mid.astype(F32)).astype(BF16)
    return hi, mid, lo


def _dot_exact_rhs(a01x3, x):
    return jnp.dot(a01x3, _stack3(x), preferred_element_type=F32)


def _stack3(x):
    parts = _split3(x)
    if x.shape[0] % (2 * SUBLANES) == 0:
        return jnp.concatenate(parts, axis=0)
    return jnp.concatenate([p.astype(F32) for p in parts], axis=0).astype(BF16)


def _dot_nt_exact_rhs(a01, x):
    hi, mid, lo = _split3(x)
    d = lambda p: lax.dot_general(a01, p, (((1,), (1,)), ((), ())), preferred_element_type=F32)
    return (d(hi) + d(mid)) + d(lo)


def _silu(x):
    return x * jax.nn.sigmoid(x)


def _row_sum(x, c):
    if c <= SUBLANES:
        return jnp.sum(x, axis=1, keepdims=True)
    hi = x.astype(BF16)
    lo = (x - hi.astype(F32)).astype(BF16)
    ones = jnp.ones((2 * x.shape[1], LANES), BF16)
    return jnp.dot(jnp.concatenate([hi, lo], axis=1), ones, preferred_element_type=F32)


def _group_specs(tm, width, npb, nsb):
    return [pl.BlockSpec((tm, width), lambda i: (jnp.minimum(i, npb - 1), 0)),
            pl.BlockSpec((tm, width), lambda i: (jnp.clip(i - npb, 0, nsb - 1), 0))]


def _group_load(npb, p_ref, s_ref):
    return jnp.where(pl.program_id(0) < npb, p_ref[...], s_ref[...])


def _group_store(npb, p_ref, s_ref, val):
    is_p = pl.program_id(0) < npb

    @pl.when(is_p)
    def _():
        p_ref[...] = val

    @pl.when(jnp.logical_not(is_p))
    def _():
        s_ref[...] = val


P_IN = 15 * BRANCH_W + 2 * N_HEADS
GATE_FEATURES = (4 * BRANCH_W, 4 * BRANCH_W + 2 * N_HEADS)


def _inproj_kernel(npb, xp_ref, xs_ref, wt_ref, z_ref):
    xb = _group_load(npb, xp_ref, xs_ref).astype(BF16)
    nt = lambda w: lax.dot_general(xb, w.astype(BF16), (((1,), (1,)), ((), ())), preferred_element_type=F32)
    g0, g1 = GATE_FEATURES
    for off in range(0, GATE_OFF, BRANCH_W):
        src = off if off < g0 else off + (g1 - g0)
        z_ref[:, off:off + BRANCH_W] = nt(wt_ref[src:src + BRANCH_W, :])
    gates = nt(wt_ref[g0:g1, :])
    z_ref[:, GATE_OFF:P_PAD] = jnp.concatenate(
        [gates, jnp.zeros((gates.shape[0], P_PAD - GATE_OFF - (g1 - g0)), F32)], axis=1)


def _inproj(xp, xs, layer, wt, tm):
    npb, nsb = xp.shape[0] // tm, xs.shape[0] // tm
    est = D_MODEL * P_IN * 4 + 2 * tm * P_PAD * 4 + 2 * 2 * tm * D_MODEL * 4
    return pl.pallas_call(
        functools.partial(_inproj_kernel, npb),
        grid=(npb + nsb,),
        in_specs=_group_specs(tm, D_MODEL, npb, nsb) + [_layer_spec(layer, P_IN, D_MODEL)],
        out_specs=pl.BlockSpec((tm, P_PAD), lambda i: (i, 0)),
        out_shape=jax.ShapeDtypeStruct(((npb + nsb) * tm, P_PAD), F32),
        compiler_params=pltpu.CompilerParams(dimension_semantics=("arbitrary",),
                                             vmem_limit_bytes=_vmem_limit(est)),
        name="inproj",
    )(xp, xs, wt)


class _State:
    def __init__(self, get, put):
        self.get = get
        self.put = put


LEVEL1_DONE, STATE_STORED = "level 1 done", "state stored"


def _advance(g, until=None):
    for label in g:
        if until is not None and label == until:
            return


def _pipelined(chunk_generators):
    pending_tail = None
    for g in chunk_generators:
        _advance(g, LEVEL1_DONE)
        if pending_tail is not None:
            _advance(pending_tail)
        _advance(g, STATE_STORED)
        pending_tail = g
    _advance(pending_tail)


def _mixer_chunk(zc, c, st, cst, prm, put_y):
    tril, causal, sel8 = cst["tril"], cst["causal"], cst["sel8"]
    old = {(name, h): st.get(name, h) for name in ("C", "n", "m", "Sr", "ShT") for h in range(N_HEADS)}
    old[("buf", 0)] = st.get("buf", 0)
    new = {}

    heads = range(N_HEADS)
    hd = lambda off, h: zc(off + h * HEAD_DIM, HEAD_DIM)
    hsl = lambda a, h: a[:, h * HEAD_DIM:(h + 1) * HEAD_DIM]
    cos2, sin2 = cst["cos2"], cst["sin2"]
    rope = lambda x: x * cos2 + pltpu.roll(x, HEAD_DIM // 2, axis=1) * sin2
    lv = cst["gla_lv"][...]

    gp = zc(GATE_OFF, LANES) + prm["bias_if"]
    lf = jnp.minimum(gp, 0.0) - jnp.log1p(jnp.exp(-jnp.abs(gp)))
    b_all = _dot_exact_rhs(tril, lf)
    gp_t = _dot_nt_exact_rhs(sel8, gp)
    s_a, qc_a, s_b, qs_b, a_c = [], [], [], [], []
    for h in heads:
        qa = hd(QA, h)
        s_a.append(_dot_nt(qa, hd(KA, h) * K_SCALE))
        qc_a.append(_dot(qa, old[("C", h)]))
    for h in heads:
        qb = rope(hd(QB, h))
        kb = rope(hd(KB, h)) * K_SCALE
        s_b.append(_dot_nt(qb, kb))
        qs_b.append(_dot(qb, old[("Sr", h)]))
        new[("Sr", h)] = (cst["ret_cdec"][h:h + 1, :] * old[("Sr", h)]
                          + _dot_tn(kb * cst["ret_toend"](h), hd(VB, h)))
    kc_all = (1.0 - prm["lb"]) * jax.nn.sigmoid(-zc(FC, BRANCH_W))
    g_all = jnp.log1p(-jnp.minimum(kc_all, GATE_CLAMP))
    xs = _dot_exact_rhs(cst["gla_m"][...], g_all)
    bc_all = xs[0:c]
    qg_all = _silu(zc(QC, BRANCH_W))
    for h in heads:
        qg, kc = hsl(qg_all, h), hsl(kc_all, h)
        a = jnp.where(lv == 0, _dot_nt(qg, kc), 0.0)
        a_c.append(jnp.where(lv == 1, _dot_nt(qg * jnp.exp(hsl(g_all, h)), kc), a))

    yield LEVEL1_DONE
    b_t = _dot_nt_exact_rhs(sel8, b_all)
    qn_a = [jnp.sum(hd(QA, h) * old[("n", h)], axis=1, keepdims=True) for h in heads]
    o_b = []
    for h in heads:
        o_b.append(_dot(s_b[h] * cst["ret_dec"](h), hd(VB, h)) + qs_b[h] * cst["ret_inter"](h))
    num_a, den_a, oi_c = [], [], []
    for h in heads:
        b_col = b_all[:, N_HEADS + h:N_HEADS + h + 1]
        b_row = b_t[N_HEADS + h:N_HEADS + h + 1, :]
        a_col = b_col + old[("m", h)][:, 0:1]
        dmat = jnp.where(causal, (b_col - b_row) + gp_t[h:h + 1, :], NEG_BIG)
        m_t = jnp.maximum(a_col, jnp.max(dmat, axis=1, keepdims=True))
        w_inter = jnp.exp(a_col - m_t)
        s = s_a[h] * jnp.exp(dmat - m_t)
        num_a.append(_dot(s, hd(VA, h)) + qc_a[h] * w_inter)
        den = jnp.sum(s, axis=1, keepdims=True) + qn_a[h] * w_inter
        den_a.append(jnp.maximum(jnp.abs(den), jnp.exp(-m_t)))
    for h in heads:
        b_col = b_all[:, N_HEADS + h:N_HEADS + h + 1]
        m_prev = old[("m", h)][:, 0:1]
        b_last = b_col[c - 1:c, :]
        g_col = (b_last - b_col) + gp[:, h:h + 1]
        m_new = jnp.maximum(b_last + m_prev, jnp.max(g_col, axis=0, keepdims=True))
        dec = jnp.exp(b_last + m_prev - m_new)
        kw = (hd(KA, h) * K_SCALE) * jnp.exp(g_col - m_new)
        new[("C", h)] = dec * old[("C", h)] + _dot_tn(kw, hd(VA, h))
        new[("n", h)] = dec * old[("n", h)] + jnp.sum(kw, axis=0, keepdims=True)
        new[("m", h)] = jnp.broadcast_to(m_new, (1, LANES))
    for h in heads:
        qg, kc, bc = hsl(qg_all, h), hsl(kc_all, h), hsl(bc_all, h)
        a, sec, n = a_c[h], 1, c // 2
        while n > 1:
            e = jnp.exp(hsl(xs[sec * c:(sec + 1) * c], h))
            a = jnp.where(lv == n, _dot_nt(qg * e, kc * e), a)
            sec, n = sec + 1, n // 2
        a_c[h] = a
        sh_t = old[("ShT", h)]
        oi_c.append(_dot_nt(qg * jnp.exp(bc), sh_t))
        bc_last = bc[c - 1:c, :]
        new[("ShT", h)] = jnp.exp(bc_last) * sh_t + _dot_tn(hd(IC, h), kc * jnp.exp(bc_last - bc))
    o_c = [_dot(a_c[h], hd(IC, h)) + oi_c[h] for h in heads]
    u = zc(CGD, BRANCH_W) * zc(XD, BRANCH_W)
    new[("buf", 0)] = u[c - 2:c, :]
    for (name, h), val in new.items():
        st.put(name, h, val)

    yield STATE_STORED
    ya, yb, yc = [], [], []
    lane_mean = lambda x: _row_sum(x, c) * (1.0 / x.shape[1])
    for h in heads:
        hh = num_a[h] / den_a[h]
        hn = hh * lax.rsqrt(lane_mean(hh * hh) + LN_EPS)
        ya.append(jax.nn.sigmoid(hd(OA, h)) * (hn * hsl(prm["norm_a"], h)))
    put_y(0, jnp.concatenate(ya, axis=1))
    buf = old[("buf", 0)]
    row = lax.broadcasted_iota(jnp.int32, (c, BRANCH_W), 0)
    u1 = jnp.where(row == 0, buf[1:2, :], pltpu.roll(u, 1, axis=0))
    u2 = jnp.where(row == 0, buf[0:1, :], jnp.where(row == 1, buf[1:2, :], pltpu.roll(u, 2, axis=0)))
    cw = prm["conv_w"]
    yconv = (u2 * cw[0:1, :] + u1 * cw[1:2, :]) + u * cw[2:3, :]
    put_y(3 * BRANCH_W, zc(BGD, BRANCH_W) * yconv)
    for h in heads:
        oc = o_b[h] - lane_mean(o_b[h])
        on = oc * lax.rsqrt(lane_mean(oc * oc) + LN_EPS)
        yb.append(_silu(hd(GB, h)) * (on * hsl(prm["norm_b"], h)))
    put_y(BRANCH_W, jnp.concatenate(yb, axis=1))
    for h in heads:
        on = o_c[h] * lax.rsqrt(lane_mean(o_c[h] * o_c[h]) + LN_EPS)
        yc.append(_silu(hd(GC, h)) * (on * hsl(prm["norm_c"], h)))
    put_y(2 * BRANCH_W, jnp.concatenate(yc, axis=1))


def _chunk_consts(c):
    r = lax.broadcasted_iota(jnp.int32, (c, c), 0)
    l = lax.broadcasted_iota(jnp.int32, (c, c), 1)
    causal = r >= l
    r3 = lax.broadcasted_iota(jnp.int32, (c, 3 * c), 0)
    l3 = lax.broadcasted_iota(jnp.int32, (c, 3 * c), 1)
    l3 = jnp.where(l3 >= 2 * c, l3 - 2 * c, jnp.where(l3 >= c, l3 - c, l3))
    r8 = lax.broadcasted_iota(jnp.int32, (SUBLANES, LANES), 0)
    l8 = lax.broadcasted_iota(jnp.int32, (SUBLANES, LANES), 1)
    return {"causal": causal,
            "tril": jnp.where(r3 >= l3, 1.0, 0.0).astype(BF16),
            "sel8": jnp.where(r8 == l8, 1.0, 0.0).astype(BF16)}


def _layer_lb(lbc, layer):
    e = jnp.exp(lbc - jnp.max(lbc, axis=0, keepdims=True))
    sm = e / jnp.sum(e, axis=0, keepdims=True)
    cum = sm[0:1, :]
    for j in range(1, layer + 1):
        cum = cum + sm[j:j + 1, :]
    return cum - sm[0:1, :]


def _load_params(bias_ref, na_ref, nb_ref, nc_ref, lbc_ref, cw_ref, layer):
    return {"bias_if": bias_ref[...], "norm_a": na_ref[...], "norm_b": nb_ref[...], "norm_c": nc_ref[...],
            "lb": _layer_lb(lbc_ref[...], layer), "conv_w": cw_ref[...]}


PROMPT_LOCKSTEP = 1


def _mixer_prompt_kernel(layer, c, tt, *refs):
    ns = PROMPT_LOCKSTEP
    z_refs = refs[:ns]
    (cos_ref, sin_ref, rdec_ref, rinter_ref, rtoend_ref, rcdec_ref,
     bias_ref, na_ref, nb_ref, nc_ref, lbc_ref, cw_ref, gm_ref, lv_ref,
     y_ref, c_out, n_out, m_out, sr_out, sh_out, buf_out,
     c_s, n_s, m_s, sr_s, sht_s, buf_s) = refs[ns:]
    j = pl.program_id(1)

    @pl.when(j == 0)
    def _():
        c_s[...] = jnp.zeros_like(c_s)
        n_s[...] = jnp.zeros_like(n_s)
        m_s[...] = jnp.zeros_like(m_s)
        sr_s[...] = jnp.zeros_like(sr_s)
        sht_s[...] = jnp.zeros_like(sht_s)
        buf_s[...] = jnp.zeros_like(buf_s)

    prm = _load_params(bias_ref, na_ref, nb_ref, nc_ref, lbc_ref, cw_ref, layer)
    cst = _chunk_consts(c)
    cst["ret_dec"] = lambda h: rdec_ref[h]
    cst["ret_inter"] = lambda h: rinter_ref[h]
    cst["ret_toend"] = lambda h: rtoend_ref[h]
    cst["ret_cdec"] = rcdec_ref[...]
    cst["gla_m"] = gm_ref
    cst["gla_lv"] = lv_ref

    def state(s):
        def get(name, h):
            if name == "C":
                return c_s[s, h]
            if name == "n":
                return n_s[s, h:h + 1, :]
            if name == "m":
                return m_s[s, h:h + 1, :]
            if name == "Sr":
                return sr_s[s, h]
            if name == "ShT":
                return sht_s[s, h]
            return buf_s[s, 0:CONV_W - 1, :]

        def put(name, h, val):
            if name == "C":
                c_s[s, h] = val
            elif name == "n":
                n_s[s, h:h + 1, :] = val
            elif name == "m":
                m_s[s, h:h + 1, :] = val
            elif name == "Sr":
                sr_s[s, h] = val
            elif name == "ShT":
                sht_s[s, h] = val
            else:
                buf_s[s, 0:CONV_W - 1, :] = val

        return _State(get, put)

    def chunk(s, k):
        r = k * c
        cc = dict(cst)
        cc["cos2"] = cos_ref[r:r + c, :]
        cc["sin2"] = sin_ref[r:r + c, :]
        zc = lambda off, w: z_refs[s][r:r + c, off:off + w]

        def put_y(off, val):
            y_ref[s, r:r + c, off:off + BRANCH_W] = val.astype(BF16)

        return _mixer_chunk(zc, c, state(s), cc, prm, put_y)

    for s in range(ns):
        _pipelined(chunk(s, k) for k in range(tt // c))

    @pl.when(j == pl.num_programs(1) - 1)
    def _():
        c_out[...] = c_s[...]
        n_out[...] = n_s[:, 0:N_HEADS, :]
        m_out[...] = m_s[:, 0:N_HEADS, :]
        sr_out[...] = sr_s[...]
        for s in range(ns):
            for h in range(N_HEADS):
                sh_out[s, h] = sht_s[s, h].T
        buf_out[...] = buf_s[:, 0:CONV_W - 1, :]


def _const_spec(shape):
    nd = len(shape)
    return pl.BlockSpec(shape, lambda *_: (0,) * nd)


def _mixer_prompt(z, bsz, t, tt, layer, tabs, prms):
    c = math.gcd(t, CHUNK)
    nj = t // tt
    ns = PROMPT_LOCKSTEP
    assert bsz % ns == 0
    cos2, sin2, rdec, rinter, rtoend, rcdec = tabs
    state_shapes = [
        jax.ShapeDtypeStruct((bsz, N_HEADS, HEAD_DIM, HEAD_DIM), F32),
        jax.ShapeDtypeStruct((bsz, N_HEADS, HEAD_DIM), F32),
        jax.ShapeDtypeStruct((bsz, N_HEADS, LANES), F32),
        jax.ShapeDtypeStruct((bsz, N_HEADS, HEAD_DIM, HEAD_DIM), F32),
        jax.ShapeDtypeStruct((bsz, N_HEADS, HEAD_DIM, HEAD_DIM), F32),
        jax.ShapeDtypeStruct((bsz, CONV_W - 1, BRANCH_W), F32),
    ]
    big = pl.BlockSpec((ns, N_HEADS, HEAD_DIM, HEAD_DIM), lambda g, j: (g, 0, 0, 0))
    small = pl.BlockSpec((ns, N_HEADS, LANES), lambda g, j: (g, 0, 0))
    est = (2 * ns * tt * P_PAD * 4 + 2 * ns * tt * Y_W * 2 + 10 * ns * N_HEADS * HEAD_DIM * HEAD_DIM * 4 + (6 << 20))

    def z_spec(s, width):
        return pl.BlockSpec((tt, width), lambda g, j: ((g * ns + s) * nj + j, 0))

    outs = pl.pallas_call(
        functools.partial(_mixer_prompt_kernel, layer, c, tt),
        grid=(bsz // ns, nj),
        in_specs=[z_spec(s, P_PAD) for s in range(ns)]
                 + [pl.BlockSpec((tt, LANES), lambda g, j: (j, 0)),
                    pl.BlockSpec((tt, LANES), lambda g, j: (j, 0)),
                    _const_spec(rdec.shape), _const_spec(rinter.shape), _const_spec(rtoend.shape),
                    _const_spec(rcdec.shape)] + [_const_spec(p.shape) for p in prms],
        out_specs=[pl.BlockSpec((None, ns, tt, Y_W), lambda g, j: (g, 0, j, 0)),
                   big, small, small, big, big,
                   pl.BlockSpec((ns, CONV_W - 1, BRANCH_W), lambda g, j: (g, 0, 0))],
        out_shape=[jax.ShapeDtypeStruct((bsz // ns, ns, t, Y_W), BF16)] + state_shapes,
        scratch_shapes=[pltpu.VMEM((ns, N_HEADS, HEAD_DIM, HEAD_DIM), F32),
                        pltpu.VMEM((ns, SUBLANES, LANES), F32),
                        pltpu.VMEM((ns, SUBLANES, LANES), F32),
                        pltpu.VMEM((ns, N_HEADS, HEAD_DIM, HEAD_DIM), F32),
                        pltpu.VMEM((ns, N_HEADS, HEAD_DIM, HEAD_DIM), F32),
                        pltpu.VMEM((ns, SUBLANES, BRANCH_W), F32)],
        compiler_params=pltpu.CompilerParams(dimension_semantics=("arbitrary", "arbitrary"),
                                             vmem_limit_bytes=_vmem_limit(est)),
        name=f"mixer_prompt_l{layer}",
    )(*([z] * ns), cos2, sin2, rdec, rinter, rtoend, rcdec, *prms)
    return [outs[0].reshape(bsz * t, Y_W)] + list(outs[1:])


N_STATES = 6
N_MIXER_CONST_INPUTS = 15


def _mixer_sample_kernel(layer, c, bb, n_carried, *refs):
    (z_ref, cos_ref, sin_ref, rdec_ref, rinter_ref, rtoend_ref, rcdec_ref,
     bias_ref, na_ref, nb_ref, nc_ref, lbc_ref, cw_ref, gm_ref, lv_ref) = refs[:N_MIXER_CONST_INPUTS]
    c_in, n_in, m_in, sr_in, sh_in, buf_in = refs[N_MIXER_CONST_INPUTS:N_MIXER_CONST_INPUTS + N_STATES]
    y_ref, c_out, n_out, m_out, sr_out, sh_out, buf_out, ybuf = refs[N_MIXER_CONST_INPUTS + N_STATES + n_carried:]
    prm = _load_params(bias_ref, na_ref, nb_ref, nc_ref, lbc_ref, cw_ref, layer)
    cst = _chunk_consts(c)
    cst["ret_dec"] = lambda h: rdec_ref[h]
    cst["ret_inter"] = lambda h: rinter_ref[h]
    cst["ret_toend"] = lambda h: rtoend_ref[h]
    cst["ret_cdec"] = rcdec_ref[...]
    cst["gla_m"] = gm_ref
    cst["gla_lv"] = lv_ref
    cst["cos2"] = cos_ref[...]
    cst["sin2"] = sin_ref[...]

    def sequence(i):
        r = i * c
        zc = lambda off, w: z_ref[r:r + c, off:off + w]

        def get(name, h):
            if name == "C":
                return c_in[i, h]
            if name == "n":
                return n_in[i, pl.ds(h, 1), :]
            if name == "m":
                return m_in[i, pl.ds(h, 1), :]
            if name == "Sr":
                return sr_in[i, h]
            if name == "ShT":
                return sh_in[i, h].T
            return buf_in[i]

        def put(name, h, val):
            if name == "C":
                c_out[i, h] = val
            elif name == "n":
                n_out[i, pl.ds(h, 1), :] = val
            elif name == "m":
                m_out[i, pl.ds(h, 1), :] = val
            elif name == "Sr":
                sr_out[i, h] = val
            elif name == "ShT":
                sh_out[i, h] = val.T
            else:
                buf_out[i] = val

        def put_y(off, val):
            ybuf[r:r + c, off:off + BRANCH_W] = val

        return _mixer_chunk(zc, c, _State(get, put), cst, prm, put_y)

    _pipelined(sequence(i) for i in range(bb))
    y_ref[...] = ybuf[...].astype(BF16)


def _mixer_sample(z, row0, bsz, t, bb, layer, tabs, prms, states, carried):
    c = t
    rows = bb * t
    blk0 = row0 // rows
    cos2, sin2, rdec, rinter, rtoend, rcdec = tabs

    def lspec(*tail):
        zeros = (0,) * (len(tail) - 1)
        return pl.BlockSpec((None,) + tail, lambda i: (layer, i) + zeros)

    big = lspec(bb, N_HEADS, HEAD_DIM, HEAD_DIM)
    small = lspec(bb, N_HEADS, LANES)
    state_specs = [big, small, small, big, big, lspec(bb, CONV_W - 1, BRANCH_W)]
    state_shapes = [jax.ShapeDtypeStruct(s.shape, F32) for s in states]
    n_in = N_MIXER_CONST_INPUTS + N_STATES
    est = (2 * rows * P_PAD * 4 + 2 * rows * Y_W * 2 + rows * Y_W * 4
           + 2 * 2 * 3 * bb * N_HEADS * HEAD_DIM * HEAD_DIM * 4)
    return pl.pallas_call(
        functools.partial(_mixer_sample_kernel, layer, c, bb, len(carried)),
        grid=(bsz // bb,),
        in_specs=[pl.BlockSpec((rows, P_PAD), lambda i: (blk0 + i, 0)),
                  _const_spec(cos2.shape), _const_spec(sin2.shape),
                  _const_spec(rdec.shape), _const_spec(rinter.shape), _const_spec(rtoend.shape),
                  _const_spec(rcdec.shape)] + [_const_spec(p.shape) for p in prms] + state_specs
                 + [pl.BlockSpec(memory_space=pl.ANY)] * len(carried),
        out_specs=[pl.BlockSpec((rows, Y_W), lambda i: (i, 0))] + state_specs,
        out_shape=[jax.ShapeDtypeStruct((bsz * t, Y_W), BF16)] + state_shapes,
        input_output_aliases={n_in + k: 1 + k for k in range(len(carried))},
        scratch_shapes=[pltpu.VMEM((rows, Y_W), F32)],
        compiler_params=pltpu.CompilerParams(dimension_semantics=("arbitrary",),
                                             vmem_limit_bytes=_vmem_limit(est)),
        name=f"mixer_sample_l{layer}",
    )(z, cos2, sin2, rdec, rinter, rtoend, rcdec, *prms, *states, *carried)


def _layer_norm(r, g, b):
    mu = jnp.mean(r, axis=1, keepdims=True)
    rc = r - mu
    var = jnp.mean(rc * rc, axis=1, keepdims=True)
    return rc * lax.rsqrt(var + LN_EPS) * g + b


def _merge_kernel(npb, xp_ref, xs_ref, yp_ref, ys_ref, wb_ref, wg_ref, bg_ref, wo_ref, g_ref, b_ref,
                  hp_ref, hs_ref):
    x = _group_load(npb, xp_ref, xs_ref)
    xb = x.astype(BF16)
    y = _group_load(npb, yp_ref, ys_ref)
    acc = None
    for n in range(N_BRANCH):
        proj = jnp.dot(y[:, n * BRANCH_W:(n + 1) * BRANCH_W], wb_ref[n].astype(BF16), preferred_element_type=F32)
        gate = jax.nn.sigmoid(jnp.dot(xb, wg_ref[:, n * D_MODEL:(n + 1) * D_MODEL].astype(BF16),
                                      preferred_element_type=F32)
                              + bg_ref[:, n * D_MODEL:(n + 1) * D_MODEL])
        acc = gate * proj if acc is None else acc + gate * proj
    mix = jnp.dot(acc.astype(BF16), wo_ref[...].astype(BF16), preferred_element_type=F32)
    _group_store(npb, hp_ref, hs_ref, _layer_norm(ALPHA * x + mix, g_ref[...], b_ref[...]))


def _layer_spec(layer, *tail):
    return pl.BlockSpec((None,) + tail, lambda *_: (layer,) + (0,) * len(tail))


def _group_out(xp, xs, tm, npb, nsb):
    return dict(out_specs=_group_specs(tm, D_MODEL, npb, nsb),
                out_shape=[jax.ShapeDtypeStruct(xp.shape, F32), jax.ShapeDtypeStruct(xs.shape, F32)])


def _merge(xp, xs, y_p, y_s, layer, wb, wg, bg, wo, g, b, tm):
    npb, nsb = xp.shape[0] // tm, xs.shape[0] // tm
    est = ((N_BRANCH * BRANCH_W * D_MODEL + D_MODEL * N_BRANCH * D_MODEL + D_MODEL * D_MODEL) * 4
           + 2 * 2 * tm * D_MODEL * 4 * 2 + 2 * 2 * tm * Y_W * 2 + 8 * tm * D_MODEL * 4)
    return pl.pallas_call(
        functools.partial(_merge_kernel, npb),
        grid=(npb + nsb,),
        in_specs=_group_specs(tm, D_MODEL, npb, nsb) + _group_specs(tm, Y_W, npb, nsb)
                 + [_layer_spec(layer, N_BRANCH, BRANCH_W, D_MODEL), _layer_spec(layer, D_MODEL, N_BRANCH * D_MODEL),
                    _const_spec(bg.shape), _layer_spec(layer, D_MODEL, D_MODEL),
                    _const_spec(g.shape), _const_spec(b.shape)],
        compiler_params=pltpu.CompilerParams(dimension_semantics=("arbitrary",),
                                             vmem_limit_bytes=_vmem_limit(est)),
        name="merge",
        **_group_out(xp, xs, tm, npb, nsb),
    )(xp, xs, y_p, y_s, wb, wg, bg, wo, g, b)


FFN_CHUNK = 1024


def _ffn_kernel(npb, hp_ref, hs_ref, wu_ref, wd_ref, g_ref, b_ref, op_ref, os_ref):
    h = _group_load(npb, hp_ref, hs_ref)
    hb = h.astype(BF16)
    ff = None
    for j in range(0, D_FF, FFN_CHUNK):
        u = jnp.maximum(jnp.dot(hb, wu_ref[:, j:j + FFN_CHUNK].astype(BF16), preferred_element_type=F32), 0.0)
        part = jnp.dot((u * u).astype(BF16), wd_ref[j:j + FFN_CHUNK, :].astype(BF16), preferred_element_type=F32)
        ff = part if ff is None else ff + part
    _group_store(npb, op_ref, os_ref, _layer_norm(ALPHA * h + ff, g_ref[...], b_ref[...]))


def _ffn(hp, hs, layer, wu, wd, g, b, tm):
    npb, nsb = hp.shape[0] // tm, hs.shape[0] // tm
    est = (2 * D_MODEL * D_FF * 4 + 2 * 2 * 2 * tm * D_MODEL * 4 + tm * FFN_CHUNK * 6 + 4 * tm * D_MODEL * 4
           + 2 * D_MODEL * FFN_CHUNK * 2)
    return pl.pallas_call(
        functools.partial(_ffn_kernel, npb),
        grid=(npb + nsb,),
        in_specs=_group_specs(tm, D_MODEL, npb, nsb)
                 + [_layer_spec(layer, D_MODEL, D_FF), _layer_spec(layer, D_FF, D_MODEL),
                    _const_spec(g.shape), _const_spec(b.shape)],
        compiler_params=pltpu.CompilerParams(dimension_semantics=("arbitrary",),
                                             vmem_limit_bytes=_vmem_limit(est)),
        name="ffn",
        **_group_out(hp, hs, tm, npb, nsb),
    )(hp, hs, wu, wd, g, b)


def _pad_lanes(a):
    return jnp.pad(a, ((0, 0), (0, LANES - a.shape[1])))


def _position_tables(pos, c):
    half = HEAD_DIM // 2
    inv = ROPE_BASE ** (-jnp.arange(half, dtype=F32) / half)
    ang = pos[:, None] * inv[None, :]
    cos, sin = jnp.cos(ang), jnp.sin(ang)
    cos2 = jnp.concatenate([cos, cos], axis=1)
    sin2 = jnp.concatenate([-sin, sin], axis=1)
    log_gamma = jnp.log(1.0 - 2.0 ** (-5.0 - jnp.arange(N_HEADS, dtype=F32)))
    idx = jnp.arange(c, dtype=F32)
    rel = idx[:, None] - idx[None, :]
    dec = jnp.where(rel >= 0, jnp.exp(log_gamma[:, None, None] * jnp.maximum(rel, 0.0)), 0.0)
    inter = jnp.swapaxes(jnp.exp(log_gamma[:, None] * (idx + 1.0)[None, :]), 0, 1)
    to_end = jnp.swapaxes(jnp.exp(log_gamma[:, None] * (c - 1.0 - idx)[None, :]), 0, 1)
    cdec = jnp.broadcast_to(jnp.exp(log_gamma * c)[:, None], (N_HEADS, LANES))
    cdec = jnp.pad(cdec, ((0, SUBLANES - N_HEADS), (0, 0)))
    lanes = lambda a: jnp.broadcast_to(jnp.swapaxes(a, 0, 1)[:, :, None], (N_HEADS, c, LANES))
    return cos2, sin2, dec, lanes(inter), lanes(to_end), cdec


def _gla_tables(c):
    r = np.arange(c)[:, None]
    j = np.arange(c)[None, :]
    mats = [j <= r]
    n = c // 2
    while n > 1:
        start = (r // n) * n
        odd = (r // n) % 2 == 1
        mats.append(np.where(odd, (j >= start) & (j <= r), (j > r) & (j <= start + n - 1)))
        n //= 2
    x = np.maximum(r ^ j, 1)
    level = np.where(j > r, -1, np.where(j == r, 0, 1 << np.floor(np.log2(x)).astype(np.int64)))
    return jnp.asarray(np.tile(np.concatenate(mats, 0), (1, 3)), BF16), jnp.asarray(level, jnp.int32)


def kernel(x_prompt, x_sample, state_mlstm_C, state_mlstm_n, state_mlstm_m, state_ret, state_hgrn, state_conv,
           w_in, b_mlstm_gate, norm_a, norm_b, norm_c, lb_c, conv_w, w_branch, w_gate, b_gate, w_out,
           ln1_g, ln1_b, ln2_g, ln2_b, w_up, w_down):
    bp, tp, _ = x_prompt.shape
    bs, ts, _ = x_sample.shape
    n_p, n_s = bp * tp, bs * ts
    tm = 256
    tt = 256
    bb = 8
    assert n_p % tm == 0 and n_s % tm == 0 and tp % tt == 0 and bs % bb == 0 and n_p % (bb * ts) == 0

    xp = x_prompt.reshape(n_p, D_MODEL)
    xs = x_sample.reshape(n_s, D_MODEL)
    tabs_p = _position_tables(jnp.arange(tp, dtype=F32), math.gcd(tp, CHUNK))
    tabs_s = _position_tables(PAST_LEN + jnp.arange(ts, dtype=F32), math.gcd(ts, CHUNK))
    gla_p = _gla_tables(math.gcd(tp, CHUNK))
    gla_s = _gla_tables(math.gcd(ts, CHUNK))
    lbc = lb_c.astype(F32)
    row = lambda a: a.reshape(1, -1).astype(F32)
    assert w_in.shape[1:] == (D_MODEL, P_IN)
    w_in_t = jnp.swapaxes(w_in, 1, 2).astype(F32)
    wb, wg, wo, wu, wd = (w.astype(F32) for w in (w_branch, w_gate, w_out, w_up, w_down))
    st_in = (state_mlstm_C.astype(F32), state_mlstm_n.astype(F32),
             jnp.broadcast_to(state_mlstm_m.astype(F32)[..., None], (DEPTH, bs, N_HEADS, LANES)),
             state_ret.astype(F32), state_hgrn.astype(F32), state_conv.astype(F32))

    p_states, s_states = [], ()
    for l in range(DEPTH):
        prms = (_pad_lanes(row(b_mlstm_gate[l])), row(norm_a[l]), row(norm_b[l]), row(norm_c[l]), lbc,
                conv_w[l].astype(F32))
        z = _inproj(xp, xs, l, w_in_t, tm)
        outs_p = _mixer_prompt(z, bp, tp, tt, l, tabs_p, prms + gla_p)
        outs_s = _mixer_sample(z, n_p, bs, ts, bb, l, tabs_s, prms + gla_s, st_in, s_states)
        hp, hs = _merge(xp, xs, outs_p[0], outs_s[0], l, wb, wg, row(b_gate[l]), wo, row(ln1_g[l]), row(ln1_b[l]), tm)
        xp, xs = _ffn(hp, hs, l, wu, wd, row(ln2_g[l]), row(ln2_b[l]), tm)
        p_states.append(outs_p[1:])
        s_states = tuple(outs_s[1:])

    cs, ns, ms, srs, shs, bufs = zip(*p_states)
    p_out = (jnp.stack(cs), jnp.stack(ns), jnp.stack([m[..., 0] for m in ms]),
             jnp.stack(srs), jnp.stack(shs), jnp.stack(bufs))
    s_out = s_states[:2] + (s_states[2][..., 0],) + s_states[3:]
    return (xp.reshape(bp, tp, D_MODEL), xs.reshape(bs, ts, D_MODEL)) + p_out + s_out
```

```python
import functools
import math

import jax
import jax.numpy as jnp
import numpy as np
from jax import lax
from jax.experimental import pallas as pl
from jax.experimental.pallas import tpu as pltpu

F32 = jnp.float32
BF16 = jnp.bfloat16

D_MODEL = 1024
DEPTH = 2
BRANCH_W = D_MODEL // 2
N_HEADS = 4
HEAD_DIM = BRANCH_W // N_HEADS
CONV_W = 3
N_BRANCH = 4
D_FF = 4 * D_MODEL
CHUNK = 64
ROPE_BASE = 10000.0
LN_EPS = 1e-5
NEG_BIG = -1e30
GATE_CLAMP = 1.0 - 1e-6
ALPHA = (2 * DEPTH) ** 0.25
PAST_LEN = 16384
K_SCALE = HEAD_DIM ** -0.5

LANES = 128
SUBLANES = 8
VMEM_BYTES = 64 * 1024 * 1024

(QA, KA, VA, OA, QB, KB, VB, GB, QC, FC, IC, GC, XD, BGD, CGD) = [i * BRANCH_W for i in range(15)]
GATE_OFF = 15 * BRANCH_W
P_PAD = GATE_OFF + LANES
Y_W = N_BRANCH * BRANCH_W


def _vmem_limit(est_bytes):
    return int(min(est_bytes + est_bytes // 4 + (4 << 20), VMEM_BYTES - (6 << 20)))


def _dot(a, b):
    return jnp.dot(a.astype(BF16), b.astype(BF16), preferred_element_type=F32)


def _dot_nt(a, b):
    return lax.dot_general(a.astype(BF16), b.astype(BF16), (((1,), (1,)), ((), ())),
                           preferred_element_type=F32)


def _dot_tn(a, b):
    return lax.dot_general(a.astype(BF16), b.astype(BF16), (((0,), (0,)), ((), ())),
                           preferred_element_type=F32)


def _split3(x):
    hi = x.astype(BF16)
    r = x - hi.astype(F32)
    mid = r.astype(BF16)
    lo = (r - mid.astype(F32)).astype(BF16)
    return hi, mid, lo


def _dot_exact_rhs(a01x3, x):
    return jnp.dot(a01x3, _stack3(x), preferred_element_type=F32)


def _stack3(x):
    parts = _split3(x)
    if x.shape[0] % (2 * SUBLANES) == 0:
        return jnp.concatenate(parts, axis=0)
    return jnp.concatenate([p.astype(F32) for p in parts], axis=0).astype(BF16)


def _dot_nt_exact_rhs(a01, x):
    hi, mid, lo = _split3(x)
    d = lambda p: lax.dot_general(a01, p, (((1,), (1,)), ((), ())), preferred_element_type=F32)
    return (d(hi) + d(mid)) + d(lo)


def _silu(x):
    return x * jax.nn.sigmoid(x)


def _row_sum(x, c):
    if c <= SUBLANES:
        return jnp.sum(x, axis=1, keepdims=True)
    hi = x.astype(BF16)
    lo = (x - hi.astype(F32)).astype(BF16)
    ones = jnp.ones((2 * x.shape[1], LANES), BF16)
    return jnp.dot(jnp.concatenate([hi, lo], axis=1), ones, preferred_element_type=F32)


def _group_specs(tm, width, npb, nsb):
    return [pl.BlockSpec((tm, width), lambda i: (jnp.minimum(i, npb - 1), 0)),
            pl.BlockSpec((tm, width), lambda i: (jnp.clip(i - npb, 0, nsb - 1), 0))]


def _group_load(npb, p_ref, s_ref):
    return jnp.where(pl.program_id(0) < npb, p_ref[...], s_ref[...])


def _group_store(npb, p_ref, s_ref, val):
    is_p = pl.program_id(0) < npb

    @pl.when(is_p)
    def _():
        p_ref[...] = val

    @pl.when(jnp.logical_not(is_p))
    def _():
        s_ref[...] = val


P_IN = 15 * BRANCH_W + 2 * N_HEADS
GATE_FEATURES = (4 * BRANCH_W, 4 * BRANCH_W + 2 * N_HEADS)


def _inproj_kernel(npb, xp_ref, xs_ref, wt_ref, z_ref):
    xb = _group_load(npb, xp_ref, xs_ref).astype(BF16)
    nt = lambda w: lax.dot_general(xb, w.astype(BF16), (((1,), (1,)), ((), ())), preferred_element_type=F32)
    g0, g1 = GATE_FEATURES
    for off in range(0, GATE_OFF, BRANCH_W):
        src = off if off < g0 else off + (g1 - g0)
        z_ref[:, off:off + BRANCH_W] = nt(wt_ref[src:src + BRANCH_W, :])
    gates = nt(wt_ref[g0:g1, :])
    z_ref[:, GATE_OFF:P_PAD] = jnp.concatenate(
        [gates, jnp.zeros((gates.shape[0], P_PAD - GATE_OFF - (g1 - g0)), F32)], axis=1)


def _inproj(xp, xs, layer, wt, tm):
    npb, nsb = xp.shape[0] // tm, xs.shape[0] // tm
    est = D_MODEL * P_IN * 4 + 2 * tm * P_PAD * 4 + 2 * 2 * tm * D_MODEL * 4
    return pl.pallas_call(
        functools.partial(_inproj_kernel, npb),
        grid=(npb + nsb,),
        in_specs=_group_specs(tm, D_MODEL, npb, nsb) + [_layer_spec(layer, P_IN, D_MODEL)],
        out_specs=pl.BlockSpec((tm, P_PAD), lambda i: (i, 0)),
        out_shape=jax.ShapeDtypeStruct(((npb + nsb) * tm, P_PAD), F32),
        compiler_params=pltpu.CompilerParams(dimension_semantics=("arbitrary",),
                                             vmem_limit_bytes=_vmem_limit(est)),
        name="inproj",
    )(xp, xs, wt)


class _State:
    def __init__(self, get, put, hgrn_transposed):
        self.get = get
        self.put = put
        self.hgrn_transposed = hgrn_transposed


LEVEL1_DONE, STATE_STORED = "level 1 done", "state stored"


def _advance(g, until=None):
    for label in g:
        if until is not None and label == until:
            return


def _pipelined(chunk_generators):
    pending_tail = None
    for g in chunk_generators:
        _advance(g, LEVEL1_DONE)
        if pending_tail is not None:
            _advance(pending_tail)
        _advance(g, STATE_STORED)
        pending_tail = g
    _advance(pending_tail)


def _mixer_chunk(zc, c, st, cst, prm, put_y):
    tril, causal, sel8 = cst["tril"], cst["causal"], cst["sel8"]
    old = {(name, h): st.get(name, h) for name in ("C", "n", "m", "Sr", "ShT") for h in range(N_HEADS)}
    old[("buf", 0)] = st.get("buf", 0)
    new = {}

    heads = range(N_HEADS)
    hd = lambda off, h: zc(off + h * HEAD_DIM, HEAD_DIM)
    hsl = lambda a, h: a[:, h * HEAD_DIM:(h + 1) * HEAD_DIM]
    cos2, sin2 = cst["cos2"], cst["sin2"]
    rope = lambda x: x * cos2 + pltpu.roll(x, HEAD_DIM // 2, axis=1) * sin2
    lv = cst["gla_lv"][...]

    gp = zc(GATE_OFF, LANES) + prm["bias_if"]
    lf = jnp.minimum(gp, 0.0) - jnp.log1p(jnp.exp(-jnp.abs(gp)))
    b_all = _dot_exact_rhs(tril, lf)
    gp_t = _dot_nt_exact_rhs(sel8, gp)
    s_a, qc_a, s_b, qs_b, a_c = [], [], [], [], []
    for h in heads:
        qa = hd(QA, h)
        s_a.append(_dot_nt(qa, hd(KA, h) * K_SCALE))
        qc_a.append(_dot(qa, old[("C", h)]))
    for h in heads:
        qb = rope(hd(QB, h))
        kb = rope(hd(KB, h)) * K_SCALE
        s_b.append(_dot_nt(qb, kb))
        qs_b.append(_dot(qb, old[("Sr", h)]))
        new[("Sr", h)] = (cst["ret_cdec"][h:h + 1, :] * old[("Sr", h)]
                          + _dot_tn(kb * cst["ret_toend"](h), hd(VB, h)))
    kc_all = (1.0 - prm["lb"]) * jax.nn.sigmoid(-zc(FC, BRANCH_W))
    g_all = jnp.log1p(-jnp.minimum(kc_all, GATE_CLAMP))
    xs = _dot_exact_rhs(cst["gla_m"][...], g_all)
    bc_all = xs[0:c]
    qg_all = _silu(zc(QC, BRANCH_W))
    for h in heads:
        qg, kc = hsl(qg_all, h), hsl(kc_all, h)
        a = jnp.where(lv == 0, _dot_nt(qg, kc), 0.0)
        a_c.append(jnp.where(lv == 1, _dot_nt(qg * jnp.exp(hsl(g_all, h)), kc), a))

    yield LEVEL1_DONE
    b_t = _dot_nt_exact_rhs(sel8, b_all)
    qn_a = [jnp.sum(hd(QA, h) * old[("n", h)], axis=1, keepdims=True) for h in heads]
    o_b = []
    for h in heads:
        o_b.append(_dot(s_b[h] * cst["ret_dec"](h), hd(VB, h)) + qs_b[h] * cst["ret_inter"](h))
    num_a, den_a, oi_c = [], [], []
    for h in heads:
        b_col = b_all[:, N_HEADS + h:N_HEADS + h + 1]
        b_row = b_t[N_HEADS + h:N_HEADS + h + 1, :]
        a_col = b_col + old[("m", h)][:, 0:1]
        dmat = jnp.where(causal, (b_col - b_row) + gp_t[h:h + 1, :], NEG_BIG)
        m_t = jnp.maximum(a_col, jnp.max(dmat, axis=1, keepdims=True))
        w_inter = jnp.exp(a_col - m_t)
        s = s_a[h] * jnp.exp(dmat - m_t)
        num_a.append(_dot(s, hd(VA, h)) + qc_a[h] * w_inter)
        den = jnp.sum(s, axis=1, keepdims=True) + qn_a[h] * w_inter
        den_a.append(jnp.maximum(jnp.abs(den), jnp.exp(-m_t)))
    for h in heads:
        b_col = b_all[:, N_HEADS + h:N_HEADS + h + 1]
        m_prev = old[("m", h)][:, 0:1]
        b_last = b_col[c - 1:c, :]
        g_col = (b_last - b_col) + gp[:, h:h + 1]
        m_new = jnp.maximum(b_last + m_prev, jnp.max(g_col, axis=0, keepdims=True))
        dec = jnp.exp(b_last + m_prev - m_new)
        kw = (hd(KA, h) * K_SCALE) * jnp.exp(g_col - m_new)
        new[("C", h)] = dec * old[("C", h)] + _dot_tn(kw, hd(VA, h))
        new[("n", h)] = dec * old[("n", h)] + jnp.sum(kw, axis=0, keepdims=True)
        new[("m", h)] = jnp.broadcast_to(m_new, (1, LANES))
    for h in heads:
        qg, kc, bc = hsl(qg_all, h), hsl(kc_all, h), hsl(bc_all, h)
        a, sec, n = a_c[h], 1, c // 2
        while n > 1:
            e = jnp.exp(hsl(xs[sec * c:(sec + 1) * c], h))
            a = jnp.where(lv == n, _dot_nt(qg * e, kc * e), a)
            sec, n = sec + 1, n // 2
        a_c[h] = a
        sh = old[("ShT", h)]
        bc_last = bc[c - 1:c, :]
        k_end = kc * jnp.exp(bc_last - bc)
        if st.hgrn_transposed:
            oi_c.append(_dot_nt(qg * jnp.exp(bc), sh))
            new[("ShT", h)] = jnp.exp(bc_last) * sh + _dot_tn(hd(IC, h), k_end)
        else:
            oi_c.append(_dot(qg * jnp.exp(bc), sh))
            ones = jnp.ones((3 * c, HEAD_DIM), BF16)
            log_dec = lax.dot_general(_stack3(hsl(g_all, h)), ones, (((0,), (0,)), ((), ())),
                                      preferred_element_type=F32)
            new[("ShT", h)] = jnp.exp(log_dec) * sh + _dot_tn(k_end, hd(IC, h))
    o_c = [_dot(a_c[h], hd(IC, h)) + oi_c[h] for h in heads]
    u = zc(CGD, BRANCH_W) * zc(XD, BRANCH_W)
    new[("buf", 0)] = u[c - 2:c, :]
    for (name, h), val in new.items():
        st.put(name, h, val)

    yield STATE_STORED
    ya, yb, yc = [], [], []
    lane_mean = lambda x: _row_sum(x, c) * (1.0 / x.shape[1])
    for h in heads:
        hh = num_a[h] / den_a[h]
        hn = hh * lax.rsqrt(lane_mean(hh * hh) + LN_EPS)
        ya.append(jax.nn.sigmoid(hd(OA, h)) * (hn * hsl(prm["norm_a"], h)))
    put_y(0, jnp.concatenate(ya, axis=1))
    buf = old[("buf", 0)]
    row = lax.broadcasted_iota(jnp.int32, (c, BRANCH_W), 0)
    u1 = jnp.where(row == 0, buf[1:2, :], pltpu.roll(u, 1, axis=0))
    u2 = jnp.where(row == 0, buf[0:1, :], jnp.where(row == 1, buf[1:2, :], pltpu.roll(u, 2, axis=0)))
    cw = prm["conv_w"]
    yconv = (u2 * cw[0:1, :] + u1 * cw[1:2, :]) + u * cw[2:3, :]
    put_y(3 * BRANCH_W, zc(BGD, BRANCH_W) * yconv)
    for h in heads:
        oc = o_b[h] - lane_mean(o_b[h])
        on = oc * lax.rsqrt(lane_mean(oc * oc) + LN_EPS)
        yb.append(_silu(hd(GB, h)) * (on * hsl(prm["norm_b"], h)))
    put_y(BRANCH_W, jnp.concatenate(yb, axis=1))
    for h in heads:
        on = o_c[h] * lax.rsqrt(lane_mean(o_c[h] * o_c[h]) + LN_EPS)
        yc.append(_silu(hd(GC, h)) * (on * hsl(prm["norm_c"], h)))
    put_y(2 * BRANCH_W, jnp.concatenate(yc, axis=1))


def _chunk_consts(c):
    r = lax.broadcasted_iota(jnp.int32, (c, c), 0)
    l = lax.broadcasted_iota(jnp.int32, (c, c), 1)
    causal = r >= l
    r3 = lax.broadcasted_iota(jnp.int32, (c, 3 * c), 0)
    l3 = lax.broadcasted_iota(jnp.int32, (c, 3 * c), 1)
    l3 = jnp.where(l3 >= 2 * c, l3 - 2 * c, jnp.where(l3 >= c, l3 - c, l3))
    r8 = lax.broadcasted_iota(jnp.int32, (SUBLANES, LANES), 0)
    l8 = lax.broadcasted_iota(jnp.int32, (SUBLANES, LANES), 1)
    return {"causal": causal,
            "tril": jnp.where(r3 >= l3, 1.0, 0.0).astype(BF16),
            "sel8": jnp.where(r8 == l8, 1.0, 0.0).astype(BF16)}


def _layer_lb(lbc, layer):
    e = jnp.exp(lbc - jnp.max(lbc, axis=0, keepdims=True))
    sm = e / jnp.sum(e, axis=0, keepdims=True)
    cum = sm[0:1, :]
    for j in range(1, layer + 1):
        cum = cum + sm[j:j + 1, :]
    return cum - sm[0:1, :]


def _load_params(bias_ref, na_ref, nb_ref, nc_ref, lbc_ref, cw_ref, layer):
    return {"bias_if": bias_ref[...], "norm_a": na_ref[...], "norm_b": nb_ref[...], "norm_c": nc_ref[...],
            "lb": _layer_lb(lbc_ref[...], layer), "conv_w": cw_ref[...]}


PROMPT_LOCKSTEP = 1


def _mixer_prompt_kernel(layer, c, tt, *refs):
    ns = PROMPT_LOCKSTEP
    z_refs = refs[:ns]
    (cos_ref, sin_ref, rdec_ref, rinter_ref, rtoend_ref, rcdec_ref,
     bias_ref, na_ref, nb_ref, nc_ref, lbc_ref, cw_ref, gm_ref, lv_ref,
     y_ref, c_out, n_out, m_out, sr_out, sh_out, buf_out,
     c_s, n_s, m_s, sr_s, sht_s, buf_s) = refs[ns:]
    j = pl.program_id(1)

    @pl.when(j == 0)
    def _():
        c_s[...] = jnp.zeros_like(c_s)
        n_s[...] = jnp.zeros_like(n_s)
        m_s[...] = jnp.zeros_like(m_s)
        sr_s[...] = jnp.zeros_like(sr_s)
        sht_s[...] = jnp.zeros_like(sht_s)
        buf_s[...] = jnp.zeros_like(buf_s)

    prm = _load_params(bias_ref, na_ref, nb_ref, nc_ref, lbc_ref, cw_ref, layer)
    cst = _chunk_consts(c)
    cst["ret_dec"] = lambda h: rdec_ref[h]
    cst["ret_inter"] = lambda h: rinter_ref[h]
    cst["ret_toend"] = lambda h: rtoend_ref[h]
    cst["ret_cdec"] = rcdec_ref[...]
    cst["gla_m"] = gm_ref
    cst["gla_lv"] = lv_ref

    def state(s):
        def get(name, h):
            if name == "C":
                return c_s[s, h]
            if name == "n":
                return n_s[s, h:h + 1, :]
            if name == "m":
                return m_s[s, h:h + 1, :]
            if name == "Sr":
                return sr_s[s, h]
            if name == "ShT":
                return sht_s[s, h]
            return buf_s[s, 0:CONV_W - 1, :]

        def put(name, h, val):
            if name == "C":
                c_s[s, h] = val
            elif name == "n":
                n_s[s, h:h + 1, :] = val
            elif name == "m":
                m_s[s, h:h + 1, :] = val
            elif name == "Sr":
                sr_s[s, h] = val
            elif name == "ShT":
                sht_s[s, h] = val
            else:
                buf_s[s, 0:CONV_W - 1, :] = val

        return _State(get, put, hgrn_transposed=True)

    def chunk(s, k):
        r = k * c
        cc = dict(cst)
        cc["cos2"] = cos_ref[r:r + c, :]
        cc["sin2"] = sin_ref[r:r + c, :]
        zc = lambda off, w: z_refs[s][r:r + c, off:off + w]

        def put_y(off, val):
            y_ref[s, r:r + c, off:off + BRANCH_W] = val.astype(BF16)

        return _mixer_chunk(zc, c, state(s), cc, prm, put_y)

    for s in range(ns):
        _pipelined(chunk(s, k) for k in range(tt // c))

    @pl.when(j == pl.num_programs(1) - 1)
    def _():
        c_out[...] = c_s[...]
        n_out[...] = n_s[:, 0:N_HEADS, :]
        m_out[...] = m_s[:, 0:N_HEADS, :]
        sr_out[...] = sr_s[...]
        for s in range(ns):
            for h in range(N_HEADS):
                sh_out[s, h] = sht_s[s, h].T
        buf_out[...] = buf_s[:, 0:CONV_W - 1, :]


def _const_spec(shape):
    nd = len(shape)
    return pl.BlockSpec(shape, lambda *_: (0,) * nd)


def _mixer_prompt(z, bsz, t, tt, layer, tabs, prms):
    c = math.gcd(t, CHUNK)
    nj = t // tt
    ns = PROMPT_LOCKSTEP
    assert bsz % ns == 0
    cos2, sin2, rdec, rinter, rtoend, rcdec = tabs
    state_shapes = [
        jax.ShapeDtypeStruct((bsz, N_HEADS, HEAD_DIM, HEAD_DIM), F32),
        jax.ShapeDtypeStruct((bsz, N_HEADS, HEAD_DIM), F32),
        jax.ShapeDtypeStruct((bsz, N_HEADS, LANES), F32),
        jax.ShapeDtypeStruct((bsz, N_HEADS, HEAD_DIM, HEAD_DIM), F32),
        jax.ShapeDtypeStruct((bsz, N_HEADS, HEAD_DIM, HEAD_DIM), F32),
        jax.ShapeDtypeStruct((bsz, CONV_W - 1, BRANCH_W), F32),
    ]
    big = pl.BlockSpec((ns, N_HEADS, HEAD_DIM, HEAD_DIM), lambda g, j: (g, 0, 0, 0))
    small = pl.BlockSpec((ns, N_HEADS, LANES), lambda g, j: (g, 0, 0))
    est = (2 * ns * tt * P_PAD * 4 + 2 * ns * tt * Y_W * 2 + 10 * ns * N_HEADS * HEAD_DIM * HEAD_DIM * 4 + (6 << 20))

    def z_spec(s, width):
        return pl.BlockSpec((tt, width), lambda g, j: ((g * ns + s) * nj + j, 0))

    outs = pl.pallas_call(
        functools.partial(_mixer_prompt_kernel, layer, c, tt),
        grid=(bsz // ns, nj),
        in_specs=[z_spec(s, P_PAD) for s in range(ns)]
                 + [pl.BlockSpec((tt, LANES), lambda g, j: (j, 0)),
                    pl.BlockSpec((tt, LANES), lambda g, j: (j, 0)),
                    _const_spec(rdec.shape), _const_spec(rinter.shape), _const_spec(rtoend.shape),
                    _const_spec(rcdec.shape)] + [_const_spec(p.shape) for p in prms],
        out_specs=[pl.BlockSpec((None, ns, tt, Y_W), lambda g, j: (g, 0, j, 0)),
                   big, small, small, big, big,
                   pl.BlockSpec((ns, CONV_W - 1, BRANCH_W), lambda g, j: (g, 0, 0))],
        out_shape=[jax.ShapeDtypeStruct((bsz // ns, ns, t, Y_W), BF16)] + state_shapes,
        scratch_shapes=[pltpu.VMEM((ns, N_HEADS, HEAD_DIM, HEAD_DIM), F32),
                        pltpu.VMEM((ns, SUBLANES, LANES), F32),
                        pltpu.VMEM((ns, SUBLANES, LANES), F32),
                        pltpu.VMEM((ns, N_HEADS, HEAD_DIM, HEAD_DIM), F32),
                        pltpu.VMEM((ns, N_HEADS, HEAD_DIM, HEAD_DIM), F32),
                        pltpu.VMEM((ns, SUBLANES, BRANCH_W), F32)],
        compiler_params=pltpu.CompilerParams(dimension_semantics=("arbitrary", "arbitrary"),
                                             vmem_limit_bytes=_vmem_limit(est)),
        name=f"mixer_prompt_l{layer}",
    )(*([z] * ns), cos2, sin2, rdec, rinter, rtoend, rcdec, *prms)
    return [outs[0].reshape(bsz * t, Y_W)] + list(outs[1:])


N_STATES = 6
N_MIXER_CONST_INPUTS = 15


def _mixer_sample_kernel(layer, c, bb, n_carried, *refs):
    (z_ref, cos_ref, sin_ref, rdec_ref, rinter_ref, rtoend_ref, rcdec_ref,
     bias_ref, na_ref, nb_ref, nc_ref, lbc_ref, cw_ref, gm_ref, lv_ref) = refs[:N_MIXER_CONST_INPUTS]
    c_in, n_in, m_in, sr_in, sh_in, buf_in = refs[N_MIXER_CONST_INPUTS:N_MIXER_CONST_INPUTS + N_STATES]
    y_ref, c_out, n_out, m_out, sr_out, sh_out, buf_out, ybuf = refs[N_MIXER_CONST_INPUTS + N_STATES + n_carried:]
    prm = _load_params(bias_ref, na_ref, nb_ref, nc_ref, lbc_ref, cw_ref, layer)
    cst = _chunk_consts(c)
    cst["ret_dec"] = lambda h: rdec_ref[h]
    cst["ret_inter"] = lambda h: rinter_ref[h]
    cst["ret_toend"] = lambda h: rtoend_ref[h]
    cst["ret_cdec"] = rcdec_ref[...]
    cst["gla_m"] = gm_ref
    cst["gla_lv"] = lv_ref
    cst["cos2"] = cos_ref[...]
    cst["sin2"] = sin_ref[...]

    def sequence(i):
        r = i * c
        zc = lambda off, w: z_ref[r:r + c, off:off + w]

        def get(name, h):
            if name == "C":
                return c_in[i, h]
            if name == "n":
                return n_in[i, pl.ds(h, 1), :]
            if name == "m":
                return m_in[i, pl.ds(h, 1), :]
            if name == "Sr":
                return sr_in[i, h]
            if name == "ShT":
                return sh_in[i, h]
            return buf_in[i]

        def put(name, h, val):
            if name == "C":
                c_out[i, h] = val
            elif name == "n":
                n_out[i, pl.ds(h, 1), :] = val
            elif name == "m":
                m_out[i, pl.ds(h, 1), :] = val
            elif name == "Sr":
                sr_out[i, h] = val
            elif name == "ShT":
                sh_out[i, h] = val
            else:
                buf_out[i] = val

        def put_y(off, val):
            ybuf[r:r + c, off:off + BRANCH_W] = val

        return _mixer_chunk(zc, c, _State(get, put, hgrn_transposed=False), cst, prm, put_y)

    _pipelined(sequence(i) for i in range(bb))
    y_ref[...] = ybuf[...].astype(BF16)


def _mixer_sample(z, row0, bsz, t, bb, layer, tabs, prms, states, carried):
    c = t
    rows = bb * t
    blk0 = row0 // rows
    cos2, sin2, rdec, rinter, rtoend, rcdec = tabs

    def lspec(*tail):
        zeros = (0,) * (len(tail) - 1)
        return pl.BlockSpec((None,) + tail, lambda i: (layer, i) + zeros)

    big = lspec(bb, N_HEADS, HEAD_DIM, HEAD_DIM)
    small = lspec(bb, N_HEADS, LANES)
    state_specs = [big, small, small, big, big, lspec(bb, CONV_W - 1, BRANCH_W)]
    state_shapes = [jax.ShapeDtypeStruct(s.shape, F32) for s in states]
    n_in = N_MIXER_CONST_INPUTS + N_STATES
    est = (2 * rows * P_PAD * 4 + 2 * rows * Y_W * 2 + rows * Y_W * 4
           + 2 * 2 * 3 * bb * N_HEADS * HEAD_DIM * HEAD_DIM * 4)
    return pl.pallas_call(
        functools.partial(_mixer_sample_kernel, layer, c, bb, len(carried)),
        grid=(bsz // bb,),
        in_specs=[pl.BlockSpec((rows, P_PAD), lambda i: (blk0 + i, 0)),
                  _const_spec(cos2.shape), _const_spec(sin2.shape),
                  _const_spec(rdec.shape), _const_spec(rinter.shape), _const_spec(rtoend.shape),
                  _const_spec(rcdec.shape)] + [_const_spec(p.shape) for p in prms] + state_specs
                 + [pl.BlockSpec(memory_space=pl.ANY)] * len(carried),
        out_specs=[pl.BlockSpec((rows, Y_W), lambda i: (i, 0))] + state_specs,
        out_shape=[jax.ShapeDtypeStruct((bsz * t, Y_W), BF16)] + state_shapes,
        input_output_aliases={n_in + k: 1 + k for k in range(len(carried))},
        scratch_shapes=[pltpu.VMEM((rows, Y_W), F32)],
        compiler_params=pltpu.CompilerParams(dimension_semantics=("arbitrary",),
                                             vmem_limit_bytes=_vmem_limit(est)),
        name=f"mixer_sample_l{layer}",
    )(z, cos2, sin2, rdec, rinter, rtoend, rcdec, *prms, *states, *carried)


def _layer_norm(r, g, b):
    mu = jnp.mean(r, axis=1, keepdims=True)
    rc = r - mu
    var = jnp.mean(rc * rc, axis=1, keepdims=True)
    return rc * lax.rsqrt(var + LN_EPS) * g + b


def _merge_kernel(npb, xp_ref, xs_ref, yp_ref, ys_ref, wb_ref, wg_ref, bg_ref, wo_ref, g_ref, b_ref,
                  hp_ref, hs_ref):
    x = _group_load(npb, xp_ref, xs_ref)
    xb = x.astype(BF16)
    y = _group_load(npb, yp_ref, ys_ref)
    acc = None
    for n in range(N_BRANCH):
        proj = jnp.dot(y[:, n * BRANCH_W:(n + 1) * BRANCH_W], wb_ref[n].astype(BF16), preferred_element_type=F32)
        gate = jax.nn.sigmoid(jnp.dot(xb, wg_ref[:, n * D_MODEL:(n + 1) * D_MODEL].astype(BF16),
                                      preferred_element_type=F32)
                              + bg_ref[:, n * D_MODEL:(n + 1) * D_MODEL])
        acc = gate * proj if acc is None else acc + gate * proj
    mix = jnp.dot(acc.astype(BF16), wo_ref[...].astype(BF16), preferred_element_type=F32)
    _group_store(npb, hp_ref, hs_ref, _layer_norm(ALPHA * x + mix, g_ref[...], b_ref[...]))


def _layer_spec(layer, *tail):
    return pl.BlockSpec((None,) + tail, lambda *_: (layer,) + (0,) * len(tail))


def _group_out(xp, xs, tm, npb, nsb):
    return dict(out_specs=_group_specs(tm, D_MODEL, npb, nsb),
                out_shape=[jax.ShapeDtypeStruct(xp.shape, F32), jax.ShapeDtypeStruct(xs.shape, F32)])


def _merge(xp, xs, y_p, y_s, layer, wb, wg, bg, wo, g, b, tm):
    npb, nsb = xp.shape[0] // tm, xs.shape[0] // tm
    est = ((N_BRANCH * BRANCH_W * D_MODEL + D_MODEL * N_BRANCH * D_MODEL + D_MODEL * D_MODEL) * 4
           + 2 * 2 * tm * D_MODEL * 4 * 2 + 2 * 2 * tm * Y_W * 2 + 8 * tm * D_MODEL * 4)
    return pl.pallas_call(
        functools.partial(_merge_kernel, npb),
        grid=(npb + nsb,),
        in_specs=_group_specs(tm, D_MODEL, npb, nsb) + _group_specs(tm, Y_W, npb, nsb)
                 + [_layer_spec(layer, N_BRANCH, BRANCH_W, D_MODEL), _layer_spec(layer, D_MODEL, N_BRANCH * D_MODEL),
                    _const_spec(bg.shape), _layer_spec(layer, D_MODEL, D_MODEL),
                    _const_spec(g.shape), _const_spec(b.shape)],
        compiler_params=pltpu.CompilerParams(dimension_semantics=("arbitrary",),
                                             vmem_limit_bytes=_vmem_limit(est)),
        name="merge",
        **_group_out(xp, xs, tm, npb, nsb),
    )(xp, xs, y_p, y_s, wb, wg, bg, wo, g, b)


FFN_CHUNK = 1024


def _ffn_kernel(npb, hp_ref, hs_ref, wu_ref, wd_ref, g_ref, b_ref, op_ref, os_ref):
    h = _group_load(npb, hp_ref, hs_ref)
    hb = h.astype(BF16)
    ff = None
    for j in range(0, D_FF, FFN_CHUNK):
        u = jnp.maximum(jnp.dot(hb, wu_ref[:, j:j + FFN_CHUNK].astype(BF16), preferred_element_type=F32), 0.0)
        part = jnp.dot((u * u).astype(BF16), wd_ref[j:j + FFN_CHUNK, :].astype(BF16), preferred_element_type=F32)
        ff = part if ff is None else ff + part
    _group_store(npb, op_ref, os_ref, _layer_norm(ALPHA * h + ff, g_ref[...], b_ref[...]))


def _ffn(hp, hs, layer, wu, wd, g, b, tm):
    npb, nsb = hp.shape[0] // tm, hs.shape[0] // tm
    est = (2 * D_MODEL * D_FF * 4 + 2 * 2 * 2 * tm * D_MODEL * 4 + tm * FFN_CHUNK * 6 + 4 * tm * D_MODEL * 4
           + 2 * D_MODEL * FFN_CHUNK * 2)
    return pl.pallas_call(
        functools.partial(_ffn_kernel, npb),
        grid=(npb + nsb,),
        in_specs=_group_specs(tm, D_MODEL, npb, nsb)
                 + [_layer_spec(layer, D_MODEL, D_FF), _layer_spec(layer, D_FF, D_MODEL),
                    _const_spec(g.shape), _const_spec(b.shape)],
        compiler_params=pltpu.CompilerParams(dimension_semantics=("arbitrary",),
                                             vmem_limit_bytes=_vmem_limit(est)),
        name="ffn",
        **_group_out(hp, hs, tm, npb, nsb),
    )(hp, hs, wu, wd, g, b)


def _pad_lanes(a):
    return jnp.pad(a, ((0, 0), (0, LANES - a.shape[1])))


def _position_tables(pos, c):
    half = HEAD_DIM // 2
    inv = ROPE_BASE ** (-jnp.arange(half, dtype=F32) / half)
    ang = pos[:, None] * inv[None, :]
    cos, sin = jnp.cos(ang), jnp.sin(ang)
    cos2 = jnp.concatenate([cos, cos], axis=1)
    sin2 = jnp.concatenate([-sin, sin], axis=1)
    log_gamma = jnp.log(1.0 - 2.0 ** (-5.0 - jnp.arange(N_HEADS, dtype=F32)))
    idx = jnp.arange(c, dtype=F32)
    rel = idx[:, None] - idx[None, :]
    dec = jnp.where(rel >= 0, jnp.exp(log_gamma[:, None, None] * jnp.maximum(rel, 0.0)), 0.0)
    inter = jnp.swapaxes(jnp.exp(log_gamma[:, None] * (idx + 1.0)[None, :]), 0, 1)
    to_end = jnp.swapaxes(jnp.exp(log_gamma[:, None] * (c - 1.0 - idx)[None, :]), 0, 1)
    cdec = jnp.broadcast_to(jnp.exp(log_gamma * c)[:, None], (N_HEADS, LANES))
    cdec = jnp.pad(cdec, ((0, SUBLANES - N_HEADS), (0, 0)))
    lanes = lambda a: jnp.broadcast_to(jnp.swapaxes(a, 0, 1)[:, :, None], (N_HEADS, c, LANES))
    return cos2, sin2, dec, lanes(inter), lanes(to_end), cdec


def _gla_tables(c):
    r = np.arange(c)[:, None]
    j = np.arange(c)[None, :]
    mats = [j <= r]
    n = c // 2
    while n > 1:
        start = (r // n) * n
        odd = (r // n) % 2 == 1
        mats.append(np.where(odd, (j >= start) & (j <= r), (j > r) & (j <= start + n - 1)))
        n //= 2
    x = np.maximum(r ^ j, 1)
    level = np.where(j > r, -1, np.where(j == r, 0, 1 << np.floor(np.log2(x)).astype(np.int64)))
    return jnp.asarray(np.tile(np.concatenate(mats, 0), (1, 3)), BF16), jnp.asarray(level, jnp.int32)


def kernel(x_prompt, x_sample, state_mlstm_C, state_mlstm_n, state_mlstm_m, state_ret, state_hgrn, state_conv,
           w_in, b_mlstm_gate, norm_a, norm_b, norm_c, lb_c, conv_w, w_branch, w_gate, b_gate, w_out,
           ln1_g, ln1_b, ln2_g, ln2_b, w_up, w_down):
    bp, tp, _ = x_prompt.shape
    bs, ts, _ = x_sample.shape
    n_p, n_s = bp * tp, bs * ts
    tm = 256
    tt = 256
    bb = 8
    assert n_p % tm == 0 and n_s % tm == 0 and tp % tt == 0 and bs % bb == 0 and n_p % (bb * ts) == 0

    xp = x_prompt.reshape(n_p, D_MODEL)
    xs = x_sample.reshape(n_s, D_MODEL)
    tabs_p = _position_tables(jnp.arange(tp, dtype=F32), math.gcd(tp, CHUNK))
    tabs_s = _position_tables(PAST_LEN + jnp.arange(ts, dtype=F32), math.gcd(ts, CHUNK))
    gla_p = _gla_tables(math.gcd(tp, CHUNK))
    gla_s = _gla_tables(math.gcd(ts, CHUNK))
    lbc = lb_c.astype(F32)
    row = lambda a: a.reshape(1, -1).astype(F32)
    assert w_in.shape[1:] == (D_MODEL, P_IN)
    w_in_t = jnp.swapaxes(w_in, 1, 2).astype(F32)
    wb, wg, wo, wu, wd = (w.astype(F32) for w in (w_branch, w_gate, w_out, w_up, w_down))
    st_in = (state_mlstm_C.astype(F32), state_mlstm_n.astype(F32),
             jnp.broadcast_to(state_mlstm_m.astype(F32)[..., None], (DEPTH, bs, N_HEADS, LANES)),
             state_ret.astype(F32), state_hgrn.astype(F32), state_conv.astype(F32))

    p_states, s_states = [], ()
    for l in range(DEPTH):
        prms = (_pad_lanes(row(b_mlstm_gate[l])), row(norm_a[l]), row(norm_b[l]), row(norm_c[l]), lbc,
                conv_w[l].astype(F32))
        z = _inproj(xp, xs, l, w_in_t, tm)
        outs_p = _mixer_prompt(z, bp, tp, tt, l, tabs_p, prms + gla_p)
        outs_s = _mixer_sample(z, n_p, bs, ts, bb, l, tabs_s, prms + gla_s, st_in, s_states)
        hp, hs = _merge(xp, xs, outs_p[0], outs_s[0], l, wb, wg, row(b_gate[l]), wo, row(ln1_g[l]), row(ln1_b[l]), tm)
        xp, xs = _ffn(hp, hs, l, wu, wd, row(ln2_g[l]), row(ln2_b[l]), tm)
        p_states.append(outs_p[1:])
        s_states = tuple(outs_s[1:])

    cs, ns, ms, srs, shs, bufs = zip(*p_states)
    p_out = (jnp.stack(cs), jnp.stack(ns), jnp.stack([m[..., 0] for m in ms]),
             jnp.stack(srs), jnp.stack(shs), jnp.stack(bufs))
    s_out = s_states[:2] + (s_states[2][..., 0],) + s_states[3:]
    return (xp.reshape(bp, tp, D_MODEL), xs.reshape(bs, ts, D_MODEL)) + p_out + s_out
```

```python
import functools
import math

import jax
import jax.numpy as jnp
import numpy as np
from jax import lax
from jax.experimental import pallas as pl
from jax.experimental.pallas import tpu as pltpu

F32 = jnp.float32
BF16 = jnp.bfloat16

D_MODEL = 1024
DEPTH = 2
BRANCH_W = D_MODEL // 2
N_HEADS = 4
HEAD_DIM = BRANCH_W // N_HEADS
CONV_W = 3
N_BRANCH = 4
D_FF = 4 * D_MODEL
CHUNK = 64
ROPE_BASE = 10000.0
LN_EPS = 1e-5
NEG_BIG = -1e30
GATE_CLAMP = 1.0 - 1e-6
ALPHA = (2 * DEPTH) ** 0.25
PAST_LEN = 16384
K_SCALE = HEAD_DIM ** -0.5

LANES = 128
SUBLANES = 8
VMEM_BYTES = 64 * 1024 * 1024

(QA, KA, VA, OA, QB, KB, VB, GB, QC, FC, IC, GC, XD, BGD, CGD) = [i * BRANCH_W for i in range(15)]
GATE_OFF = 15 * BRANCH_W
P_PAD = GATE_OFF + LANES
Y_W = N_BRANCH * BRANCH_W


def _vmem_limit(est_bytes):
    return int(min(est_bytes + est_bytes // 4 + (4 << 20), VMEM_BYTES - (6 << 20)))


def _dot(a, b):
    return jnp.dot(a.astype(BF16), b.astype(BF16), preferred_element_type=F32)


def _dot_nt(a, b):
    return lax.dot_general(a.astype(BF16), b.astype(BF16), (((1,), (1,)), ((), ())),
                           preferred_element_type=F32)


def _dot_tn(a, b):
    return lax.dot_general(a.astype(BF16), b.astype(BF16), (((0,), (0,)), ((), ())),
                           preferred_element_type=F32)


def _split3(x):
    hi = x.astype(BF16)
    r = x - hi.astype(F32)
    mid = r.astype(BF16)
    lo = (r - mid.astype(F32)).astype(BF16)
    return hi, mid, lo


def _dot_exact_rhs(a01x3, x):
    return jnp.dot(a01x3, _stack3(x), preferred_element_type=F32)


def _stack3(x):
    parts = _split3(x)
    if x.shape[0] % (2 * SUBLANES) == 0:
        return jnp.concatenate(parts, axis=0)
    return jnp.concatenate([p.astype(F32) for p in parts], axis=0).astype(BF16)


def _dot_nt_exact_rhs(a01, x):
    hi, mid, lo = _split3(x)
    d = lambda p: lax.dot_general(a01, p, (((1,), (1,)), ((), ())), preferred_element_type=F32)
    return (d(hi) + d(mid)) + d(lo)


def _silu(x):
    return x * jax.nn.sigmoid(x)


def _row_sum(x, c):
    if c <= SUBLANES:
        return jnp.sum(x, axis=1, keepdims=True)
    hi = x.astype(BF16)
    lo = (x - hi.astype(F32)).astype(BF16)
    ones = jnp.ones((2 * x.shape[1], LANES), BF16)
    return jnp.dot(jnp.concatenate([hi, lo], axis=1), ones, preferred_element_type=F32)


def _group_specs(tm, width, npb, nsb):
    return [pl.BlockSpec((tm, width), lambda i: (jnp.minimum(i, npb - 1), 0)),
            pl.BlockSpec((tm, width), lambda i: (jnp.clip(i - npb, 0, nsb - 1), 0))]


def _group_load(npb, p_ref, s_ref):
    return jnp.where(pl.program_id(0) < npb, p_ref[...], s_ref[...])


def _group_store(npb, p_ref, s_ref, val):
    is_p = pl.program_id(0) < npb

    @pl.when(is_p)
    def _():
        p_ref[...] = val

    @pl.when(jnp.logical_not(is_p))
    def _():
        s_ref[...] = val


P_IN = 15 * BRANCH_W + 2 * N_HEADS
GATE_FEATURES = (4 * BRANCH_W, 4 * BRANCH_W + 2 * N_HEADS)


def _inproj_kernel(npb, xp_ref, xs_ref, wt_ref, z_ref):
    xb = _group_load(npb, xp_ref, xs_ref).astype(BF16)
    nt = lambda w: lax.dot_general(xb, w.astype(BF16), (((1,), (1,)), ((), ())), preferred_element_type=F32)
    g0, g1 = GATE_FEATURES
    for off in range(0, GATE_OFF, BRANCH_W):
        src = off if off < g0 else off + (g1 - g0)
        z_ref[:, off:off + BRANCH_W] = nt(wt_ref[src:src + BRANCH_W, :])
    gates = nt(wt_ref[g0:g1, :])
    z_ref[:, GATE_OFF:P_PAD] = jnp.concatenate(
        [gates, jnp.zeros((gates.shape[0], P_PAD - GATE_OFF - (g1 - g0)), F32)], axis=1)


def _inproj(xp, xs, layer, wt, tm):
    npb, nsb = xp.shape[0] // tm, xs.shape[0] // tm
    est = D_MODEL * P_IN * 4 + 2 * tm * P_PAD * 4 + 2 * 2 * tm * D_MODEL * 4
    return pl.pallas_call(
        functools.partial(_inproj_kernel, npb),
        grid=(npb + nsb,),
        in_specs=_group_specs(tm, D_MODEL, npb, nsb) + [_layer_spec(layer, P_IN, D_MODEL)],
        out_specs=pl.BlockSpec((tm, P_PAD), lambda i: (i, 0)),
        out_shape=jax.ShapeDtypeStruct(((npb + nsb) * tm, P_PAD), F32),
        compiler_params=pltpu.CompilerParams(dimension_semantics=("arbitrary",),
                                             vmem_limit_bytes=_vmem_limit(est)),
        name="inproj",
    )(xp, xs, wt)


class _State:
    def __init__(self, get, put, hgrn_transposed):
        self.get = get
        self.put = put
        self.hgrn_transposed = hgrn_transposed


LEVEL1_DONE, STATE_STORED = "level 1 done", "state stored"


def _advance(g, until=None):
    for label in g:
        if until is not None and label == until:
            return


def _pipelined(chunk_generators):
    pending_tail = None
    for g in chunk_generators:
        _advance(g, LEVEL1_DONE)
        if pending_tail is not None:
            _advance(pending_tail)
        _advance(g, STATE_STORED)
        pending_tail = g
    _advance(pending_tail)


def _mixer_chunk(zc, c, st, cst, prm, put_y):
    tril, causal, sel8 = cst["tril"], cst["causal"], cst["sel8"]
    old = {(name, h): st.get(name, h) for name in ("C", "n", "m", "Sr", "ShT") for h in range(N_HEADS)}
    old[("buf", 0)] = st.get("buf", 0)
    new = {}

    heads = range(N_HEADS)
    hd = lambda off, h: zc(off + h * HEAD_DIM, HEAD_DIM)
    hsl = lambda a, h: a[:, h * HEAD_DIM:(h + 1) * HEAD_DIM]
    cos2, sin2 = cst["cos2"], cst["sin2"]
    rope = lambda x: x * cos2 + pltpu.roll(x, HEAD_DIM // 2, axis=1) * sin2
    lv = cst["gla_lv"][...]

    gp = zc(GATE_OFF, LANES) + prm["bias_if"]
    lf = jnp.minimum(gp, 0.0) - jnp.log1p(jnp.exp(-jnp.abs(gp)))
    b_all = _dot_exact_rhs(tril, lf)
    gp_t = _dot_nt_exact_rhs(sel8, gp)
    s_a, qc_a, s_b, qs_b, a_c = [], [], [], [], []
    for h in heads:
        qa = hd(QA, h)
        s_a.append(_dot_nt(qa, hd(KA, h) * K_SCALE))
        qc_a.append(_dot(qa, old[("C", h)]))
    for h in heads:
        qb = rope(hd(QB, h))
        kb = rope(hd(KB, h)) * K_SCALE
        s_b.append(_dot_nt(qb, kb))
        qs_b.append(_dot(qb, old[("Sr", h)]))
        new[("Sr", h)] = (cst["ret_cdec"][h:h + 1, :] * old[("Sr", h)]
                          + _dot_tn(kb * cst["ret_toend"](h), hd(VB, h)))
    kc_all = (1.0 - prm["lb"]) * jax.nn.sigmoid(-zc(FC, BRANCH_W))
    g_all = jnp.log1p(-jnp.minimum(kc_all, GATE_CLAMP))
    xs = _dot_exact_rhs(cst["gla_m"][...], g_all)
    bc_all = xs[0:c]
    qg_all = _silu(zc(QC, BRANCH_W))
    for h in heads:
        qg, kc = hsl(qg_all, h), hsl(kc_all, h)
        a = jnp.where(lv == 0, _dot_nt(qg, kc), 0.0)
        a_c.append(jnp.where(lv == 1, _dot_nt(qg * jnp.exp(hsl(g_all, h)), kc), a))

    yield LEVEL1_DONE
    b_t = _dot_nt_exact_rhs(sel8, b_all)
    qn_a = [jnp.sum(hd(QA, h) * old[("n", h)], axis=1, keepdims=True) for h in heads]
    o_b = []
    for h in heads:
        o_b.append(_dot(s_b[h] * cst["ret_dec"](h), hd(VB, h)) + qs_b[h] * cst["ret_inter"](h))
    num_a, den_a, oi_c = [], [], []
    for h in heads:
        b_col = b_all[:, N_HEADS + h:N_HEADS + h + 1]
        b_row = b_t[N_HEADS + h:N_HEADS + h + 1, :]
        a_col = b_col + old[("m", h)][:, 0:1]
        dmat = jnp.where(causal, (b_col - b_row) + gp_t[h:h + 1, :], NEG_BIG)
        m_t = jnp.maximum(a_col, jnp.max(dmat, axis=1, keepdims=True))
        w_inter = jnp.exp(a_col - m_t)
        s = s_a[h] * jnp.exp(dmat - m_t)
        num_a.append(_dot(s, hd(VA, h)) + qc_a[h] * w_inter)
        den = jnp.sum(s, axis=1, keepdims=True) + qn_a[h] * w_inter
        den_a.append(jnp.maximum(jnp.abs(den), jnp.exp(-m_t)))
    for h in heads:
        b_col = b_all[:, N_HEADS + h:N_HEADS + h + 1]
        m_prev = old[("m", h)][:, 0:1]
        b_last = b_col[c - 1:c, :]
        g_col = (b_last - b_col) + gp[:, h:h + 1]
        m_new = jnp.maximum(b_last + m_prev, jnp.max(g_col, axis=0, keepdims=True))
        dec = jnp.exp(b_last + m_prev - m_new)
        kw = (hd(KA, h) * K_SCALE) * jnp.exp(g_col - m_new)
        new[("C", h)] = dec * old[("C", h)] + _dot_tn(kw, hd(VA, h))
        new[("n", h)] = dec * old[("n", h)] + jnp.sum(kw, axis=0, keepdims=True)
        new[("m", h)] = jnp.broadcast_to(m_new, (1, LANES))
    for h in heads:
        qg, kc, bc = hsl(qg_all, h), hsl(kc_all, h), hsl(bc_all, h)
        a, sec, n = a_c[h], 1, c // 2
        while n > 1:
            e = jnp.exp(hsl(xs[sec * c:(sec + 1) * c], h))
            a = jnp.where(lv == n, _dot_nt(qg * e, kc * e), a)
            sec, n = sec + 1, n // 2
        a_c[h] = a
        sh = old[("ShT", h)]
        bc_last = bc[c - 1:c, :]
        k_end = kc * jnp.exp(bc_last - bc)
        if st.hgrn_transposed:
            oi_c.append(_dot_nt(qg * jnp.exp(bc), sh))
            new[("ShT", h)] = jnp.exp(bc_last) * sh + _dot_tn(hd(IC, h), k_end)
        else:
            oi_c.append(_dot(qg * jnp.exp(bc), sh))
            ones = jnp.ones((3 * c, HEAD_DIM), BF16)
            log_dec = lax.dot_general(_stack3(hsl(g_all, h)), ones, (((0,), (0,)), ((), ())),
                                      preferred_element_type=F32)
            new[("ShT", h)] = jnp.exp(log_dec) * sh + _dot_tn(k_end, hd(IC, h))
    o_c = [_dot(a_c[h], hd(IC, h)) + oi_c[h] for h in heads]
    u = zc(CGD, BRANCH_W) * zc(XD, BRANCH_W)
    new[("buf", 0)] = u[c - 2:c, :]
    for (name, h), val in new.items():
        st.put(name, h, val)

    yield STATE_STORED
    ya, yb, yc = [], [], []
    lane_mean = lambda x: _row_sum(x, c) * (1.0 / x.shape[1])
    for h in heads:
        hh = num_a[h] / den_a[h]
        hn = hh * lax.rsqrt(lane_mean(hh * hh) + LN_EPS)
        ya.append(jax.nn.sigmoid(hd(OA, h)) * (hn * hsl(prm["norm_a"], h)))
    put_y(0, jnp.concatenate(ya, axis=1))
    buf = old[("buf", 0)]
    row = lax.broadcasted_iota(jnp.int32, (c, BRANCH_W), 0)
    u1 = jnp.where(row == 0, buf[1:2, :], pltpu.roll(u, 1, axis=0))
    u2 = jnp.where(row == 0, buf[0:1, :], jnp.where(row == 1, buf[1:2, :], pltpu.roll(u, 2, axis=0)))
    cw = prm["conv_w"]
    yconv = (u2 * cw[0:1, :] + u1 * cw[1:2, :]) + u * cw[2:3, :]
    put_y(3 * BRANCH_W, zc(BGD, BRANCH_W) * yconv)
    for h in heads:
        oc = o_b[h] - lane_mean(o_b[h])
        on = oc * lax.rsqrt(lane_mean(oc * oc) + LN_EPS)
        yb.append(_silu(hd(GB, h)) * (on * hsl(prm["norm_b"], h)))
    put_y(BRANCH_W, jnp.concatenate(yb, axis=1))
    for h in heads:
        on = o_c[h] * lax.rsqrt(lane_mean(o_c[h] * o_c[h]) + LN_EPS)
        yc.append(_silu(hd(GC, h)) * (on * hsl(prm["norm_c"], h)))
    put_y(2 * BRANCH_W, jnp.concatenate(yc, axis=1))


def _chunk_consts(c):
    r = lax.broadcasted_iota(jnp.int32, (c, c), 0)
    l = lax.broadcasted_iota(jnp.int32, (c, c), 1)
    causal = r >= l
    r3 = lax.broadcasted_iota(jnp.int32, (c, 3 * c), 0)
    l3 = lax.broadcasted_iota(jnp.int32, (c, 3 * c), 1)
    l3 = jnp.where(l3 >= 2 * c, l3 - 2 * c, jnp.where(l3 >= c, l3 - c, l3))
    r8 = lax.broadcasted_iota(jnp.int32, (SUBLANES, LANES), 0)
    l8 = lax.broadcasted_iota(jnp.int32, (SUBLANES, LANES), 1)
    return {"causal": causal,
            "tril": jnp.where(r3 >= l3, 1.0, 0.0).astype(BF16),
            "sel8": jnp.where(r8 == l8, 1.0, 0.0).astype(BF16)}


def _layer_lb(lbc, layer):
    e = jnp.exp(lbc - jnp.max(lbc, axis=0, keepdims=True))
    sm = e / jnp.sum(e, axis=0, keepdims=True)
    cum = sm[0:1, :]
    for j in range(1, layer + 1):
        cum = cum + sm[j:j + 1, :]
    return cum - sm[0:1, :]


def _load_params(bias_ref, na_ref, nb_ref, nc_ref, lbc_ref, cw_ref, layer):
    return {"bias_if": bias_ref[...], "norm_a": na_ref[...], "norm_b": nb_ref[...], "norm_c": nc_ref[...],
            "lb": _layer_lb(lbc_ref[...], layer), "conv_w": cw_ref[...]}


PROMPT_LOCKSTEP = 1
Z_BUFS = 3


def _mixer_prompt_kernel(layer, c, tt, *refs):
    ns = PROMPT_LOCKSTEP
    (z_hbm, cos_ref, sin_ref, rdec_ref, rinter_ref, rtoend_ref, rcdec_ref,
     bias_ref, na_ref, nb_ref, nc_ref, lbc_ref, cw_ref, gm_ref, lv_ref,
     y_ref, c_out, n_out, m_out, sr_out, sh_out, buf_out,
     c_s, n_s, m_s, sr_s, sht_s, buf_s, zbuf, zsem) = refs
    j = pl.program_id(1)

    step = pl.program_id(0) * pl.num_programs(1) + j
    n_steps = pl.num_programs(0) * pl.num_programs(1)

    def z_copy(blk):
        slot = lax.rem(blk, Z_BUFS)
        return pltpu.make_async_copy(z_hbm.at[pl.ds(pl.multiple_of(blk * tt, tt), tt), :], zbuf.at[slot],
                                     zsem.at[slot])

    @pl.when(step == 0)
    def _():
        for blk in range(Z_BUFS - 1):
            z_copy(blk).start()

    @pl.when(step + (Z_BUFS - 1) < n_steps)
    def _():
        z_copy(step + (Z_BUFS - 1)).start()

    z_copy(step).wait()
    z_refs = [zbuf.at[lax.rem(step, Z_BUFS)]]

    @pl.when(j == 0)
    def _():
        c_s[...] = jnp.zeros_like(c_s)
        n_s[...] = jnp.zeros_like(n_s)
        m_s[...] = jnp.zeros_like(m_s)
        sr_s[...] = jnp.zeros_like(sr_s)
        sht_s[...] = jnp.zeros_like(sht_s)
        buf_s[...] = jnp.zeros_like(buf_s)

    prm = _load_params(bias_ref, na_ref, nb_ref, nc_ref, lbc_ref, cw_ref, layer)
    cst = _chunk_consts(c)
    cst["ret_dec"] = lambda h: rdec_ref[h]
    cst["ret_inter"] = lambda h: rinter_ref[h]
    cst["ret_toend"] = lambda h: rtoend_ref[h]
    cst["ret_cdec"] = rcdec_ref[...]
    cst["gla_m"] = gm_ref
    cst["gla_lv"] = lv_ref

    def state(s):
        def get(name, h):
            if name == "C":
                return c_s[s, h]
            if name == "n":
                return n_s[s, h:h + 1, :]
            if name == "m":
                return m_s[s, h:h + 1, :]
            if name == "Sr":
                return sr_s[s, h]
            if name == "ShT":
                return sht_s[s, h]
            return buf_s[s, 0:CONV_W - 1, :]

        def put(name, h, val):
            if name == "C":
                c_s[s, h] = val
            elif name == "n":
                n_s[s, h:h + 1, :] = val
            elif name == "m":
                m_s[s, h:h + 1, :] = val
            elif name == "Sr":
                sr_s[s, h] = val
            elif name == "ShT":
                sht_s[s, h] = val
            else:
                buf_s[s, 0:CONV_W - 1, :] = val

        return _State(get, put, hgrn_transposed=True)

    def chunk(s, k):
        r = k * c
        cc = dict(cst)
        cc["cos2"] = cos_ref[r:r + c, :]
        cc["sin2"] = sin_ref[r:r + c, :]
        zc = lambda off, w: z_refs[s][r:r + c, off:off + w]

        def put_y(off, val):
            y_ref[s, r:r + c, off:off + BRANCH_W] = val.astype(BF16)

        return _mixer_chunk(zc, c, state(s), cc, prm, put_y)

    for s in range(ns):
        _pipelined(chunk(s, k) for k in range(tt // c))

    @pl.when(j == pl.num_programs(1) - 1)
    def _():
        c_out[...] = c_s[...]
        n_out[...] = n_s[:, 0:N_HEADS, :]
        m_out[...] = m_s[:, 0:N_HEADS, :]
        sr_out[...] = sr_s[...]
        for s in range(ns):
            for h in range(N_HEADS):
                sh_out[s, h] = sht_s[s, h].T
        buf_out[...] = buf_s[:, 0:CONV_W - 1, :]


def _const_spec(shape):
    nd = len(shape)
    return pl.BlockSpec(shape, lambda *_: (0,) * nd)


def _mixer_prompt(z, bsz, t, tt, layer, tabs, prms):
    c = math.gcd(t, CHUNK)
    nj = t // tt
    ns = PROMPT_LOCKSTEP
    assert bsz % ns == 0
    cos2, sin2, rdec, rinter, rtoend, rcdec = tabs
    state_shapes = [
        jax.ShapeDtypeStruct((bsz, N_HEADS, HEAD_DIM, HEAD_DIM), F32),
        jax.ShapeDtypeStruct((bsz, N_HEADS, HEAD_DIM), F32),
        jax.ShapeDtypeStruct((bsz, N_HEADS, LANES), F32),
        jax.ShapeDtypeStruct((bsz, N_HEADS, HEAD_DIM, HEAD_DIM), F32),
        jax.ShapeDtypeStruct((bsz, N_HEADS, HEAD_DIM, HEAD_DIM), F32),
        jax.ShapeDtypeStruct((bsz, CONV_W - 1, BRANCH_W), F32),
    ]
    big = pl.BlockSpec((ns, N_HEADS, HEAD_DIM, HEAD_DIM), lambda g, j: (g, 0, 0, 0))
    small = pl.BlockSpec((ns, N_HEADS, LANES), lambda g, j: (g, 0, 0))
    assert ns == 1
    est = (Z_BUFS * tt * P_PAD * 4 + 2 * ns * tt * Y_W * 2 + 10 * ns * N_HEADS * HEAD_DIM * HEAD_DIM * 4 + (6 << 20))

    outs = pl.pallas_call(
        functools.partial(_mixer_prompt_kernel, layer, c, tt),
        grid=(bsz // ns, nj),
        in_specs=[pl.BlockSpec(memory_space=pl.ANY)]
                 + [pl.BlockSpec((tt, LANES), lambda g, j: (j, 0)),
                    pl.BlockSpec((tt, LANES), lambda g, j: (j, 0)),
                    _const_spec(rdec.shape), _const_spec(rinter.shape), _const_spec(rtoend.shape),
                    _const_spec(rcdec.shape)] + [_const_spec(p.shape) for p in prms],
        out_specs=[pl.BlockSpec((None, ns, tt, Y_W), lambda g, j: (g, 0, j, 0)),
                   big, small, small, big, big,
                   pl.BlockSpec((ns, CONV_W - 1, BRANCH_W), lambda g, j: (g, 0, 0))],
        out_shape=[jax.ShapeDtypeStruct((bsz // ns, ns, t, Y_W), BF16)] + state_shapes,
        scratch_shapes=[pltpu.VMEM((ns, N_HEADS, HEAD_DIM, HEAD_DIM), F32),
                        pltpu.VMEM((ns, SUBLANES, LANES), F32),
                        pltpu.VMEM((ns, SUBLANES, LANES), F32),
                        pltpu.VMEM((ns, N_HEADS, HEAD_DIM, HEAD_DIM), F32),
                        pltpu.VMEM((ns, N_HEADS, HEAD_DIM, HEAD_DIM), F32),
                        pltpu.VMEM((ns, SUBLANES, BRANCH_W), F32),
                        pltpu.VMEM((Z_BUFS, tt, P_PAD), F32),
                        pltpu.SemaphoreType.DMA((Z_BUFS,))],
        compiler_params=pltpu.CompilerParams(dimension_semantics=("arbitrary", "arbitrary"),
                                             vmem_limit_bytes=_vmem_limit(est)),
        name=f"mixer_prompt_l{layer}",
    )(z, cos2, sin2, rdec, rinter, rtoend, rcdec, *prms)
    return [outs[0].reshape(bsz * t, Y_W)] + list(outs[1:])


N_STATES = 6
N_MIXER_CONST_INPUTS = 15


def _mixer_sample_kernel(layer, c, bb, n_carried, *refs):
    (z_ref, cos_ref, sin_ref, rdec_ref, rinter_ref, rtoend_ref, rcdec_ref,
     bias_ref, na_ref, nb_ref, nc_ref, lbc_ref, cw_ref, gm_ref, lv_ref) = refs[:N_MIXER_CONST_INPUTS]
    c_in, n_in, m_in, sr_in, sh_in, buf_in = refs[N_MIXER_CONST_INPUTS:N_MIXER_CONST_INPUTS + N_STATES]
    y_ref, c_out, n_out, m_out, sr_out, sh_out, buf_out, ybuf = refs[N_MIXER_CONST_INPUTS + N_STATES + n_carried:]
    prm = _load_params(bias_ref, na_ref, nb_ref, nc_ref, lbc_ref, cw_ref, layer)
    cst = _chunk_consts(c)
    cst["ret_dec"] = lambda h: rdec_ref[h]
    cst["ret_inter"] = lambda h: rinter_ref[h]
    cst["ret_toend"] = lambda h: rtoend_ref[h]
    cst["ret_cdec"] = rcdec_ref[...]
    cst["gla_m"] = gm_ref
    cst["gla_lv"] = lv_ref
    cst["cos2"] = cos_ref[...]
    cst["sin2"] = sin_ref[...]

    def sequence(i):
        r = i * c
        zc = lambda off, w: z_ref[r:r + c, off:off + w]

        def get(name, h):
            if name == "C":
                return c_in[i, h]
            if name == "n":
                return n_in[i, pl.ds(h, 1), :]
            if name == "m":
                return m_in[i, pl.ds(h, 1), :]
            if name == "Sr":
                return sr_in[i, h]
            if name == "ShT":
                return sh_in[i, h]
            return buf_in[i]

        def put(name, h, val):
            if name == "C":
                c_out[i, h] = val
            elif name == "n":
                n_out[i, pl.ds(h, 1), :] = val
            elif name == "m":
                m_out[i, pl.ds(h, 1), :] = val
            elif name == "Sr":
                sr_out[i, h] = val
            elif name == "ShT":
                sh_out[i, h] = val
            else:
                buf_out[i] = val

        def put_y(off, val):
            ybuf[r:r + c, off:off + BRANCH_W] = val

        return _mixer_chunk(zc, c, _State(get, put, hgrn_transposed=False), cst, prm, put_y)

    _pipelined(sequence(i) for i in range(bb))
    y_ref[...] = ybuf[...].astype(BF16)


def _mixer_sample(z, row0, bsz, t, bb, layer, tabs, prms, states, carried):
    c = t
    rows = bb * t
    blk0 = row0 // rows
    cos2, sin2, rdec, rinter, rtoend, rcdec = tabs

    def lspec(*tail):
        zeros = (0,) * (len(tail) - 1)
        return pl.BlockSpec((None,) + tail, lambda i: (layer, i) + zeros)

    big = lspec(bb, N_HEADS, HEAD_DIM, HEAD_DIM)
    small = lspec(bb, N_HEADS, LANES)
    state_specs = [big, small, small, big, big, lspec(bb, CONV_W - 1, BRANCH_W)]
    state_shapes = [jax.ShapeDtypeStruct(s.shape, F32) for s in states]
    n_in = N_MIXER_CONST_INPUTS + N_STATES
    est = (2 * rows * P_PAD * 4 + 2 * rows * Y_W * 2 + rows * Y_W * 4
           + 2 * 2 * 3 * bb * N_HEADS * HEAD_DIM * HEAD_DIM * 4)
    return pl.pallas_call(
        functools.partial(_mixer_sample_kernel, layer, c, bb, len(carried)),
        grid=(bsz // bb,),
        in_specs=[pl.BlockSpec((rows, P_PAD), lambda i: (blk0 + i, 0)),
                  _const_spec(cos2.shape), _const_spec(sin2.shape),
                  _const_spec(rdec.shape), _const_spec(rinter.shape), _const_spec(rtoend.shape),
                  _const_spec(rcdec.shape)] + [_const_spec(p.shape) for p in prms] + state_specs
                 + [pl.BlockSpec(memory_space=pl.ANY)] * len(carried),
        out_specs=[pl.BlockSpec((rows, Y_W), lambda i: (i, 0))] + state_specs,
        out_shape=[jax.ShapeDtypeStruct((bsz * t, Y_W), BF16)] + state_shapes,
        input_output_aliases={n_in + k: 1 + k for k in range(len(carried))},
        scratch_shapes=[pltpu.VMEM((rows, Y_W), F32)],
        compiler_params=pltpu.CompilerParams(dimension_semantics=("arbitrary",),
                                             vmem_limit_bytes=_vmem_limit(est)),
        name=f"mixer_sample_l{layer}",
    )(z, cos2, sin2, rdec, rinter, rtoend, rcdec, *prms, *states, *carried)


def _layer_norm(r, g, b):
    mu = jnp.mean(r, axis=1, keepdims=True)
    rc = r - mu
    var = jnp.mean(rc * rc, axis=1, keepdims=True)
    return rc * lax.rsqrt(var + LN_EPS) * g + b


def _merge_kernel(npb, xp_ref, xs_ref, yp_ref, ys_ref, wb_ref, wg_ref, bg_ref, wo_ref, g_ref, b_ref,
                  hp_ref, hs_ref):
    x = _group_load(npb, xp_ref, xs_ref)
    xb = x.astype(BF16)
    y = _group_load(npb, yp_ref, ys_ref)
    acc = None
    for n in range(N_BRANCH):
        proj = jnp.dot(y[:, n * BRANCH_W:(n + 1) * BRANCH_W], wb_ref[n].astype(BF16), preferred_element_type=F32)
        gate = jax.nn.sigmoid(jnp.dot(xb, wg_ref[:, n * D_MODEL:(n + 1) * D_MODEL].astype(BF16),
                                      preferred_element_type=F32)
                              + bg_ref[:, n * D_MODEL:(n + 1) * D_MODEL])
        acc = gate * proj if acc is None else acc + gate * proj
    mix = jnp.dot(acc.astype(BF16), wo_ref[...].astype(BF16), preferred_element_type=F32)
    _group_store(npb, hp_ref, hs_ref, _layer_norm(ALPHA * x + mix, g_ref[...], b_ref[...]))


def _layer_spec(layer, *tail):
    return pl.BlockSpec((None,) + tail, lambda *_: (layer,) + (0,) * len(tail))


def _group_out(xp, xs, tm, npb, nsb):
    return dict(out_specs=_group_specs(tm, D_MODEL, npb, nsb),
                out_shape=[jax.ShapeDtypeStruct(xp.shape, F32), jax.ShapeDtypeStruct(xs.shape, F32)])


def _merge(xp, xs, y_p, y_s, layer, wb, wg, bg, wo, g, b, tm):
    npb, nsb = xp.shape[0] // tm, xs.shape[0] // tm
    est = ((N_BRANCH * BRANCH_W * D_MODEL + D_MODEL * N_BRANCH * D_MODEL + D_MODEL * D_MODEL) * 4
           + 2 * 2 * tm * D_MODEL * 4 * 2 + 2 * 2 * tm * Y_W * 2 + 8 * tm * D_MODEL * 4)
    return pl.pallas_call(
        functools.partial(_merge_kernel, npb),
        grid=(npb + nsb,),
        in_specs=_group_specs(tm, D_MODEL, npb, nsb) + _group_specs(tm, Y_W, npb, nsb)
                 + [_layer_spec(layer, N_BRANCH, BRANCH_W, D_MODEL), _layer_spec(layer, D_MODEL, N_BRANCH * D_MODEL),
                    _const_spec(bg.shape), _layer_spec(layer, D_MODEL, D_MODEL),
                    _const_spec(g.shape), _const_spec(b.shape)],
        compiler_params=pltpu.CompilerParams(dimension_semantics=("arbitrary",),
                                             vmem_limit_bytes=_vmem_limit(est)),
        name="merge",
        **_group_out(xp, xs, tm, npb, nsb),
    )(xp, xs, y_p, y_s, wb, wg, bg, wo, g, b)


FFN_CHUNK = 1024


def _ffn_kernel(npb, hp_ref, hs_ref, wu_ref, wd_ref, g_ref, b_ref, op_ref, os_ref):
    h = _group_load(npb, hp_ref, hs_ref)
    hb = h.astype(BF16)
    ff = None
    for j in range(0, D_FF, FFN_CHUNK):
        u = jnp.maximum(jnp.dot(hb, wu_ref[:, j:j + FFN_CHUNK].astype(BF16), preferred_element_type=F32), 0.0)
        part = jnp.dot((u * u).astype(BF16), wd_ref[j:j + FFN_CHUNK, :].astype(BF16), preferred_element_type=F32)
        ff = part if ff is None else ff + part
    _group_store(npb, op_ref, os_ref, _layer_norm(ALPHA * h + ff, g_ref[...], b_ref[...]))


def _ffn(hp, hs, layer, wu, wd, g, b, tm):
    npb, nsb = hp.shape[0] // tm, hs.shape[0] // tm
    est = (2 * D_MODEL * D_FF * 4 + 2 * 2 * 2 * tm * D_MODEL * 4 + tm * FFN_CHUNK * 6 + 4 * tm * D_MODEL * 4
           + 2 * D_MODEL * FFN_CHUNK * 2)
    return pl.pallas_call(
        functools.partial(_ffn_kernel, npb),
        grid=(npb + nsb,),
        in_specs=_group_specs(tm, D_MODEL, npb, nsb)
                 + [_layer_spec(layer, D_MODEL, D_FF), _layer_spec(layer, D_FF, D_MODEL),
                    _const_spec(g.shape), _const_spec(b.shape)],
        compiler_params=pltpu.CompilerParams(dimension_semantics=("arbitrary",),
                                             vmem_limit_bytes=_vmem_limit(est)),
        name="ffn",
        **_group_out(hp, hs, tm, npb, nsb),
    )(hp, hs, wu, wd, g, b)


def _pad_lanes(a):
    return jnp.pad(a, ((0, 0), (0, LANES - a.shape[1])))


def _position_tables(pos, c):
    half = HEAD_DIM // 2
    inv = ROPE_BASE ** (-jnp.arange(half, dtype=F32) / half)
    ang = pos[:, None] * inv[None, :]
    cos, sin = jnp.cos(ang), jnp.sin(ang)
    cos2 = jnp.concatenate([cos, cos], axis=1)
    sin2 = jnp.concatenate([-sin, sin], axis=1)
    log_gamma = jnp.log(1.0 - 2.0 ** (-5.0 - jnp.arange(N_HEADS, dtype=F32)))
    idx = jnp.arange(c, dtype=F32)
    rel = idx[:, None] - idx[None, :]
    dec = jnp.where(rel >= 0, jnp.exp(log_gamma[:, None, None] * jnp.maximum(rel, 0.0)), 0.0)
    inter = jnp.swapaxes(jnp.exp(log_gamma[:, None] * (idx + 1.0)[None, :]), 0, 1)
    to_end = jnp.swapaxes(jnp.exp(log_gamma[:, None] * (c - 1.0 - idx)[None, :]), 0, 1)
    cdec = jnp.broadcast_to(jnp.exp(log_gamma * c)[:, None], (N_HEADS, LANES))
    cdec = jnp.pad(cdec, ((0, SUBLANES - N_HEADS), (0, 0)))
    lanes = lambda a: jnp.broadcast_to(jnp.swapaxes(a, 0, 1)[:, :, None], (N_HEADS, c, LANES))
    return cos2, sin2, dec, lanes(inter), lanes(to_end), cdec


def _gla_tables(c):
    r = np.arange(c)[:, None]
    j = np.arange(c)[None, :]
    mats = [j <= r]
    n = c // 2
    while n > 1:
        start = (r // n) * n
        odd = (r // n) % 2 == 1
        mats.append(np.where(odd, (j >= start) & (j <= r), (j > r) & (j <= start + n - 1)))
        n //= 2
    x = np.maximum(r ^ j, 1)
    level = np.where(j > r, -1, np.where(j == r, 0, 1 << np.floor(np.log2(x)).astype(np.int64)))
    return jnp.asarray(np.tile(np.concatenate(mats, 0), (1, 3)), BF16), jnp.asarray(level, jnp.int32)


def kernel(x_prompt, x_sample, state_mlstm_C, state_mlstm_n, state_mlstm_m, state_ret, state_hgrn, state_conv,
           w_in, b_mlstm_gate, norm_a, norm_b, norm_c, lb_c, conv_w, w_branch, w_gate, b_gate, w_out,
           ln1_g, ln1_b, ln2_g, ln2_b, w_up, w_down):
    bp, tp, _ = x_prompt.shape
    bs, ts, _ = x_sample.shape
    n_p, n_s = bp * tp, bs * ts
    tm = 256
    tt = 256
    bb = 8
    assert n_p % tm == 0 and n_s % tm == 0 and tp % tt == 0 and bs % bb == 0 and n_p % (bb * ts) == 0

    xp = x_prompt.reshape(n_p, D_MODEL)
    xs = x_sample.reshape(n_s, D_MODEL)
    tabs_p = _position_tables(jnp.arange(tp, dtype=F32), math.gcd(tp, CHUNK))
    tabs_s = _position_tables(PAST_LEN + jnp.arange(ts, dtype=F32), math.gcd(ts, CHUNK))
    gla_p = _gla_tables(math.gcd(tp, CHUNK))
    gla_s = _gla_tables(math.gcd(ts, CHUNK))
    lbc = lb_c.astype(F32)
    row = lambda a: a.reshape(1, -1).astype(F32)
    assert w_in.shape[1:] == (D_MODEL, P_IN)
    w_in_t = jnp.swapaxes(w_in, 1, 2).astype(F32)
    wb, wg, wo, wu, wd = (w.astype(F32) for w in (w_branch, w_gate, w_out, w_up, w_down))
    st_in = (state_mlstm_C.astype(F32), state_mlstm_n.astype(F32),
             jnp.broadcast_to(state_mlstm_m.astype(F32)[..., None], (DEPTH, bs, N_HEADS, LANES)),
             state_ret.astype(F32), state_hgrn.astype(F32), state_conv.astype(F32))

    p_states, s_states = [], ()
    for l in range(DEPTH):
        prms = (_pad_lanes(row(b_mlstm_gate[l])), row(norm_a[l]), row(norm_b[l]), row(norm_c[l]), lbc,
                conv_w[l].astype(F32))
        z = _inproj(xp, xs, l, w_in_t, tm)
        outs_p = _mixer_prompt(z, bp, tp, tt, l, tabs_p, prms + gla_p)
        outs_s = _mixer_sample(z, n_p, bs, ts, bb, l, tabs_s, prms + gla_s, st_in, s_states)
        hp, hs = _merge(xp, xs, outs_p[0], outs_s[0], l, wb, wg, row(b_gate[l]), wo, row(ln1_g[l]), row(ln1_b[l]), tm)
        xp, xs = _ffn(hp, hs, l, wu, wd, row(ln2_g[l]), row(ln2_b[l]), tm)
        p_states.append(outs_p[1:])
        s_states = tuple(outs_s[1:])

    cs, ns, ms, srs, shs, bufs = zip(*p_states)
    p_out = (jnp.stack(cs), jnp.stack(ns), jnp.stack([m[..., 0] for m in ms]),
             jnp.stack(srs), jnp.stack(shs), jnp.stack(bufs))
    s_out = s_states[:2] + (s_states[2][..., 0],) + s_states[3:]
    return (xp.reshape(bp, tp, D_MODEL), xs.reshape(bs, ts, D_MODEL)) + p_out + s_out
```
